```python
import jax
import jax.numpy as jnp
from jax import lax
import numpy as np

D_MODEL = 1024
BATCH = 8
SEQ = 8192
DEPTH = 2

GRID_W = 64
CTX_LEN = 256
HEAD_DIM = 64
N_HEADS = D_MODEL // HEAD_DIM
N_KV_HEADS = N_HEADS // 4
Q_PER_KV = N_HEADS // N_KV_HEADS
D_ATTN = N_HEADS * HEAD_DIM
D_KV = N_KV_HEADS * HEAD_DIM
WINDOW = 128
QBLK = 128
ROPE_BASE = 10000.0
ROPE_AXIS_DIM = HEAD_DIM // 2
D_RNN = D_MODEL
RG_BLOCKS = 16
RG_BW = D_RNN // RG_BLOCKS
CONV_W = 4
CONV_LEFT = 2
RG_C = 8.0
D_FOUR = D_MODEL
F_GROUPS = 4
F_GW = D_FOUR // F_GROUPS
N_BRANCH = 3
Q_OFF = 0
K_OFF = Q_OFF + D_ATTN
V_OFF = K_OFF + D_KV
XR_OFF = V_OFF + D_KV
GR_OFF = XR_OFF + D_RNN
XF_OFF = GR_OFF + D_RNN
IN_COLS = XF_OFF + D_FOUR
N_EXPERTS = 32
TOP_K = 4
D_EXPERT = D_MODEL
SWIGLU_LIMIT = 7.0
SWIGLU_ALPHA = 1.702
MOE_BLK = 128
N_MOD = 6
LN_EPS = 1e-5
DEEPNORM_ALPHA = (2 * DEPTH) ** 0.25
DEEPNORM_BETA = (8 * DEPTH) ** -0.25
NEG_INF = -1e30

kernel_name = 'hybrid_fourier_rglru_swa_moe_dit_block'


def layer_norm(x, g, b):
    xf = x.astype(jnp.float32)
    xc = xf - jnp.mean(xf, axis=-1, keepdims=True)
    var = jnp.mean(xc * xc, axis=-1, keepdims=True)
    y = xc * lax.rsqrt(var + LN_EPS) * g.astype(jnp.float32) + b.astype(jnp.float32)
    return y.astype(x.dtype)


def adaln(cond, w_mod, b_mod):
    m = jax.nn.silu(cond) @ w_mod + b_mod
    m = m.reshape(m.shape[:-1] + (N_MOD, D_MODEL))
    return tuple(jnp.expand_dims(m[..., i, :], -2) for i in range(N_MOD))


def modulate(x, shift, scale):
    return x * (1 + scale) + shift


def axial_rope_tables(rows):
    row = jnp.repeat(jnp.arange(rows, dtype=jnp.float32), GRID_W)
    col = jnp.tile(jnp.arange(GRID_W, dtype=jnp.float32), rows)
    n_freq = ROPE_AXIS_DIM // 2
    freqs = ROPE_BASE ** (-jnp.arange(n_freq, dtype=jnp.float32) / n_freq)
    ang_r = (row[:, None] * freqs)[:, None, :]
    ang_c = (col[:, None] * freqs)[:, None, :]
    return (jnp.cos(ang_r), jnp.sin(ang_r), jnp.cos(ang_c), jnp.sin(ang_c))


def rope_rotate(x, cos, sin):
    x1, x2 = jnp.split(x, 2, axis=-1)
    cos = cos.astype(x.dtype)
    sin = sin.astype(x.dtype)
    return jnp.concatenate([x1 * cos - x2 * sin, x2 * cos + x1 * sin], axis=-1)


def apply_axial_rope(x, rope):
    cos_r, sin_r, cos_c, sin_c = rope
    x_row, x_col = jnp.split(x, 2, axis=-1)
    return jnp.concatenate([rope_rotate(x_row, cos_r, sin_r), rope_rotate(x_col, cos_c, sin_c)], axis=-1)


def window_attention(q, k, v, kc, vc, sink):
    bsz, seq_len = q.shape[:2]
    nb = seq_len // QBLK
    span = QBLK + 2 * WINDOW
    scale = HEAD_DIM ** -0.5
    qb = (q * scale).reshape(bsz, nb, QBLK, N_KV_HEADS, Q_PER_KV, HEAD_DIM).swapaxes(0, 1)
    pad = ((0, 0), (WINDOW, WINDOW), (0, 0), (0, 0))
    kp = jnp.pad(k, pad)
    vp = jnp.pad(v, pad)
    sink_logit = sink.astype(jnp.float32).reshape(1, N_KV_HEADS, Q_PER_KV, 1, 1)

    def block(args):
        n, qn = args
        kn = lax.dynamic_slice_in_dim(kp, n * QBLK, span, axis=1)
        vn = lax.dynamic_slice_in_dim(vp, n * QBLK, span, axis=1)
        qpos = n * QBLK + jnp.arange(QBLK)
        kpos = n * QBLK - WINDOW + jnp.arange(span)
        valid = (jnp.abs(qpos[:, None] - kpos[None, :]) <= WINDOW) & (kpos >= 0)[None, :] & (kpos < seq_len)[None, :]
        s_loc = jnp.einsum('bqgrd,bkgd->bgrqk', qn, kn).astype(jnp.float32)
        s_loc = jnp.where(valid, s_loc, NEG_INF)
        s_ctx = jnp.einsum('bqgrd,bkgd->bgrqk', qn, kc).astype(jnp.float32)
        s_sink = jnp.broadcast_to(sink_logit, s_loc.shape[:-1] + (1,))
        prob = jax.nn.softmax(jnp.concatenate([s_loc, s_ctx, s_sink], axis=-1), axis=-1)
        p_loc = prob[..., :span].astype(v.dtype)
        p_ctx = prob[..., span:-1].astype(v.dtype)
        return jnp.einsum('bgrqk,bkgd->bqgrd', p_loc, vn) + jnp.einsum('bgrqk,bkgd->bqgrd', p_ctx, vc)

    out = lax.map(block, (jnp.arange(nb), qb))
    return out.swapaxes(0, 1).reshape(bsz, seq_len, D_ATTN)


def context_attention(qc, kc, vc, sink):
    bsz, ctx_len = qc.shape[:2]
    qg = (qc * HEAD_DIM ** -0.5).reshape(bsz, ctx_len, N_KV_HEADS, Q_PER_KV, HEAD_DIM)
    s = jnp.einsum('bqgrd,bkgd->bgrqk', qg, kc).astype(jnp.float32)
    s_sink = jnp.broadcast_to(sink.astype(jnp.float32).reshape(1, N_KV_HEADS, Q_PER_KV, 1, 1), s.shape[:-1] + (1,))
    prob = jax.nn.softmax(jnp.concatenate([s, s_sink], axis=-1), axis=-1)[..., :-1].astype(vc.dtype)
    return jnp.einsum('bgrqk,bkgd->bqgrd', prob, vc).reshape(bsz, ctx_len, D_ATTN)


def short_conv(x, w, b):
    seq_len = x.shape[1]
    xp = jnp.pad(x, ((0, 0), (CONV_LEFT, CONV_W - 1 - CONV_LEFT), (0, 0)))
    y = b
    for tap in range(CONV_W):
        y = y + xp[:, tap:tap + seq_len] * w[tap]
    return y


def rglru_coeffs(x, w_a, b_a, w_i, b_i, lam):
    bsz, seq_len, _ = x.shape
    xb = x.reshape(bsz, seq_len, RG_BLOCKS, RG_BW)
    r = jax.nn.sigmoid((jnp.einsum('blhi,hij->blhj', xb, w_a).reshape(bsz, seq_len, D_RNN) + b_a).astype(jnp.float32))
    ig = jax.nn.sigmoid((jnp.einsum('blhi,hij->blhj', xb, w_i).reshape(bsz, seq_len, D_RNN) + b_i).astype(jnp.float32))
    log_a = -RG_C * r * jax.nn.softplus(-lam.astype(jnp.float32))
    a = jnp.exp(log_a)
    b = jnp.sqrt(-jnp.expm1(2 * log_a)) * (ig * x.astype(jnp.float32))
    return a, b


def linear_scan(a, b, h0, reverse):
    first = -1 if reverse else 0
    b = b.at[:, first].add(a[:, first] * h0)

    def combine(lhs, rhs):
        a_l, b_l = lhs
        a_r, b_r = rhs
        return a_l * a_r, a_r * b_l + b_r

    _, h = lax.associative_scan(combine, (a, b), reverse=reverse, axis=1)
    return h


def rglru_bidir(x_lat, x_ctx, conv_w, conv_b, w_a, b_a, w_i, b_i, lam):
    xl = short_conv(x_lat, conv_w, conv_b)
    xc = short_conv(x_ctx, conv_w, conv_b)
    h_lat = jnp.zeros(xl.shape, jnp.float32)
    h_ctx = jnp.zeros(xc.shape, jnp.float32)
    for d, rev in ((0, False), (1, True)):
        a_c, b_c = rglru_coeffs(xc, w_a[d], b_a[d], w_i[d], b_i[d], lam[d])
        hc = linear_scan(a_c, b_c, jnp.zeros(b_c[:, 0].shape, jnp.float32), rev)
        h_end = hc[:, 0] if rev else hc[:, -1]
        a_l, b_l = rglru_coeffs(xl, w_a[d], b_a[d], w_i[d], b_i[d], lam[d])
        h_lat = h_lat + linear_scan(a_l, b_l, h_end, rev)
        h_ctx = h_ctx + hc
    return h_lat, h_ctx


def fourier_mix(xf):
    bsz, seq_len, _ = xf.shape
    xg = xf.astype(jnp.float32).reshape(bsz, seq_len, F_GROUPS, F_GW)
    y = jnp.fft.fft2(xg, axes=(1, 3), norm='ortho').real
    return y.reshape(bsz, seq_len, D_FOUR).astype(xf.dtype)


def branch_merge(u, y_attn, y_rg, y_four, w_merge, b_merge):
    g = jax.nn.sigmoid(u @ w_merge + b_merge).reshape(u.shape[:-1] + (N_BRANCH, D_MODEL))
    return g[..., 0, :] * y_attn + g[..., 1, :] * y_rg + g[..., 2, :] * y_four


def clamped_swiglu(h):
    x_glu, x_lin = jnp.split(h, 2, axis=-1)
    x_glu = jnp.minimum(x_glu, SWIGLU_LIMIT)
    x_lin = jnp.clip(x_lin, -SWIGLU_LIMIT, SWIGLU_LIMIT)
    return x_glu * jax.nn.sigmoid(SWIGLU_ALPHA * x_glu) * (x_lin + 1)


def moe_ffn(u, w_router, b_router, w_gu, b_gu, w_dn, b_dn):
    lead = u.shape[:-1]
    xt = u.reshape(-1, D_MODEL)
    n_tok = xt.shape[0]
    logits = (xt @ w_router + b_router).astype(jnp.float32)
    top_logit, top_e = lax.top_k(logits, TOP_K)
    gate = jax.nn.softmax(top_logit, axis=-1)
    n_asg = n_tok * TOP_K
    e_flat = top_e.reshape(n_asg)
    tok_flat = jnp.repeat(jnp.arange(n_tok, dtype=jnp.int32), TOP_K)
    g_flat = gate.reshape(n_asg)
    order = jnp.argsort(e_flat)
    e_s, tok_s, g_s = e_flat[order], tok_flat[order], g_flat[order]
    counts = jnp.bincount(e_flat, length=N_EXPERTS)
    padded = (counts + MOE_BLK - 1) // MOE_BLK * MOE_BLK
    start = jnp.cumsum(counts) - counts
    pend = jnp.cumsum(padded)
    pstart = pend - padded
    dest = pstart[e_s] + jnp.arange(n_asg, dtype=jnp.int32) - start[e_s]
    n_blocks = -(-n_asg // MOE_BLK) + N_EXPERTS
    n_slots = n_blocks * MOE_BLK
    slot_tok = jnp.full((n_slots,), n_tok, jnp.int32).at[dest].set(tok_s)
    slot_gate = jnp.zeros((n_slots,), jnp.float32).at[dest].set(g_s)
    block_e = jnp.minimum(jnp.searchsorted(pend, jnp.arange(n_blocks) * MOE_BLK, side='right'), N_EXPERTS - 1)
    x_pad = jnp.concatenate([xt, jnp.zeros((1, D_MODEL), xt.dtype)], axis=0)

    def expert_block(args):
        tok, e = args
        h = x_pad[tok] @ w_gu[e] + b_gu[e]
        return clamped_swiglu(h) @ w_dn[e] + b_dn[e]

    ys = lax.map(expert_block, (slot_tok.reshape(n_blocks, MOE_BLK), block_e))
    ys = ys.reshape(n_slots, D_MODEL).astype(jnp.float32) * slot_gate[:, None]
    out = jax.ops.segment_sum(ys, slot_tok, num_segments=n_tok + 1)[:n_tok]
    return out.astype(u.dtype).reshape(lead + (D_MODEL,))


def mixer_sublayer(u, u_c, rope, p, last):
    bsz, seq_len, _ = u.shape
    ctx_len = u_c.shape[1]
    w_in = p['w_in']
    z = u @ w_in
    q = apply_axial_rope(z[..., Q_OFF:K_OFF].reshape(bsz, seq_len, N_HEADS, HEAD_DIM), rope)
    k = apply_axial_rope(z[..., K_OFF:V_OFF].reshape(bsz, seq_len, N_KV_HEADS, HEAD_DIM), rope)
    v = z[..., V_OFF:XR_OFF].reshape(bsz, seq_len, N_KV_HEADS, HEAD_DIM)
    xr = z[..., XR_OFF:GR_OFF]
    gr = z[..., GR_OFF:XF_OFF]
    xf = z[..., XF_OFF:IN_COLS]
    kc = (u_c @ w_in[:, K_OFF:V_OFF]).reshape(bsz, ctx_len, N_KV_HEADS, HEAD_DIM)
    vc = (u_c @ w_in[:, V_OFF:XR_OFF]).reshape(bsz, ctx_len, N_KV_HEADS, HEAD_DIM)
    xrc = u_c @ w_in[:, XR_OFF:GR_OFF]
    h_lat, h_ctx = rglru_bidir(xr, xrc, p['conv_w'], p['conv_b'], p['rg_w_a'], p['rg_b_a'], p['rg_w_i'], p['rg_b_i'], p['rg_lambda'])
    y_attn = window_attention(q, k, v, kc, vc, p['attn_sink']) @ p['w_o_attn']
    y_rg = (jax.nn.gelu(gr) * h_lat.astype(gr.dtype)) @ p['w_o_rg']
    y_four = fourier_mix(xf) @ p['w_o_four']
    y = branch_merge(u, y_attn, y_rg, y_four, p['w_merge'], p['b_merge']) @ p['w_out']
    if last:
        return y, None
    qc = (u_c @ w_in[:, Q_OFF:K_OFF]).reshape(bsz, ctx_len, N_HEADS, HEAD_DIM)
    grc = u_c @ w_in[:, GR_OFF:XF_OFF]
    xfc = u_c @ w_in[:, XF_OFF:IN_COLS]
    yc_attn = context_attention(qc, kc, vc, p['attn_sink']) @ p['w_o_attn']
    yc_rg = (jax.nn.gelu(grc) * h_ctx.astype(grc.dtype)) @ p['w_o_rg']
    yc_four = fourier_mix(xfc) @ p['w_o_four']
    y_c = branch_merge(u_c, yc_attn, yc_rg, yc_four, p['w_merge'], p['b_merge']) @ p['w_out']
    return y, y_c


def trunk_layer(x, ctx, c, c_ctx, rope, p, last):
    sh1, sc1, g1, sh2, sc2, g2 = adaln(c, p['w_mod'], p['b_mod'])
    sh1c, sc1c, g1c, sh2c, sc2c, g2c = adaln(c_ctx, p['w_mod'], p['b_mod'])
    y, y_c = mixer_sublayer(modulate(x, sh1, sc1), modulate(ctx, sh1c, sc1c), rope, p, last)
    x = layer_norm(DEEPNORM_ALPHA * x + g1 * y, p['ln1_g'], p['ln1_b'])
    f = moe_ffn(modulate(x, sh2, sc2), p['w_router'], p['b_router'], p['w_gate_up'], p['b_gate_up'], p['w_down'], p['b_down'])
    x = layer_norm(DEEPNORM_ALPHA * x + g2 * f, p['ln2_g'], p['ln2_b'])
    if last:
        return x, None
    ctx = layer_norm(DEEPNORM_ALPHA * ctx + g1c * y_c, p['ln1_g'], p['ln1_b'])
    f_c = moe_ffn(modulate(ctx, sh2c, sc2c), p['w_router'], p['b_router'], p['w_gate_up'], p['b_gate_up'], p['w_down'], p['b_down'])
    ctx = layer_norm(DEEPNORM_ALPHA * ctx + g2c * f_c, p['ln2_g'], p['ln2_b'])
    return x, ctx


def setup_inputs(seed: int = 0) -> dict:
    key = jax.random.key(seed)
    keys = jax.random.split(key, 32)

    def nrm(i, shape, scale):
        return jax.random.normal(keys[i], shape, jnp.float32) * scale

    a_pow_c = jax.random.uniform(keys[15], (DEPTH, 2, D_RNN), jnp.float32, minval=0.9, maxval=0.999)
    a0 = a_pow_c ** (1.0 / RG_C)
    rg_lambda = jnp.log(a0) - jnp.log1p(-a0)
    return {
        'x': nrm(0, (BATCH, SEQ, D_MODEL), 1.0),
        'c': nrm(1, (BATCH, D_MODEL), 1.0),
        'ctx': nrm(2, (BATCH, CTX_LEN, D_MODEL), 1.0),
        'c_ctx': nrm(3, (D_MODEL,), 1.0),
        'w_mod': nrm(4, (DEPTH, D_MODEL, N_MOD * D_MODEL), 0.5 * D_MODEL ** -0.5),
        'b_mod': nrm(5, (DEPTH, N_MOD * D_MODEL), 0.02),
        'w_in': nrm(6, (DEPTH, D_MODEL, IN_COLS), D_MODEL ** -0.5),
        'attn_sink': nrm(7, (DEPTH, N_HEADS), 0.5),
        'w_o_attn': nrm(8, (DEPTH, D_ATTN, D_MODEL), D_ATTN ** -0.5),
        'conv_w': nrm(9, (DEPTH, CONV_W, D_RNN), CONV_W ** -0.5),
        'conv_b': nrm(10, (DEPTH, D_RNN), 0.02),
        'rg_w_a': nrm(11, (DEPTH, 2, RG_BLOCKS, RG_BW, RG_BW), RG_BW ** -0.5),
        'rg_b_a': nrm(12, (DEPTH, 2, D_RNN), 0.02),
        'rg_w_i': nrm(13, (DEPTH, 2, RG_BLOCKS, RG_BW, RG_BW), RG_BW ** -0.5),
        'rg_b_i': nrm(14, (DEPTH, 2, D_RNN), 0.02),
        'rg_lambda': rg_lambda,
        'w_o_rg': nrm(16, (DEPTH, D_RNN, D_MODEL), D_RNN ** -0.5),
        'w_o_four': nrm(17, (DEPTH, D_FOUR, D_MODEL), D_FOUR ** -0.5),
        'w_merge': nrm(18, (DEPTH, D_MODEL, N_BRANCH * D_MODEL), D_MODEL ** -0.5),
        'b_merge': nrm(19, (DEPTH, N_BRANCH * D_MODEL), 0.02),
        'w_out': nrm(20, (DEPTH, D_MODEL, D_MODEL), DEEPNORM_BETA * D_MODEL ** -0.5),
        'ln1_g': 1.0 + nrm(21, (DEPTH, D_MODEL), 0.02),
        'ln1_b': nrm(22, (DEPTH, D_MODEL), 0.02),
        'w_router': nrm(23, (DEPTH, D_MODEL, N_EXPERTS), D_MODEL ** -0.5),
        'b_router': nrm(24, (DEPTH, N_EXPERTS), 0.01),
        'w_gate_up': nrm(25, (DEPTH, N_EXPERTS, D_MODEL, 2 * D_EXPERT), D_MODEL ** -0.5),
        'b_gate_up': nrm(26, (DEPTH, N_EXPERTS, 2 * D_EXPERT), 0.02),
        'w_down': nrm(27, (DEPTH, N_EXPERTS, D_EXPERT, D_MODEL), DEEPNORM_BETA * D_EXPERT ** -0.5),
        'b_down': nrm(28, (DEPTH, N_EXPERTS, D_MODEL), 0.02),
        'ln2_g': 1.0 + nrm(29, (DEPTH, D_MODEL), 0.02),
        'ln2_b': nrm(30, (DEPTH, D_MODEL), 0.02),
    }


def reference(x, c, ctx, c_ctx, w_mod, b_mod, w_in, attn_sink, w_o_attn, conv_w, conv_b, rg_w_a, rg_b_a, rg_w_i, rg_b_i, rg_lambda, w_o_rg, w_o_four, w_merge, b_merge, w_out, ln1_g, ln1_b, w_router, b_router, w_gate_up, b_gate_up, w_down, b_down, ln2_g, ln2_b):
    ROWS = x.shape[1] // GRID_W
    rope = axial_rope_tables(ROWS)
    for l in range(DEPTH):
        p = dict(w_mod=w_mod[l], b_mod=b_mod[l], w_in=w_in[l], attn_sink=attn_sink[l], w_o_attn=w_o_attn[l],
                 conv_w=conv_w[l], conv_b=conv_b[l], rg_w_a=rg_w_a[l], rg_b_a=rg_b_a[l], rg_w_i=rg_w_i[l],
                 rg_b_i=rg_b_i[l], rg_lambda=rg_lambda[l], w_o_rg=w_o_rg[l], w_o_four=w_o_four[l],
                 w_merge=w_merge[l], b_merge=b_merge[l], w_out=w_out[l], ln1_g=ln1_g[l], ln1_b=ln1_b[l],
                 w_router=w_router[l], b_router=b_router[l], w_gate_up=w_gate_up[l], b_gate_up=b_gate_up[l],
                 w_down=w_down[l], b_down=b_down[l], ln2_g=ln2_g[l], ln2_b=ln2_b[l])
        x, ctx = trunk_layer(x, ctx, c, c_ctx, rope, p, l == DEPTH - 1)
    return x
```

```python
import functools
import math

import jax
import jax.numpy as jnp
from jax import lax
from jax.experimental import pallas as pl
from jax.experimental.pallas import tpu as pltpu

F32 = jnp.float32
BF16 = jnp.bfloat16

D = 1024
HEAD_DIM = 64
N_HEADS = 16
N_KV = 4
Q_PER_KV = 4
WINDOW = 128
QBLK = 128
GRID_W = 64
ROPE_BASE = 10000.0
RG_C = 8.0
CONV_LEFT = 2
F_GROUPS = 4
F_GW = 256
N_EXPERTS = 32
TOP_K = 4
SWIGLU_LIMIT = 7.0
SWIGLU_ALPHA = 1.702
LN_EPS = 1e-5
DEPTH = 2
DEEPNORM_ALPHA = (2 * DEPTH) ** 0.25
NEG_INF = -1e30

Q_OFF, K_OFF, V_OFF, XR_OFF, GR_OFF, XF_OFF, IN_COLS = 0, 1024, 1280, 1536, 2560, 3584, 4608
ZQ, ZXR, ZGR, ZXC, ZXS, ZK, ZV, Z_COLS = 0, 1024, 2048, 3072, 4096, 5120, 5376, 5632

VMEM_LIMIT_V7X = 56 * 1024 * 1024
LANES = 128
MXU_N = 256


def _params(sem, vmem=VMEM_LIMIT_V7X):
    return pltpu.CompilerParams(dimension_semantics=sem, vmem_limit_bytes=vmem)


def _resident(shape):
    nd = len(shape)
    return pl.BlockSpec(shape, lambda *_: (0,) * nd, pipeline_mode=pl.Buffered(1))


def _sigmoid(x):
    return 1.0 / (1.0 + jnp.exp(-x))


def _layer_norm(r, g, b):
    mu = jnp.mean(r, axis=-1, keepdims=True)
    rc = r - mu
    var = jnp.mean(rc * rc, axis=-1, keepdims=True)
    return rc * lax.rsqrt(var + LN_EPS) * g + b


def _adaln_body(c_ref, w_ref, b_ref, o_ref):
    c = c_ref[...]
    s = (c * _sigmoid(c)).astype(BF16)
    o_ref[...] = jnp.dot(s, w_ref[...].astype(BF16), preferred_element_type=F32) + b_ref[...]


def _adaln(cond, w_mod, b_mod):
    rows, n = cond.shape[0], w_mod.shape[1]
    tn = 1024
    return pl.pallas_call(
        _adaln_body,
        grid=(n // tn,),
        in_specs=[pl.BlockSpec((rows, D), lambda j: (0, 0)),
                  pl.BlockSpec((D, tn), lambda j: (0, j)),
                  pl.BlockSpec((1, tn), lambda j: (0, j))],
        out_specs=pl.BlockSpec((rows, tn), lambda j: (0, j)),
        out_shape=jax.ShapeDtypeStruct((rows, n), F32),
        compiler_params=_params(("arbitrary",)),
        name="adaln",
    )(cond, w_mod, b_mod.reshape(1, n))


def _proj_dst(j):
    if j < 4:
        return ZQ + j * MXU_N
    if j == 4:
        return ZK
    if j == 5:
        return ZV
    if j < 10:
        return ZXR + (j - 6) * MXU_N
    if j < 14:
        return ZGR + (j - 10) * MXU_N
    return None


def _proj_body(x_ref, mod_ref, w_ref, csw_ref, *rest, rope):
    if rope:
        c_ref, sa_ref, sb_ref, o_ref = rest
    else:
        (o_ref,) = rest
    mod = mod_ref[...]
    u = (x_ref[...] * (1.0 + mod[1:2]) + mod[0:1]).astype(BF16)
    for j in range(IN_COLS // MXU_N):
        acc = jnp.dot(u, w_ref[:, j * MXU_N:(j + 1) * MXU_N], preferred_element_type=F32)
        if rope and j < 5:
            acc = (acc * c_ref[...] + pltpu.roll(acc, MXU_N - 16, 1) * sa_ref[...]
                   + pltpu.roll(acc, 16, 1) * sb_ref[...])
        dst = _proj_dst(j)
        if dst is not None:
            o_ref[:, dst:dst + MXU_N] = acc.astype(BF16)
        else:
            g = j - 14
            t = jnp.dot(acc.astype(BF16), csw_ref[...], preferred_element_type=F32)
            o_ref[:, ZXC + g * F_GW:ZXC + (g + 1) * F_GW] = t[:, :F_GW].astype(BF16)
            o_ref[:, ZXS + g * F_GW:ZXS + (g + 1) * F_GW] = t[:, F_GW:].astype(BF16)


def _proj(x2, mods, w_in, csw, rope_tabs, seq_len, tm):
    n = x2.shape[0]
    tpb = seq_len // tm
    bm = mods.shape[0]
    mod_map = (lambda i: (i // tpb, 0, 0)) if bm > 1 else (lambda i: (0, 0, 0))
    in_specs = [pl.BlockSpec((tm, D), lambda i: (i, 0)),
                pl.BlockSpec((None, 6, D), mod_map),
                _resident((D, IN_COLS)),
                _resident((F_GW, 2 * F_GW))]
    args = [x2, mods, w_in, csw]
    if rope_tabs is not None:
        in_specs += [pl.BlockSpec((tm, MXU_N), lambda i: (i % tpb, 0))] * 3
        args += list(rope_tabs)
    return pl.pallas_call(
        functools.partial(_proj_body, rope=rope_tabs is not None),
        grid=(n // tm,),
        in_specs=in_specs,
        out_specs=pl.BlockSpec((tm, Z_COLS), lambda i: (i, 0)),
        out_shape=jax.ShapeDtypeStruct((n, Z_COLS), BF16),
        compiler_params=_params(("parallel",)),
        name="proj_rope" if rope_tabs is not None else "proj_ctx",
    )(*args)


def _softmax_pv(s_parts, v_parts, sink_col):
    m = sink_col
    for s in s_parts:
        m = jnp.maximum(m, jnp.max(s, axis=-1, keepdims=True))
    den = jnp.exp(sink_col - m)
    out = None
    for s, v in zip(s_parts, v_parts):
        p = jnp.exp(s - m)
        den = den + jnp.sum(p, axis=-1, keepdims=True)
        o = jnp.dot(p.astype(BF16), v, preferred_element_type=F32)
        out = o if out is None else out + o
    return out / den


def _sink_col(sink_ref, g, rows):
    return jnp.concatenate(
        [jnp.full((rows, 1), sink_ref[g * Q_PER_KV + r], F32) for r in range(Q_PER_KV)], axis=0)


def _qk(q, k):
    return lax.dot_general(q, k, (((1,), (1,)), ((), ())), preferred_element_type=F32)


def _attn_body(sink_ref, q_ref, kl_ref, km_ref, kr_ref, vl_ref, vm_ref, vr_ref, kc_ref, vc_ref,
               o_ref, *, seq_len):
    n = pl.program_id(1)
    qpos = n * QBLK + lax.broadcasted_iota(jnp.int32, (QBLK, 3 * QBLK), 0)
    kpos = (n - 1) * QBLK + lax.broadcasted_iota(jnp.int32, (QBLK, 3 * QBLK), 1)
    valid = (jnp.abs(qpos - kpos) <= WINDOW) & (kpos >= 0) & (kpos < seq_len)
    bias = jnp.where(valid, 0.0, NEG_INF).astype(F32)
    bias = jnp.concatenate([bias] * Q_PER_KV, axis=0)
    q = q_ref[...] * jnp.asarray(HEAD_DIM ** -0.5, BF16)
    k_loc = jnp.concatenate([kl_ref[...], km_ref[...], kr_ref[...]], axis=0)
    v_loc = jnp.concatenate([vl_ref[...], vm_ref[...], vr_ref[...]], axis=0)
    k_ctx = kc_ref[...]
    v_ctx = vc_ref[...]
    outs = []
    for g in range(N_KV):
        lo, hi = g * HEAD_DIM, (g + 1) * HEAD_DIM
        qs = jnp.concatenate(
            [q[:, (g * Q_PER_KV + r) * HEAD_DIM:(g * Q_PER_KV + r + 1) * HEAD_DIM]
             for r in range(Q_PER_KV)], axis=0)
        s_loc = _qk(qs, k_loc[:, lo:hi]) + bias
        s_ctx = _qk(qs, k_ctx[:, lo:hi])
        o = _softmax_pv([s_loc, s_ctx], [v_loc[:, lo:hi], v_ctx[:, lo:hi]],
                        _sink_col(sink_ref, g, QBLK))
        outs += [o[r * QBLK:(r + 1) * QBLK] for r in range(Q_PER_KV)]
    o_ref[...] = jnp.concatenate(outs, axis=1).astype(BF16)


def _attention(z3, zc3, sink):
    bsz, seq_len, _ = z3.shape
    ctx_len = zc3.shape[1]
    nb = seq_len // QBLK
    kcol, vcol = ZK // MXU_N, ZV // MXU_N

    def blk(col, off):
        return pl.BlockSpec((None, QBLK, MXU_N),
                            lambda b, n: (b, jnp.clip(n + off, 0, nb - 1), col))

    return pl.pallas_call(
        functools.partial(_attn_body, seq_len=seq_len),
        grid=(bsz, nb),
        in_specs=[pl.BlockSpec(memory_space=pltpu.SMEM),
                  pl.BlockSpec((None, QBLK, D), lambda b, n: (b, n, 0)),
                  blk(kcol, -1), blk(kcol, 0), blk(kcol, 1),
                  blk(vcol, -1), blk(vcol, 0), blk(vcol, 1),
                  pl.BlockSpec((None, ctx_len, MXU_N), lambda b, n: (b, 0, kcol)),
                  pl.BlockSpec((None, ctx_len, MXU_N), lambda b, n: (b, 0, vcol))],
        out_specs=pl.BlockSpec((None, QBLK, D), lambda b, n: (b, n, 0)),
        out_shape=jax.ShapeDtypeStruct((bsz, seq_len, D), BF16),
        compiler_params=_params(("parallel", "parallel")),
        name="window_attn",
    )(sink, z3, z3, z3, z3, z3, z3, z3, zc3, zc3)


def _ctx_attn_body(sink_ref, q_ref, kc_ref, vc_ref, o_ref):
    rows = q_ref.shape[0]
    q = q_ref[...] * jnp.asarray(HEAD_DIM ** -0.5, BF16)
    k_ctx = kc_ref[...]
    v_ctx = vc_ref[...]
    outs = []
    for g in range(N_KV):
        lo, hi = g * HEAD_DIM, (g + 1) * HEAD_DIM
        qs = jnp.concatenate(
            [q[:, (g * Q_PER_KV + r) * HEAD_DIM:(g * Q_PER_KV + r + 1) * HEAD_DIM]
             for r in range(Q_PER_KV)], axis=0)
        o = _softmax_pv([_qk(qs, k_ctx[:, lo:hi])], [v_ctx[:, lo:hi]], _sink_col(sink_ref, g, rows))
        outs += [o[r * rows:(r + 1) * rows] for r in range(Q_PER_KV)]
    o_ref[...] = jnp.concatenate(outs, axis=1).astype(BF16)


def _ctx_attention(zc3, sink):
    bsz, ctx_len, _ = zc3.shape
    kcol, vcol = ZK // MXU_N, ZV // MXU_N
    return pl.pallas_call(
        _ctx_attn_body,
        grid=(bsz,),
        in_specs=[pl.BlockSpec(memory_space=pltpu.SMEM),
                  pl.BlockSpec((None, ctx_len, D), lambda b: (b, 0, 0)),
                  pl.BlockSpec((None, ctx_len, MXU_N), lambda b: (b, 0, kcol)),
                  pl.BlockSpec((None, ctx_len, MXU_N), lambda b: (b, 0, vcol))],
        out_specs=pl.BlockSpec((None, ctx_len, D), lambda b: (b, 0, 0)),
        out_shape=jax.ShapeDtypeStruct((bsz, ctx_len, D), BF16),
        compiler_params=_params(("parallel",)),
        name="ctx_attn",
    )(sink, zc3, zc3, zc3)


RG_CB = 512
RG_HALO = 16


def _rglru_body(x_ref, xp_ref, xn_ref, cw_ref, cb_ref, wg_ref, ba_ref, bi_ref, lam_ref, h0_ref,
                h_ref, hend_ref, a_scr, b_scr, carry_scr, *, reverse, n_t):
    t = pl.program_id(2)
    t_idx = (n_t - 1 - t) if reverse else t
    rows = x_ref.shape[0]
    x = x_ref[...].astype(F32)
    row = lax.broadcasted_iota(jnp.int32, (rows, RG_CB), 0)

    prev = xp_ref[...].astype(F32) * jnp.where(t_idx > 0, 1.0, 0.0)
    nxt = xn_ref[...].astype(F32) * jnp.where(t_idx < n_t - 1, 1.0, 0.0)
    p2, p1, n0 = prev[RG_HALO - 2:RG_HALO - 1], prev[RG_HALO - 1:RG_HALO], nxt[0:1]
    x_m1 = jnp.where(row == 0, p1, pltpu.roll(x, 1, 0))
    x_m2 = jnp.where(row == 0, p2, jnp.where(row == 1, p1, pltpu.roll(x, 2, 0)))
    x_p1 = jnp.where(row == rows - 1, n0, pltpu.roll(x, rows - 1, 0))
    cw = cw_ref[...]
    xc = cb_ref[...] + x_m2 * cw[0:1] + x_m1 * cw[1:2] + x * cw[2:3] + x_p1 * cw[3:4]

    lam = lam_ref[...]
    softplus_neg_lam = jnp.maximum(-lam, 0.0) + jnp.log1p(jnp.exp(-jnp.abs(lam)))
    row8 = row[:, :LANES] % 8
    for j in range(RG_CB // LANES):
        sl = slice(j * LANES, (j + 1) * LANES)
        xj = xc[:, sl]
        gates = jnp.dot(xj.astype(BF16), wg_ref[j], preferred_element_type=F32)
        r = _sigmoid(gates[:, :LANES] + ba_ref[:, sl])
        ig = _sigmoid(gates[:, LANES:] + bi_ref[:, sl])
        log_a = -RG_C * r * softplus_neg_lam[:, sl]
        a = jnp.exp(log_a)
        th = jnp.tanh(log_a)
        b = jnp.sqrt(-2.0 * th / (1.0 - th)) * (ig * xj)
        for d in (1, 2, 4):
            if reverse:
                a_s, b_s = pltpu.roll(a, rows - d, 0), pltpu.roll(b, rows - d, 0)
                ok = row8 < 8 - d
            else:
                a_s, b_s = pltpu.roll(a, d, 0), pltpu.roll(b, d, 0)
                ok = row8 >= d
            b = jnp.where(ok, a * b_s + b, b)
            a = jnp.where(ok, a * a_s, a)
        a_scr[:, sl] = a
        b_scr[:, sl] = b

    @pl.when(t == 0)
    def _():
        carry_scr[...] = jnp.broadcast_to(h0_ref[...], (8, RG_CB))

    n_grp = rows // 8

    def group(i, carry):
        g = (n_grp - 1 - i) if reverse else i
        off = pl.multiple_of(g * 8, 8)
        h = a_scr[pl.ds(off, 8), :] * carry + b_scr[pl.ds(off, 8), :]
        b_scr[pl.ds(off, 8), :] = h
        last = h[0:1] if reverse else h[7:8]
        return jnp.broadcast_to(last, (8, RG_CB))

    carry = lax.fori_loop(0, n_grp, group, carry_scr[...])
    carry_scr[...] = carry
    h_ref[...] = b_scr[...].astype(h_ref.dtype)
    hend_ref[...] = carry[0:1]


def _rglru(z3, conv_w, conv_b, wg, b_a, b_i, lam, h0, *, direction, tile):
    bsz, seq_len, _ = z3.shape
    n_t = seq_len // tile
    reverse = direction == 1
    n_cb = D // RG_CB
    xcol = ZXR // RG_CB
    hpt = tile // RG_HALO
    n_halo = seq_len // RG_HALO

    def tix(t):
        return (n_t - 1 - t) if reverse else t

    vec = lambda: pl.BlockSpec((None, 1, RG_CB), lambda b, c, t: (direction, 0, c))
    return pl.pallas_call(
        functools.partial(_rglru_body, reverse=reverse, n_t=n_t),
        grid=(bsz, n_cb, n_t),
        in_specs=[
            pl.BlockSpec((None, tile, RG_CB), lambda b, c, t: (b, tix(t), xcol + c)),
            pl.BlockSpec((None, RG_HALO, RG_CB),
                         lambda b, c, t: (b, jnp.maximum(tix(t) * hpt - 1, 0), xcol + c)),
            pl.BlockSpec((None, RG_HALO, RG_CB),
                         lambda b, c, t: (b, jnp.minimum((tix(t) + 1) * hpt, n_halo - 1), xcol + c)),
            pl.BlockSpec((4, RG_CB), lambda b, c, t: (0, c)),
            pl.BlockSpec((1, RG_CB), lambda b, c, t: (0, c)),
            pl.BlockSpec((None, RG_CB // LANES, LANES, 2 * LANES), lambda b, c, t: (direction, c, 0, 0)),
            vec(), vec(), vec(),
            pl.BlockSpec((None, 1, RG_CB), lambda b, c, t: (b, 0, c)),
        ],
        out_specs=[pl.BlockSpec((None, tile, RG_CB), lambda b, c, t: (b, tix(t), c)),
                   pl.BlockSpec((None, 1, RG_CB), lambda b, c, t: (b, 0, c))],
        out_shape=[jax.ShapeDtypeStruct((bsz, seq_len, D), BF16),
                   jax.ShapeDtypeStruct((bsz, 1, D), F32)],
        scratch_shapes=[pltpu.VMEM((tile, RG_CB), F32), pltpu.VMEM((tile, RG_CB), F32),
                        pltpu.VMEM((8, RG_CB), F32)],
        compiler_params=_params(("parallel", "parallel", "arbitrary")),
        name="rglru_bwd" if reverse else "rglru_fwd",
    )(z3, z3, z3, conv_w, conv_b.reshape(1, D), wg, b_a.reshape(2, 1, D), b_i.reshape(2, 1, D),
      lam.reshape(2, 1, D), h0)


def _gate_weights(w_a, w_i):
    def pair(w):
        w = w.reshape(2, 8, 2, 64, 64)
        z = jnp.zeros_like(w[:, :, 0])
        top = jnp.concatenate([w[:, :, 0], z], axis=-1)
        bot = jnp.concatenate([z, w[:, :, 1]], axis=-1)
        return jnp.concatenate([top, bot], axis=-2)
    return jnp.concatenate([pair(w_a), pair(w_i)], axis=-1).astype(BF16)


def _seqdft_body(c_ref, s_ref, xc_ref, xs_ref, o_ref, acc_ref):
    k = pl.program_id(2)

    @pl.when(k == 0)
    def _():
        acc_ref[...] = jnp.zeros_like(acc_ref)

    acc_ref[...] += (jnp.dot(c_ref[...], xc_ref[...], preferred_element_type=F32)
                     + jnp.dot(s_ref[...], xs_ref[...], preferred_element_type=F32))

    @pl.when(k == pl.num_programs(2) - 1)
    def _():
        o_ref[...] = acc_ref[...].astype(o_ref.dtype)


def _seq_dft(z3, cmat, nsmat, tile):
    bsz, seq_len, _ = z3.shape
    nt = seq_len // tile
    return pl.pallas_call(
        _seqdft_body,
        grid=(bsz, nt, nt),
        in_specs=[pl.BlockSpec((tile, tile), lambda b, i, k: (i, k)),
                  pl.BlockSpec((tile, tile), lambda b, i, k: (i, k)),
                  pl.BlockSpec((None, tile, D), lambda b, i, k: (b, k, ZXC // D)),
                  pl.BlockSpec((None, tile, D), lambda b, i, k: (b, k, ZXS // D))],
        out_specs=pl.BlockSpec((None, tile, D), lambda b, i, k: (b, i, 0)),
        out_shape=jax.ShapeDtypeStruct((bsz, seq_len, D), BF16),
        scratch_shapes=[pltpu.VMEM((tile, D), F32)],
        compiler_params=_params(("parallel", "parallel", "arbitrary")),
        name="seq_dft",
    )(cmat, nsmat, z3, z3)


def _dft_mats(n, scale):
    idx = jnp.arange(n, dtype=jnp.int32)
    ang = ((idx[:, None] * idx[None, :]) % n).astype(F32) * (2.0 * math.pi / n)
    return (jnp.cos(ang) * scale), (jnp.sin(ang) * scale)


def _gelu_tanh(x):
    return 0.5 * x * (1.0 + jnp.tanh(math.sqrt(2.0 / math.pi) * (x + 0.044715 * (x * x * x))))


def _merge_body(x_ref, mod_ref, attn_ref, gr_ref, hf_ref, hb_ref, four_ref,
                woa_ref, wor_ref, wof_ref, wm_ref, bm_ref, wout_ref, lng_ref, lnb_ref, wr_ref, br_ref,
                x1_ref, u2_ref, eid_ref, gate_ref):
    mod = mod_ref[...]
    x = x_ref[...]
    u = (x * (1.0 + mod[1:2]) + mod[0:1]).astype(BF16)
    rg_in = (_gelu_tanh(gr_ref[...].astype(F32))
             * (hf_ref[...].astype(F32) + hb_ref[...].astype(F32))).astype(BF16)
    branches = ((attn_ref[...], woa_ref), (rg_in, wor_ref), (four_ref[...], wof_ref))
    merged = None
    for j, (inp, w_ref) in enumerate(branches):
        y = jnp.dot(inp, w_ref[...], preferred_element_type=F32)
        g = _sigmoid(jnp.dot(u, wm_ref[:, j * D:(j + 1) * D], preferred_element_type=F32)
                     + bm_ref[:, j * D:(j + 1) * D])
        merged = g * y if merged is None else merged + g * y
    y = jnp.dot(merged.astype(BF16), wout_ref[...], preferred_element_type=F32)
    x1 = _layer_norm(DEEPNORM_ALPHA * x + mod[2:3] * y, lng_ref[...], lnb_ref[...])
    x1_ref[...] = x1
    u2 = x1 * (1.0 + mod[4:5]) + mod[3:4]
    u2_ref[...] = u2

    logits = jnp.dot(u2.astype(BF16), wr_ref[...], preferred_element_type=F32) + br_ref[...]
    lane = lax.broadcasted_iota(jnp.int32, logits.shape, 1)
    vals, idxs = [], []
    for _ in range(TOP_K):
        m = jnp.max(logits, axis=-1, keepdims=True)
        idx = jnp.min(jnp.where(logits == m, lane, LANES), axis=-1, keepdims=True)
        vals.append(m)
        idxs.append(idx)
        logits = jnp.where(lane == idx, -3.0e38, logits)
    exps = [jnp.exp(v - vals[0]) for v in vals]
    den = exps[0] + exps[1] + exps[2] + exps[3]
    eid = jnp.zeros(lane.shape, jnp.int32)
    gate = jnp.zeros(lane.shape, F32)
    for k in range(TOP_K):
        eid = jnp.where(lane == k, idxs[k], eid)
        gate = jnp.where(lane == k, exps[k] / den, gate)
    eid_ref[...] = eid
    gate_ref[...] = gate


def _merge(x2, mods, attn, z, hf, hb, four, p, seq_len, tm):
    n = x2.shape[0]
    tpb = seq_len // tm
    bm = mods.shape[0]
    mod_map = (lambda i: (i // tpb, 0, 0)) if bm > 1 else (lambda i: (0, 0, 0))
    act = lambda: pl.BlockSpec((tm, D), lambda i: (i, 0))
    return pl.pallas_call(
        _merge_body,
        grid=(n // tm,),
        in_specs=[act(), pl.BlockSpec((None, 6, D), mod_map), act(),
                  pl.BlockSpec((tm, D), lambda i: (i, ZGR // D)), act(), act(), act(),
                  _resident((D, D)), _resident((D, D)), _resident((D, D)),
                  _resident((D, 3 * D)), _resident((1, 3 * D)), _resident((D, D)),
                  _resident((1, D)), _resident((1, D)), _resident((D, LANES)), _resident((1, LANES))],
        out_specs=[act(), act(), pl.BlockSpec((tm, LANES), lambda i: (i, 0)),
                   pl.BlockSpec((tm, LANES), lambda i: (i, 0))],
        out_shape=[jax.ShapeDtypeStruct((n, D), F32), jax.ShapeDtypeStruct((n, D), F32),
                   jax.ShapeDtypeStruct((n, LANES), jnp.int32), jax.ShapeDtypeStruct((n, LANES), F32)],
        compiler_params=_params(("parallel",)),
        name="merge_ln_router",
    )(x2, mods, attn, z, hf, hb, four, p["w_o_attn"], p["w_o_rg"], p["w_o_four"], p["w_merge"],
      p["b_merge"], p["w_out"], p["ln1_g"], p["ln1_b"], p["w_router"], p["b_router"])


def _moe_body(be_ref, nused_ref, idx_hbm, u_hbm, wgu_ref, bgu_ref, wdn_ref, bdn_ref, y_hbm,
              idx_smem, xbuf, ybuf, sem_i, sem_g, sem_s, *, tb, n_tok):
    i = pl.program_id(0)
    n_used = nused_ref[0]
    slot = i % 2
    nslot = 1 - slot

    def idx_copy(block, s):
        return pltpu.make_async_copy(idx_hbm.at[block], idx_smem.at[s], sem_i.at[s])

    def issue_gather(s):
        def one(r, c):
            tok = idx_smem[s, r]
            pltpu.make_async_copy(u_hbm.at[pl.ds(tok, 1)], xbuf.at[s, pl.ds(r, 1)], sem_g.at[s]).start()
            return c
        lax.fori_loop(0, tb, one, 0, unroll=8)

    def issue_scatter(s):
        def one(r, c):
            dst = idx_smem[s, tb + r]
            pltpu.make_async_copy(ybuf.at[s, pl.ds(r, 1)], y_hbm.at[pl.ds(dst, 1)], sem_s.at[s]).start()
            return c
        lax.fori_loop(0, tb, one, 0, unroll=8)

    def wait_gather(s):
        pltpu.make_async_copy(u_hbm.at[pl.ds(0, tb)], xbuf.at[s], sem_g.at[s]).wait()

    def wait_scatter(s):
        pltpu.make_async_copy(ybuf.at[s], y_hbm.at[pl.ds(0, tb)], sem_s.at[s]).wait()

    @pl.when(i == 0)
    def _():
        c = idx_copy(0, 0)
        c.start()
        ybuf[0] = jnp.zeros((tb, D), F32)
        spare = [pltpu.make_async_copy(ybuf.at[0], y_hbm.at[pl.ds(TOP_K * n_tok + e * tb, tb)], sem_s.at[0])
                 for e in range(N_EXPERTS)]
        for cp in spare:
            cp.start()
        c.wait()
        issue_gather(0)
        for cp in spare:
            cp.wait()

    @pl.when(i < n_used)
    def _():
        @pl.when(i + 1 < n_used)
        def _():
            idx_copy(i + 1, nslot).start()

        wait_gather(slot)
        x = xbuf[slot].astype(BF16)
        h = jnp.dot(x, wgu_ref[...], preferred_element_type=F32) + bgu_ref[...]
        half = h.shape[1] // 2
        x_glu = jnp.minimum(h[:, :half], SWIGLU_LIMIT)
        x_lin = jnp.clip(h[:, half:], -SWIGLU_LIMIT, SWIGLU_LIMIT)
        act = (x_glu * _sigmoid(SWIGLU_ALPHA * x_glu) * (x_lin + 1.0)).astype(BF16)

        @pl.when(i + 1 < n_used)
        def _():
            idx_copy(i + 1, nslot).wait()
            issue_gather(nslot)

        y = jnp.dot(act, wdn_ref[...], preferred_element_type=F32) + bdn_ref[...]

        @pl.when(i >= 2)
        def _():
            wait_scatter(slot)

        ybuf[slot] = y
        issue_scatter(slot)

        @pl.when(i == n_used - 1)
        def _():
            wait_scatter(slot)

            @pl.when(i >= 1)
            def _():
                wait_scatter(nslot)


def _moe(u2, block_e, n_used, idx, w_gu, b_gu, w_dn, b_dn, tb, n_rows_out):
    n_blocks = idx.shape[0]
    d_ff2 = w_gu.shape[2]
    grid_spec = pltpu.PrefetchScalarGridSpec(
        num_scalar_prefetch=2,
        grid=(n_blocks,),
        in_specs=[pl.BlockSpec(memory_space=pl.ANY),
                  pl.BlockSpec(memory_space=pl.ANY),
                  pl.BlockSpec((None, D, d_ff2), lambda i, be, nu: (be[i], 0, 0)),
                  pl.BlockSpec((None, 1, d_ff2), lambda i, be, nu: (be[i], 0, 0)),
                  pl.BlockSpec((None, d_ff2 // 2, D), lambda i, be, nu: (be[i], 0, 0)),
                  pl.BlockSpec((None, 1, D), lambda i, be, nu: (be[i], 0, 0))],
        out_specs=pl.BlockSpec(memory_space=pl.ANY),
        scratch_shapes=[pltpu.SMEM((2, 2 * tb), jnp.int32),
                        pltpu.VMEM((2, tb, D), F32), pltpu.VMEM((2, tb, D), F32),
                        pltpu.SemaphoreType.DMA((2,)), pltpu.SemaphoreType.DMA((2,)),
                        pltpu.SemaphoreType.DMA((2,))],
    )
    return pl.pallas_call(
        functools.partial(_moe_body, tb=tb, n_tok=u2.shape[0]),
        grid_spec=grid_spec,
        out_shape=jax.ShapeDtypeStruct((n_rows_out, D), F32),
        compiler_params=_params(("arbitrary",)),
        name="moe_experts",
    )(block_e, n_used, idx, u2, w_gu, b_gu, w_dn, b_dn)


def _route(eid, n_tok, tb):
    n_pad_rows = N_EXPERTS * tb
    n_rows_out = n_tok * TOP_K + n_pad_rows
    n_slots = n_tok * TOP_K + n_pad_rows
    n_blocks = n_slots // tb
    onehot = (eid[:, :, None] == jnp.arange(N_EXPERTS, dtype=jnp.int32)).astype(jnp.int32)
    member = onehot.sum(axis=1)
    csum = jnp.cumsum(member, axis=0)
    counts = csum[-1]
    rank = jnp.sum(onehot * (csum - member)[:, None, :], axis=-1)
    padded = (counts + tb - 1) // tb * tb
    pend = jnp.cumsum(padded)
    pstart = pend - padded
    dest = jnp.sum(onehot * pstart[None, None, :], axis=-1) + rank
    block_e = jnp.minimum(
        jnp.searchsorted(pend, jnp.arange(n_blocks, dtype=jnp.int32) * tb, side="right"),
        N_EXPERTS - 1).astype(jnp.int32)
    n_used = (pend[-1] // tb).astype(jnp.int32).reshape(1)
    tok = jnp.broadcast_to(jnp.arange(n_tok, dtype=jnp.int32)[:, None], (n_tok, TOP_K))
    kk = jnp.broadcast_to(jnp.arange(TOP_K, dtype=jnp.int32)[None, :], (n_tok, TOP_K))
    flat = dest.reshape(-1)
    slot_src = jnp.zeros((n_slots,), jnp.int32).at[flat].set(tok.reshape(-1), unique_indices=True)
    s = jnp.arange(n_slots, dtype=jnp.int32)
    spare = n_tok * TOP_K + block_e[s // tb] * tb + s % tb
    slot_dst = spare.at[flat].set((kk * n_tok + tok).reshape(-1), unique_indices=True)
    idx = jnp.concatenate([slot_src.reshape(n_blocks, tb), slot_dst.reshape(n_blocks, tb)], axis=1)
    return block_e, n_used, idx, n_rows_out


def _combine_body(x1_ref, mod_ref, gate_ref, y0_ref, y1_ref, y2_ref, y3_ref, lng_ref, lnb_ref, o_ref):
    mod = mod_ref[...]
    gate = gate_ref[...]
    f = None
    for k, y_ref in enumerate((y0_ref, y1_ref, y2_ref, y3_ref)):
        t = y_ref[...] * gate[:, k:k + 1]
        f = t if f is None else f + t
    o_ref[...] = _layer_norm(DEEPNORM_ALPHA * x1_ref[...] + mod[5:6] * f, lng_ref[...], lnb_ref[...])


def _combine(x1, mods, gate, planes, ln_g, ln_b, seq_len, tm):
    n = x1.shape[0]
    tpb = seq_len // tm
    bm = mods.shape[0]
    mod_map = (lambda i: (i // tpb, 0, 0)) if bm > 1 else (lambda i: (0, 0, 0))
    pln = lambda k: pl.BlockSpec((tm, D), lambda i: (k * (n // tm) + i, 0))
    return pl.pallas_call(
        _combine_body,
        grid=(n // tm,),
        in_specs=[pl.BlockSpec((tm, D), lambda i: (i, 0)), pl.BlockSpec((None, 6, D), mod_map),
                  pl.BlockSpec((tm, LANES), lambda i: (i, 0)), pln(0), pln(1), pln(2), pln(3),
                  _resident((1, D)), _resident((1, D))],
        out_specs=pl.BlockSpec((tm, D), lambda i: (i, 0)),
        out_shape=jax.ShapeDtypeStruct((n, D), F32),
        compiler_params=_params(("parallel",)),
        name="combine_ln",
    )(x1, mods, gate, planes, planes, planes, planes, ln_g, ln_b)


def _rope_tables(seq_len):
    pos = jnp.arange(seq_len, dtype=jnp.int32)
    row = (pos // GRID_W).astype(F32)
    col = (pos % GRID_W).astype(F32)
    n_freq = HEAD_DIM // 4
    freqs = ROPE_BASE ** (-jnp.arange(n_freq, dtype=F32) / n_freq)
    ar, ac = row[:, None] * freqs, col[:, None] * freqs
    zero = jnp.zeros_like(ar)
    c = jnp.concatenate([jnp.cos(ar), jnp.cos(ar), jnp.cos(ac), jnp.cos(ac)], axis=1)
    sa = jnp.concatenate([-jnp.sin(ar), zero, -jnp.sin(ac), zero], axis=1)
    sb = jnp.concatenate([zero, jnp.sin(ar), zero, jnp.sin(ac)], axis=1)
    rep = MXU_N // HEAD_DIM
    return tuple(jnp.tile(t, (1, rep)) for t in (c, sa, sb))


def _layer_params(l, w):
    w_router = jnp.zeros((D, LANES), BF16).at[:, :N_EXPERTS].set(w["w_router"][l].astype(BF16))
    b_router = jnp.full((1, LANES), NEG_INF, F32).at[0, :N_EXPERTS].set(w["b_router"][l])
    return dict(
        w_in=w["w_in"][l].astype(BF16),
        w_o_attn=w["w_o_attn"][l].astype(BF16), w_o_rg=w["w_o_rg"][l].astype(BF16),
        w_o_four=w["w_o_four"][l].astype(BF16), w_merge=w["w_merge"][l].astype(BF16),
        b_merge=w["b_merge"][l].reshape(1, 3 * D), w_out=w["w_out"][l].astype(BF16),
        ln1_g=w["ln1_g"][l].reshape(1, D), ln1_b=w["ln1_b"][l].reshape(1, D),
        ln2_g=w["ln2_g"][l].reshape(1, D), ln2_b=w["ln2_b"][l].reshape(1, D),
        w_router=w_router, b_router=b_router,
        w_gu=w["w_gate_up"][l].astype(BF16), b_gu=w["b_gate_up"][l].reshape(N_EXPERTS, 1, -1),
        w_dn=w["w_down"][l].astype(BF16), b_dn=w["b_down"][l].reshape(N_EXPERTS, 1, D),
        wg=_gate_weights(w["rg_w_a"][l], w["rg_w_i"][l]),
        conv_w=w["conv_w"][l], conv_b=w["conv_b"][l], b_a=w["rg_b_a"][l], b_i=w["rg_b_i"][l],
        lam=w["rg_lambda"][l], sink=w["attn_sink"][l],
    )


def _row_tile(seq_len, want):
    return min(want, seq_len)


def _ffn(x1, u2, eid128, gate128, mods, p, seq_len, tb, tm):
    n_tok = x1.shape[0]
    block_e, n_used, idx, n_rows_out = _route(eid128[:, :TOP_K], n_tok, tb)
    y = _moe(u2, block_e, n_used, idx, p["w_gu"], p["b_gu"], p["w_dn"], p["b_dn"], tb, n_rows_out)
    return _combine(x1, mods, gate128, y, p["ln2_g"], p["ln2_b"], seq_len, tm)


def kernel(x, c, ctx, c_ctx, w_mod, b_mod, w_in, attn_sink, w_o_attn, conv_w, conv_b, rg_w_a, rg_b_a, rg_w_i, rg_b_i, rg_lambda, w_o_rg, w_o_four, w_merge, b_merge, w_out, ln1_g, ln1_b, w_router, b_router, w_gate_up, b_gate_up, w_down, b_down, ln2_g, ln2_b):
    w = dict(w_mod=w_mod, b_mod=b_mod, w_in=w_in, attn_sink=attn_sink, w_o_attn=w_o_attn, conv_w=conv_w,
             conv_b=conv_b, rg_w_a=rg_w_a, rg_b_a=rg_b_a, rg_w_i=rg_w_i, rg_b_i=rg_b_i, rg_lambda=rg_lambda,
             w_o_rg=w_o_rg, w_o_four=w_o_four, w_merge=w_merge, b_merge=b_merge, w_out=w_out, ln1_g=ln1_g,
             ln1_b=ln1_b, w_router=w_router, b_router=b_router, w_gate_up=w_gate_up, b_gate_up=b_gate_up,
             w_down=w_down, b_down=b_down, ln2_g=ln2_g, ln2_b=ln2_b)
    bsz, seq_len, _ = x.shape
    ctx_len = ctx.shape[1]
    n_lat, n_ctx = bsz * seq_len, bsz * ctx_len
    tm_lat, tm_ctx = _row_tile(seq_len, 512), _row_tile(ctx_len, 256)
    dft_lat, dft_ctx = _row_tile(seq_len, 1024), _row_tile(ctx_len, 256)
    rg_lat, rg_ctx = _row_tile(seq_len, 256), _row_tile(ctx_len, 256)
    tb_lat, tb_ctx = 512, 128

    rope = _rope_tables(seq_len)
    cw, sw = _dft_mats(F_GW, F_GW ** -0.5)
    csw = jnp.concatenate([cw, sw], axis=1).astype(BF16)
    cl, sl = _dft_mats(seq_len, seq_len ** -0.5)
    cl, nsl = cl.astype(BF16), (-sl).astype(BF16)
    cc, sc = _dft_mats(ctx_len, ctx_len ** -0.5)
    cc, nsc = cc.astype(BF16), (-sc).astype(BF16)

    cond = jnp.zeros((16, D), F32).at[:bsz].set(c).at[bsz].set(c_ctx)
    x2 = x.reshape(n_lat, D)
    ctx2 = ctx.reshape(n_ctx, D)
    zero_h = jnp.zeros((bsz, 1, D), F32)

    for l in range(DEPTH):
        last = l == DEPTH - 1
        p = _layer_params(l, w)
        m = _adaln(cond, w_mod[l], b_mod[l]).reshape(16, 6, D)
        mods_lat, mods_ctx = m[:bsz], m[bsz:bsz + 1]

        zc = _proj(ctx2, mods_ctx, p["w_in"], csw, None, ctx_len, tm_ctx)
        z = _proj(x2, mods_lat, p["w_in"], csw, rope, seq_len, tm_lat)
        zc3 = zc.reshape(bsz, ctx_len, Z_COLS)
        z3 = z.reshape(bsz, seq_len, Z_COLS)

        rg = lambda zz, h0, d, tile: _rglru(zz, p["conv_w"], p["conv_b"], p["wg"], p["b_a"], p["b_i"],
                                            p["lam"], h0, direction=d, tile=tile)
        hcf, endf = rg(zc3, zero_h, 0, rg_ctx)
        hcb, endb = rg(zc3, zero_h, 1, rg_ctx)
        hf, _ = rg(z3, endf, 0, rg_lat)
        hb, _ = rg(z3, endb, 1, rg_lat)

        attn = _attention(z3, zc3, p["sink"])
        four = _seq_dft(z3, cl, nsl, dft_lat)
        x1, u2, eid, gate = _merge(x2, mods_lat, attn.reshape(n_lat, D), z, hf.reshape(n_lat, D),
                                   hb.reshape(n_lat, D), four.reshape(n_lat, D), p, seq_len, tm_lat)
        x2 = _ffn(x1, u2, eid, gate, mods_lat, p, seq_len, tb_lat, tm_lat)

        if not last:
            attn_c = _ctx_attention(zc3, p["sink"])
            four_c = _seq_dft(zc3, cc, nsc, dft_ctx)
            c1, uc2, eid_c, gate_c = _merge(ctx2, mods_ctx, attn_c.reshape(n_ctx, D), zc,
                                            hcf.reshape(n_ctx, D), hcb.reshape(n_ctx, D),
                                            four_c.reshape(n_ctx, D), p, ctx_len, tm_ctx)
            ctx2 = _ffn(c1, uc2, eid_c, gate_c, mods_ctx, p, ctx_len, tb_ctx, tm_ctx)

    return x2.reshape(bsz, seq_len, D)
```

```python
import functools
import math

import jax
import jax.numpy as jnp
from jax import lax
from jax.experimental import pallas as pl
from jax.experimental.pallas import tpu as pltpu

F32 = jnp.float32
BF16 = jnp.bfloat16

D = 1024
HEAD_DIM = 64
N_HEADS = 16
N_KV = 4
Q_PER_KV = 4
WINDOW = 128
QBLK = 128
assert WINDOW == QBLK
GRID_W = 64
ROPE_BASE = 10000.0
RG_C = 8.0
CONV_LEFT = 2
F_GROUPS = 4
F_GW = 256
N_EXPERTS = 32
TOP_K = 4
SWIGLU_LIMIT = 7.0
SWIGLU_ALPHA = 1.702
LN_EPS = 1e-5
DEPTH = 2
DEEPNORM_ALPHA = (2 * DEPTH) ** 0.25
NEG_INF = -1e30

Q_OFF, K_OFF, V_OFF, XR_OFF, GR_OFF, XF_OFF, IN_COLS = 0, 1024, 1280, 1536, 2560, 3584, 4608
ZQ, ZXR, ZGR, ZXC, ZXS, ZK, ZV, Z_COLS = 0, 1024, 2048, 3072, 4096, 5120, 5376, 5632

VMEM_LIMIT_V7X = 56 * 1024 * 1024
LANES = 128
MXU_N = 256


def _params(sem, vmem=VMEM_LIMIT_V7X):
    return pltpu.CompilerParams(dimension_semantics=sem, vmem_limit_bytes=vmem)


def _resident(shape):
    nd = len(shape)
    return pl.BlockSpec(shape, lambda *_: (0,) * nd, pipeline_mode=pl.Buffered(1))


def _sigmoid(x):
    return 1.0 / (1.0 + jnp.exp(-x))


def _layer_norm(r, g, b):
    mu = jnp.mean(r, axis=-1, keepdims=True)
    rc = r - mu
    var = jnp.mean(rc * rc, axis=-1, keepdims=True)
    return rc * lax.rsqrt(var + LN_EPS) * g + b


def _adaln_body(c_ref, w_ref, b_ref, o_ref):
    c = c_ref[...]
    s = (c * _sigmoid(c)).astype(BF16)
    o_ref[...] = jnp.dot(s, w_ref[...].astype(BF16), preferred_element_type=F32) + b_ref[...]


def _adaln(cond, w_mod, b_mod):
    rows, n = cond.shape[0], w_mod.shape[1]
    tn = 1024
    return pl.pallas_call(
        _adaln_body,
        grid=(n // tn,),
        in_specs=[pl.BlockSpec((rows, D), lambda j: (0, 0)),
                  pl.BlockSpec((D, tn), lambda j: (0, j)),
                  pl.BlockSpec((1, tn), lambda j: (0, j))],
        out_specs=pl.BlockSpec((rows, tn), lambda j: (0, j)),
        out_shape=jax.ShapeDtypeStruct((rows, n), F32),
        compiler_params=_params(("arbitrary",)),
        name="adaln",
    )(cond, w_mod, b_mod.reshape(1, n))


def _proj_dst(j):
    if j < 4:
        return ZQ + j * MXU_N
    if j == 4:
        return ZK
    if j == 5:
        return ZV
    if j < 10:
        return ZXR + (j - 6) * MXU_N
    if j < 14:
        return ZGR + (j - 10) * MXU_N
    return None


def _proj_body(x_ref, mod_ref, w_ref, csw_ref, *rest, rope):
    if rope:
        c_ref, sa_ref, sb_ref, o_ref = rest
    else:
        (o_ref,) = rest
    mod = mod_ref[...]
    u = (x_ref[...] * (1.0 + mod[1:2]) + mod[0:1]).astype(BF16)
    for j in range(IN_COLS // MXU_N):
        acc = jnp.dot(u, w_ref[:, j * MXU_N:(j + 1) * MXU_N], preferred_element_type=F32)
        if rope and j < 5:
            acc = (acc * c_ref[...] + pltpu.roll(acc, MXU_N - 16, 1) * sa_ref[...]
                   + pltpu.roll(acc, 16, 1) * sb_ref[...])
        dst = _proj_dst(j)
        if dst is not None:
            o_ref[:, dst:dst + MXU_N] = acc.astype(BF16)
        else:
            g = j - 14
            t = jnp.dot(acc.astype(BF16), csw_ref[...], preferred_element_type=F32)
            o_ref[:, ZXC + g * F_GW:ZXC + (g + 1) * F_GW] = t[:, :F_GW].astype(BF16)
            o_ref[:, ZXS + g * F_GW:ZXS + (g + 1) * F_GW] = t[:, F_GW:].astype(BF16)


def _proj(x2, mods, w_in, csw, rope_tabs, seq_len, tm):
    n = x2.shape[0]
    tpb = seq_len // tm
    bm = mods.shape[0]
    mod_map = (lambda i: (i // tpb, 0, 0)) if bm > 1 else (lambda i: (0, 0, 0))
    in_specs = [pl.BlockSpec((tm, D), lambda i: (i, 0)),
                pl.BlockSpec((None, 6, D), mod_map),
                _resident((D, IN_COLS)),
                _resident((F_GW, 2 * F_GW))]
    args = [x2, mods, w_in, csw]
    if rope_tabs is not None:
        in_specs += [pl.BlockSpec((tm, MXU_N), lambda i: (i % tpb, 0))] * 3
        args += list(rope_tabs)
    return pl.pallas_call(
        functools.partial(_proj_body, rope=rope_tabs is not None),
        grid=(n // tm,),
        in_specs=in_specs,
        out_specs=pl.BlockSpec((tm, Z_COLS), lambda i: (i, 0)),
        out_shape=jax.ShapeDtypeStruct((n, Z_COLS), BF16),
        compiler_params=_params(("parallel",)),
        name="proj_rope" if rope_tabs is not None else "proj_ctx",
    )(*args)


def _group_attention(sink_ref, q_ref, o_ref, g, parts):
    rows = q_ref.shape[0]
    key_low = g % 2 == 0
    lane = lax.broadcasted_iota(jnp.int32, (1, LANES), 1)
    keep = (lane < HEAD_DIM) if key_low else (lane >= HEAD_DIM)
    kv_lanes = slice((g // 2) * LANES, (g // 2 + 1) * LANES)

    blocks = []
    for j in range(2):
        q32 = q_ref[:, (2 * g + j) * LANES:(2 * g + j + 1) * LANES].astype(F32) * (HEAD_DIM ** -0.5)
        same, rot = q32.astype(BF16), pltpu.roll(q32, HEAD_DIM, 1).astype(BF16)
        blocks += [same, rot] if key_low else [rot, same]
    qs = jnp.concatenate(blocks, axis=0)

    row = lax.broadcasted_iota(jnp.int32, (Q_PER_KV * rows, 1), 0)
    sink = jnp.full((Q_PER_KV * rows, 1), sink_ref[g * Q_PER_KV + Q_PER_KV - 1], F32)
    for r in range(Q_PER_KV - 2, -1, -1):
        sink = jnp.where(row < (r + 1) * rows, sink_ref[g * Q_PER_KV + r], sink)

    scores = []
    m = sink
    for k_ref, _, bias in parts:
        k2 = jnp.where(keep, k_ref[:, kv_lanes], jnp.zeros((), BF16))
        s = lax.dot_general(qs, k2, (((1,), (1,)), ((), ())), preferred_element_type=F32)
        if bias is not None:
            s = s + bias
        scores.append(s)
        m = jnp.maximum(m, jnp.max(s, axis=-1, keepdims=True))
    den = jnp.exp(sink - m)
    out = None
    for s, (_, v_ref, _) in zip(scores, parts):
        p = jnp.exp(s - m)
        den = den + jnp.sum(p, axis=-1, keepdims=True)
        v2 = jnp.where(keep, v_ref[:, kv_lanes], jnp.zeros((), BF16))
        o = jnp.dot(p.astype(BF16), v2, preferred_element_type=F32)
        out = o if out is None else out + o
    out = out / den
    for j in range(2):
        first, second = out[2 * j * rows:(2 * j + 1) * rows], out[(2 * j + 1) * rows:(2 * j + 2) * rows]
        if key_low:
            both = first + pltpu.roll(second, HEAD_DIM, 1)
        else:
            both = pltpu.roll(first, HEAD_DIM, 1) + second
        o_ref[:, (2 * g + j) * LANES:(2 * g + j + 1) * LANES] = both.astype(o_ref.dtype)


def _attn_body(sink_ref, q_ref, kl_ref, km_ref, kr_ref, vl_ref, vm_ref, vr_ref, kc_ref, vc_ref,
               o_ref, *, n_blk):
    n = pl.program_id(1)
    i = lax.broadcasted_iota(jnp.int32, (Q_PER_KV * QBLK, QBLK), 0) % QBLK
    j = lax.broadcasted_iota(jnp.int32, (Q_PER_KV * QBLK, QBLK), 1)
    bias_prev = jnp.where((j >= i) & (n > 0), 0.0, NEG_INF).astype(F32)
    bias_next = jnp.where((j <= i) & (n < n_blk - 1), 0.0, NEG_INF).astype(F32)
    parts = ((kl_ref, vl_ref, bias_prev), (km_ref, vm_ref, None), (kr_ref, vr_ref, bias_next),
             (kc_ref, vc_ref, None))
    for g in range(N_KV):
        _group_attention(sink_ref, q_ref, o_ref, g, parts)


def _attention(z3, zc3, sink):
    bsz, seq_len, _ = z3.shape
    ctx_len = zc3.shape[1]
    nb = seq_len // QBLK
    kcol, vcol = ZK // MXU_N, ZV // MXU_N

    def blk(col, off):
        return pl.BlockSpec((None, QBLK, MXU_N),
                            lambda b, n: (b, jnp.clip(n + off, 0, nb - 1), col))

    return pl.pallas_call(
        functools.partial(_attn_body, n_blk=nb),
        grid=(bsz, nb),
        in_specs=[pl.BlockSpec(memory_space=pltpu.SMEM),
                  pl.BlockSpec((None, QBLK, D), lambda b, n: (b, n, 0)),
                  blk(kcol, -1), blk(kcol, 0), blk(kcol, 1),
                  blk(vcol, -1), blk(vcol, 0), blk(vcol, 1),
                  pl.BlockSpec((None, ctx_len, MXU_N), lambda b, n: (b, 0, kcol)),
                  pl.BlockSpec((None, ctx_len, MXU_N), lambda b, n: (b, 0, vcol))],
        out_specs=pl.BlockSpec((None, QBLK, D), lambda b, n: (b, n, 0)),
        out_shape=jax.ShapeDtypeStruct((bsz, seq_len, D), BF16),
        compiler_params=_params(("parallel", "parallel")),
        name="window_attn",
    )(sink, z3, z3, z3, z3, z3, z3, z3, zc3, zc3)


def _ctx_attn_body(sink_ref, q_ref, kc_ref, vc_ref, o_ref):
    for g in range(N_KV):
        _group_attention(sink_ref, q_ref, o_ref, g, ((kc_ref, vc_ref, None),))


def _ctx_attention(zc3, sink):
    bsz, ctx_len, _ = zc3.shape
    kcol, vcol = ZK // MXU_N, ZV // MXU_N
    return pl.pallas_call(
        _ctx_attn_body,
        grid=(bsz,),
        in_specs=[pl.BlockSpec(memory_space=pltpu.SMEM),
                  pl.BlockSpec((None, ctx_len, D), lambda b: (b, 0, 0)),
                  pl.BlockSpec((None, ctx_len, MXU_N), lambda b: (b, 0, kcol)),
                  pl.BlockSpec((None, ctx_len, MXU_N), lambda b: (b, 0, vcol))],
        out_specs=pl.BlockSpec((None, ctx_len, D), lambda b: (b, 0, 0)),
        out_shape=jax.ShapeDtypeStruct((bsz, ctx_len, D), BF16),
        compiler_params=_params(("parallel",)),
        name="ctx_attn",
    )(sink, zc3, zc3, zc3)


RG_CB = 512
RG_HALO = 16


def _rglru_body(x_ref, xp_ref, xn_ref, cw_ref, cb_ref, wg_ref, ba_ref, bi_ref, lam_ref, h0_ref,
                h_ref, hend_ref, a_scr, b_scr, carry_scr, *, reverse, n_t):
    t = pl.program_id(2)
    t_idx = (n_t - 1 - t) if reverse else t
    rows = x_ref.shape[0]
    x = x_ref[...].astype(F32)
    row = lax.broadcasted_iota(jnp.int32, (rows, RG_CB), 0)

    prev = xp_ref[...].astype(F32) * jnp.where(t_idx > 0, 1.0, 0.0)
    nxt = xn_ref[...].astype(F32) * jnp.where(t_idx < n_t - 1, 1.0, 0.0)
    p2, p1, n0 = prev[RG_HALO - 2:RG_HALO - 1], prev[RG_HALO - 1:RG_HALO], nxt[0:1]
    x_m1 = jnp.where(row == 0, p1, pltpu.roll(x, 1, 0))
    x_m2 = jnp.where(row == 0, p2, jnp.where(row == 1, p1, pltpu.roll(x, 2, 0)))
    x_p1 = jnp.where(row == rows - 1, n0, pltpu.roll(x, rows - 1, 0))
    cw = cw_ref[...]
    xc = cb_ref[...] + x_m2 * cw[0:1] + x_m1 * cw[1:2] + x * cw[2:3] + x_p1 * cw[3:4]

    lam = lam_ref[...]
    softplus_neg_lam = jnp.maximum(-lam, 0.0) + jnp.log1p(jnp.exp(-jnp.abs(lam)))
    row8 = row[:, :LANES] % 8
    for j in range(RG_CB // LANES):
        sl = slice(j * LANES, (j + 1) * LANES)
        xj = xc[:, sl]
        gates = jnp.dot(xj.astype(BF16), wg_ref[j], preferred_element_type=F32)
        r = _sigmoid(gates[:, :LANES] + ba_ref[:, sl])
        ig = _sigmoid(gates[:, LANES:] + bi_ref[:, sl])
        log_a = -RG_C * r * softplus_neg_lam[:, sl]
        a = jnp.exp(log_a)
        th = jnp.tanh(log_a)
        b = jnp.sqrt(-2.0 * th / (1.0 - th)) * (ig * xj)
        for d in (1, 2, 4):
            if reverse:
                a_s, b_s = pltpu.roll(a, rows - d, 0), pltpu.roll(b, rows - d, 0)
                ok = row8 < 8 - d
            else:
                a_s, b_s = pltpu.roll(a, d, 0), pltpu.roll(b, d, 0)
                ok = row8 >= d
            b = jnp.where(ok, a * b_s + b, b)
            a = jnp.where(ok, a * a_s, a)
        a_scr[:, sl] = a
        b_scr[:, sl] = b

    @pl.when(t == 0)
    def _():
        carry_scr[...] = jnp.broadcast_to(h0_ref[...], (8, RG_CB))

    n_grp = rows // 8

    def group(i, carry):
        g = (n_grp - 1 - i) if reverse else i
        off = pl.multiple_of(g * 8, 8)
        h = a_scr[pl.ds(off, 8), :] * carry + b_scr[pl.ds(off, 8), :]
        b_scr[pl.ds(off, 8), :] = h
        last = h[0:1] if reverse else h[7:8]
        return jnp.broadcast_to(last, (8, RG_CB))

    carry = lax.fori_loop(0, n_grp, group, carry_scr[...])
    carry_scr[...] = carry
    h_ref[...] = b_scr[...].astype(h_ref.dtype)
    hend_ref[...] = carry[0:1]


def _rglru(z3, conv_w, conv_b, wg, b_a, b_i, lam, h0, *, direction, tile):
    bsz, seq_len, _ = z3.shape
    n_t = seq_len // tile
    reverse = direction == 1
    n_cb = D // RG_CB
    xcol = ZXR // RG_CB
    hpt = tile // RG_HALO
    n_halo = seq_len // RG_HALO

    def tix(t):
        return (n_t - 1 - t) if reverse else t

    vec = lambda: pl.BlockSpec((None, 1, RG_CB), lambda b, c, t: (direction, 0, c))
    return pl.pallas_call(
        functools.partial(_rglru_body, reverse=reverse, n_t=n_t),
        grid=(bsz, n_cb, n_t),
        in_specs=[
            pl.BlockSpec((None, tile, RG_CB), lambda b, c, t: (b, tix(t), xcol + c)),
            pl.BlockSpec((None, RG_HALO, RG_CB),
                         lambda b, c, t: (b, jnp.maximum(tix(t) * hpt - 1, 0), xcol + c)),
            pl.BlockSpec((None, RG_HALO, RG_CB),
                         lambda b, c, t: (b, jnp.minimum((tix(t) + 1) * hpt, n_halo - 1), xcol + c)),
            pl.BlockSpec((4, RG_CB), lambda b, c, t: (0, c)),
            pl.BlockSpec((1, RG_CB), lambda b, c, t: (0, c)),
            pl.BlockSpec((None, RG_CB // LANES, LANES, 2 * LANES), lambda b, c, t: (direction, c, 0, 0)),
            vec(), vec(), vec(),
            pl.BlockSpec((None, 1, RG_CB), lambda b, c, t: (b, 0, c)),
        ],
        out_specs=[pl.BlockSpec((None, tile, RG_CB), lambda b, c, t: (b, tix(t), c)),
                   pl.BlockSpec((None, 1, RG_CB), lambda b, c, t: (b, 0, c))],
        out_shape=[jax.ShapeDtypeStruct((bsz, seq_len, D), BF16),
                   jax.ShapeDtypeStruct((bsz, 1, D), F32)],
        scratch_shapes=[pltpu.VMEM((tile, RG_CB), F32), pltpu.VMEM((tile, RG_CB), F32),
                        pltpu.VMEM((8, RG_CB), F32)],
        compiler_params=_params(("parallel", "parallel", "arbitrary")),
        name="rglru_bwd" if reverse else "rglru_fwd",
    )(z3, z3, z3, conv_w, conv_b.reshape(1, D), wg, b_a.reshape(2, 1, D), b_i.reshape(2, 1, D),
      lam.reshape(2, 1, D), h0)


def _gate_weights(w_a, w_i):
    def pair(w):
        w = w.reshape(2, 8, 2, 64, 64)
        z = jnp.zeros_like(w[:, :, 0])
        top = jnp.concatenate([w[:, :, 0], z], axis=-1)
        bot = jnp.concatenate([z, w[:, :, 1]], axis=-1)
        return jnp.concatenate([top, bot], axis=-2)
    return jnp.concatenate([pair(w_a), pair(w_i)], axis=-1).astype(BF16)


def _seqdft_body(c_ref, s_ref, xc_ref, xs_ref, o_ref, acc_ref):
    k = pl.program_id(2)

    @pl.when(k == 0)
    def _():
        acc_ref[...] = jnp.zeros_like(acc_ref)

    acc_ref[...] += (jnp.dot(c_ref[...], xc_ref[...], preferred_element_type=F32)
                     + jnp.dot(s_ref[...], xs_ref[...], preferred_element_type=F32))

    @pl.when(k == pl.num_programs(2) - 1)
    def _():
        o_ref[...] = acc_ref[...].astype(o_ref.dtype)


def _seq_dft(z3, cmat, nsmat, tile):
    bsz, seq_len, _ = z3.shape
    nt = seq_len // tile
    return pl.pallas_call(
        _seqdft_body,
        grid=(bsz, nt, nt),
        in_specs=[pl.BlockSpec((tile, tile), lambda b, i, k: (i, k)),
                  pl.BlockSpec((tile, tile), lambda b, i, k: (i, k)),
                  pl.BlockSpec((None, tile, D), lambda b, i, k: (b, k, ZXC // D)),
                  pl.BlockSpec((None, tile, D), lambda b, i, k: (b, k, ZXS // D))],
        out_specs=pl.BlockSpec((None, tile, D), lambda b, i, k: (b, i, 0)),
        out_shape=jax.ShapeDtypeStruct((bsz, seq_len, D), BF16),
        scratch_shapes=[pltpu.VMEM((tile, D), F32)],
        compiler_params=_params(("parallel", "parallel", "arbitrary")),
        name="seq_dft",
    )(cmat, nsmat, z3, z3)


def _dft_mats(n, scale):
    idx = jnp.arange(n, dtype=jnp.int32)
    ang = ((idx[:, None] * idx[None, :]) % n).astype(F32) * (2.0 * math.pi / n)
    return (jnp.cos(ang) * scale), (jnp.sin(ang) * scale)


FFT_L1 = 32
FFT_GRP = 16
FFT_K1B = 8


def _fft1_body(wa_ref, wb_ref, xc_ref, xs_ref, o_ref):
    rows = FFT_L1 * FFT_GRP
    xc = xc_ref[...].reshape(rows, D)
    xs = xs_ref[...].reshape(rows, D)
    a = (jnp.dot(wa_ref[...], xc, preferred_element_type=F32)
         + jnp.dot(wb_ref[...], xs, preferred_element_type=F32))
    a = a.astype(BF16)
    o_ref[:, :, :D] = a[:rows].reshape(FFT_L1, FFT_GRP, D)
    o_ref[:, :, D:] = a[rows:].reshape(FFT_L1, FFT_GRP, D)


def _fft2_body(m_ref, a_ref, o_ref):
    l2 = m_ref.shape[1]
    for j in range(FFT_K1B):
        o_ref[j] = (jnp.dot(m_ref[j, :, :l2], a_ref[j, :, :D], preferred_element_type=F32)
                    + jnp.dot(m_ref[j, :, l2:], a_ref[j, :, D:], preferred_element_type=F32)
                    ).astype(o_ref.dtype)


def _fft_tables(seq_len):
    l1, l2, g = FFT_L1, seq_len // FFT_L1, FFT_GRP
    i1 = jnp.arange(l1, dtype=jnp.int32)
    ang1 = ((i1[:, None] * i1[None, :]) % l1).astype(F32) * (2.0 * math.pi / l1)
    w1r, w1i = jnp.cos(ang1), -jnp.sin(ang1)
    eye = jnp.eye(g, dtype=F32)
    kron = lambda w: jnp.kron(w, eye)
    wa = jnp.concatenate([kron(w1r), kron(w1i)], axis=0).astype(BF16)
    wb = jnp.concatenate([kron(w1i), kron(-w1r)], axis=0).astype(BF16)
    i2 = jnp.arange(l2, dtype=jnp.int32)
    num = (i2[None, :, None] * i2[None, None, :] * l1 + i1[:, None, None] * i2[None, None, :]) % seq_len
    ang2 = num.astype(F32) * (2.0 * math.pi / seq_len)
    scale = seq_len ** -0.5
    m2 = jnp.concatenate([jnp.cos(ang2) * scale, jnp.sin(ang2) * scale], axis=2).astype(BF16)
    return wa, wb, m2


def _seq_fft(z, bsz, seq_len, tabs):
    wa, wb, m2 = tabs
    l1, l2, g = FFT_L1, seq_len // FFT_L1, FFT_GRP
    z4 = z.reshape(bsz, l1, l2, Z_COLS)
    rows = l1 * g
    a4 = pl.pallas_call(
        _fft1_body,
        grid=(bsz, l2 // g),
        in_specs=[_resident((2 * rows, rows)), _resident((2 * rows, rows)),
                  pl.BlockSpec((None, l1, g, D), lambda b, t: (b, 0, t, ZXC // D)),
                  pl.BlockSpec((None, l1, g, D), lambda b, t: (b, 0, t, ZXS // D))],
        out_specs=pl.BlockSpec((None, l1, g, 2 * D), lambda b, t: (b, 0, t, 0)),
        out_shape=jax.ShapeDtypeStruct((bsz, l1, l2, 2 * D), BF16),
        compiler_params=_params(("parallel", "parallel")),
        name="seq_fft_stage1",
    )(wa, wb, z4, z4)
    yp = pl.pallas_call(
        _fft2_body,
        grid=(l1 // FFT_K1B, bsz),
        in_specs=[pl.BlockSpec((FFT_K1B, l2, 2 * l2), lambda k, b: (k, 0, 0)),
                  pl.BlockSpec((None, FFT_K1B, l2, 2 * D), lambda k, b: (b, k, 0, 0))],
        out_specs=pl.BlockSpec((None, FFT_K1B, l2, D), lambda k, b: (b, k, 0, 0)),
        out_shape=jax.ShapeDtypeStruct((bsz, l1, l2, D), BF16),
        compiler_params=_params(("parallel", "parallel")),
        name="seq_fft_stage2",
    )(m2, a4)
    return yp.transpose(0, 2, 1, 3).reshape(bsz, seq_len, D)


def _gelu_tanh(x):
    return 0.5 * x * (1.0 + jnp.tanh(math.sqrt(2.0 / math.pi) * (x + 0.044715 * (x * x * x))))


def _merge_body(x_ref, mod_ref, attn_ref, gr_ref, hf_ref, hb_ref, four_ref,
                woa_ref, wor_ref, wof_ref, wm_ref, bm_ref, wout_ref, lng_ref, lnb_ref, wr_ref, br_ref,
                x1_ref, u2_ref, eid_ref, gate_ref):
    mod = mod_ref[...]
    x = x_ref[...]
    u = (x * (1.0 + mod[1:2]) + mod[0:1]).astype(BF16)
    rg_in = (_gelu_tanh(gr_ref[...].astype(F32))
             * (hf_ref[...].astype(F32) + hb_ref[...].astype(F32))).astype(BF16)
    branches = ((attn_ref[...], woa_ref), (rg_in, wor_ref), (four_ref[...], wof_ref))
    merged = None
    for j, (inp, w_ref) in enumerate(branches):
        y = jnp.dot(inp, w_ref[...], preferred_element_type=F32)
        g = _sigmoid(jnp.dot(u, wm_ref[:, j * D:(j + 1) * D], preferred_element_type=F32)
                     + bm_ref[:, j * D:(j + 1) * D])
        merged = g * y if merged is None else merged + g * y
    y = jnp.dot(merged.astype(BF16), wout_ref[...], preferred_element_type=F32)
    x1 = _layer_norm(DEEPNORM_ALPHA * x + mod[2:3] * y, lng_ref[...], lnb_ref[...])
    x1_ref[...] = x1
    u2 = (x1 * (1.0 + mod[4:5]) + mod[3:4]).astype(BF16)
    u2_ref[...] = u2

    logits = jnp.dot(u2, wr_ref[...], preferred_element_type=F32) + br_ref[...]
    lane = lax.broadcasted_iota(jnp.int32, logits.shape, 1)
    vals, idxs = [], []
    for _ in range(TOP_K):
        m = jnp.max(logits, axis=-1, keepdims=True)
        idx = jnp.min(jnp.where(logits == m, lane, LANES), axis=-1, keepdims=True)
        vals.append(m)
        idxs.append(idx)
        logits = jnp.where(lane == idx, -3.0e38, logits)
    exps = [jnp.exp(v - vals[0]) for v in vals]
    den = exps[0] + exps[1] + exps[2] + exps[3]
    eid = jnp.zeros(lane.shape, jnp.int32)
    gate = jnp.zeros(lane.shape, F32)
    for k in range(TOP_K):
        eid = jnp.where(lane == k, idxs[k], eid)
        gate = jnp.where(lane == k, exps[k] / den, gate)
    eid_ref[...] = eid
    gate_ref[...] = gate


def _merge(x2, mods, attn, z, hf, hb, four, p, seq_len, tm):
    n = x2.shape[0]
    tpb = seq_len // tm
    bm = mods.shape[0]
    mod_map = (lambda i: (i // tpb, 0, 0)) if bm > 1 else (lambda i: (0, 0, 0))
    act = lambda: pl.BlockSpec((tm, D), lambda i: (i, 0))
    return pl.pallas_call(
        _merge_body,
        grid=(n // tm,),
        in_specs=[act(), pl.BlockSpec((None, 6, D), mod_map), act(),
                  pl.BlockSpec((tm, D), lambda i: (i, ZGR // D)), act(), act(), act(),
                  _resident((D, D)), _resident((D, D)), _resident((D, D)),
                  _resident((D, 3 * D)), _resident((1, 3 * D)), _resident((D, D)),
                  _resident((1, D)), _resident((1, D)), _resident((D, LANES)), _resident((1, LANES))],
        out_specs=[act(), act(), pl.BlockSpec((tm, LANES), lambda i: (i, 0)),
                   pl.BlockSpec((tm, LANES), lambda i: (i, 0))],
        out_shape=[jax.ShapeDtypeStruct((n, D), F32), jax.ShapeDtypeStruct((n, D), BF16),
                   jax.ShapeDtypeStruct((n, LANES), jnp.int32), jax.ShapeDtypeStruct((n, LANES), F32)],
        compiler_params=_params(("parallel",)),
        name="merge_ln_router",
    )(x2, mods, attn, z, hf, hb, four, p["w_o_attn"], p["w_o_rg"], p["w_o_four"], p["w_merge"],
      p["b_merge"], p["w_out"], p["ln1_g"], p["ln1_b"], p["w_router"], p["b_router"])


RUN_ALIGN = 8


def _sorted_rows(td):
    return TOP_K * td + N_EXPERTS * RUN_ALIGN


def _chunk_bits(max_len):
    bits, b = [], RUN_ALIGN
    while b <= max_len:
        bits.append(b)
        b *= 2
    return tuple(reversed(bits))


def _chunked_copies(wait, src_ref, dst_ref, src_off, dst_off, length, bits, sem):
    for bit in bits:
        @pl.when((length & bit) != 0)
        def _():
            off = length & (-2 * bit)
            cp = pltpu.make_async_copy(src_ref.at[pl.ds(pl.multiple_of(src_off + off, RUN_ALIGN), bit)],
                                       dst_ref.at[pl.ds(pl.multiple_of(dst_off + off, RUN_ALIGN), bit)], sem)
            if wait:
                cp.wait()
            else:
                cp.start()


def _slot_positions(eid, toff, ltri):
    lane = lax.broadcasted_iota(jnp.int32, eid.shape, 1)
    hits = [eid[:, k:k + 1] == lane for k in range(TOP_K)]
    member = jnp.zeros(eid.shape, F32)
    for h in hits:
        member = member + jnp.where(h, 1.0, 0.0)
    rank = jnp.dot(ltri, member.astype(BF16), preferred_element_type=F32)
    base = toff + rank
    return [jnp.sum(jnp.where(h, base, 0.0), axis=-1, keepdims=True) for h in hits]


def _dispatch_body(rl_ref, to_ref, ro_ref, tlo_ref, tll_ref, nu_ref, u_ref, eid_ref, toff_ref, ltri_ref,
                   xs_hbm, pos_ref, srt, sems, *, td, tb):
    t = pl.program_id(0)
    n_t = pl.num_programs(0)
    slot = t % 2
    rows = srt.shape[1]
    run_bits = _chunk_bits(td)

    def run_copies(wait, tile, s):
        def one(e, c):
            i = tile * N_EXPERTS + e
            _chunked_copies(wait, srt.at[s], xs_hbm, to_ref[i], ro_ref[i], rl_ref[i], run_bits, sems.at[s])
            return c
        lax.fori_loop(0, N_EXPERTS, one, 0)

    @pl.when(t == 0)
    def _():
        srt[1, :tb, :] = jnp.zeros((tb, D), F32)
        tail_bits = _chunk_bits(tb)
        for wait in (False, True):
            def one(e, c):
                _chunked_copies(wait, srt.at[1], xs_hbm, 0, tlo_ref[e], tll_ref[e], tail_bits, sems.at[1])
                return c
            lax.fori_loop(0, N_EXPERTS, one, 0)
        n_blocks = xs_hbm.shape[0] // tb
        for wait in (False, True):
            def blk(j, c):
                cp = pltpu.make_async_copy(srt.at[1, pl.ds(0, tb)],
                                           xs_hbm.at[pl.ds(pl.multiple_of(j * tb, RUN_ALIGN), tb)], sems.at[1])
                if wait:
                    cp.wait()
                else:
                    cp.start()
                return c
            lax.fori_loop(nu_ref[0], n_blocks, blk, 0)

    pos = _slot_positions(eid_ref[...], toff_ref[...], ltri_ref[...])
    lane = lax.broadcasted_iota(jnp.int32, (td, LANES), 1)
    pos_out = jnp.zeros((td, LANES), jnp.int32)
    for k in range(TOP_K):
        pos_out = jnp.where(lane == k, pos[k].astype(jnp.int32), pos_out)
    pos_ref[...] = pos_out

    col = lax.broadcasted_iota(jnp.int32, (td, rows), 1).astype(F32)
    hit = col == pos[0]
    for k in range(1, TOP_K):
        hit = hit | (col == pos[k])
    onehot_t = jnp.where(hit, 1.0, 0.0).astype(BF16)
    srt[slot] = lax.dot_general(onehot_t, u_ref[...], (((0,), (0,)), ((), ())),
                                preferred_element_type=F32)
    run_copies(False, t, slot)

    @pl.when(t >= 1)
    def _():
        run_copies(True, t - 1, 1 - slot)

    @pl.when(t == n_t - 1)
    def _():
        run_copies(True, t, slot)


def _dispatch(u2, eid128, rt, td, tb):
    n_tok = u2.shape[0]
    n_t = n_tok // td
    rows = _sorted_rows(td)
    ltri = (jnp.arange(td)[:, None] > jnp.arange(td)[None, :]).astype(BF16)
    grid_spec = pltpu.PrefetchScalarGridSpec(
        num_scalar_prefetch=6,
        grid=(n_t,),
        in_specs=[pl.BlockSpec((td, D), lambda t, *_: (t, 0)),
                  pl.BlockSpec((td, LANES), lambda t, *_: (t, 0)),
                  pl.BlockSpec((None, 1, LANES), lambda t, *_: (t, 0, 0)),
                  pl.BlockSpec((td, td), lambda t, *_: (0, 0), pipeline_mode=pl.Buffered(1))],
        out_specs=[pl.BlockSpec(memory_space=pl.ANY),
                   pl.BlockSpec((td, LANES), lambda t, *_: (t, 0))],
        scratch_shapes=[pltpu.VMEM((2, rows, D), F32), pltpu.SemaphoreType.DMA((2,))],
    )
    return pl.pallas_call(
        functools.partial(_dispatch_body, td=td, tb=tb),
        grid_spec=grid_spec,
        out_shape=[jax.ShapeDtypeStruct((rt["n_slots"], D), F32),
                   jax.ShapeDtypeStruct((n_tok, LANES), jnp.int32)],
        compiler_params=_params(("arbitrary",)),
        name="moe_dispatch",
    )(rt["run_len"], rt["tile_off"], rt["run_off"], rt["tail_off"], rt["tail_len"], rt["n_used"],
      u2, eid128, rt["toff_f"], ltri)


def _moe_body(be_ref, nused_ref, xs_ref, wgu_ref, bgu_ref, wdn_ref, bdn_ref, ys_ref):
    i = pl.program_id(0)

    @pl.when(i < nused_ref[0])
    def _():
        x = xs_ref[...].astype(BF16)
        h = jnp.dot(x, wgu_ref[...], preferred_element_type=F32) + bgu_ref[...]
        half = h.shape[1] // 2
        x_glu = jnp.minimum(h[:, :half], SWIGLU_LIMIT)
        x_lin = jnp.clip(h[:, half:], -SWIGLU_LIMIT, SWIGLU_LIMIT)
        act = (x_glu * _sigmoid(SWIGLU_ALPHA * x_glu) * (x_lin + 1.0)).astype(BF16)
        ys_ref[...] = jnp.dot(act, wdn_ref[...], preferred_element_type=F32) + bdn_ref[...]

    @pl.when(i >= nused_ref[0])
    def _():
        ys_ref[...] = jnp.zeros_like(ys_ref)


def _moe(xs, block_e, n_used, w_gu, b_gu, w_dn, b_dn, tb):
    n_blocks = xs.shape[0] // tb
    d_ff2 = w_gu.shape[2]
    live = lambda i, nu: jnp.minimum(i, nu[0] - 1)
    grid_spec = pltpu.PrefetchScalarGridSpec(
        num_scalar_prefetch=2,
        grid=(n_blocks,),
        in_specs=[pl.BlockSpec((tb, D), lambda i, be, nu: (live(i, nu), 0)),
                  pl.BlockSpec((None, D, d_ff2), lambda i, be, nu: (be[live(i, nu)], 0, 0)),
                  pl.BlockSpec((None, 1, d_ff2), lambda i, be, nu: (be[live(i, nu)], 0, 0)),
                  pl.BlockSpec((None, d_ff2 // 2, D), lambda i, be, nu: (be[live(i, nu)], 0, 0)),
                  pl.BlockSpec((None, 1, D), lambda i, be, nu: (be[live(i, nu)], 0, 0))],
        out_specs=pl.BlockSpec((tb, D), lambda i, be, nu: (i, 0)),
    )
    return pl.pallas_call(
        _moe_body,
        grid_spec=grid_spec,
        out_shape=jax.ShapeDtypeStruct(xs.shape, F32),
        compiler_params=_params(("arbitrary",)),
        name="moe_experts",
    )(block_e, n_used, xs, w_gu, b_gu, w_dn, b_dn)


def _route(eid, n_tok, td, tb):
    n_t = n_tok // td
    onehot = (eid[:, :, None] == jnp.arange(N_EXPERTS, dtype=jnp.int32)).astype(jnp.int32)
    cnt = onehot.sum(axis=1).reshape(n_t, td, N_EXPERTS).sum(axis=1)
    run = (cnt + RUN_ALIGN - 1) // RUN_ALIGN * RUN_ALIGN
    tile_off = jnp.cumsum(run, axis=1) - run
    total = run.sum(axis=0)
    padded = (total + tb - 1) // tb * tb
    pend = jnp.cumsum(padded)
    pstart = pend - padded
    run_off = pstart[None, :] + jnp.cumsum(run, axis=0) - run
    worst = TOP_K * n_tok + n_t * N_EXPERTS * RUN_ALIGN + N_EXPERTS * tb
    n_slots = (worst + tb - 1) // tb * tb
    first_slot = jnp.arange(n_slots // tb, dtype=jnp.int32) * tb
    block_e = jnp.minimum(jnp.sum((pend[None, :] <= first_slot[:, None]).astype(jnp.int32), axis=1),
                          N_EXPERTS - 1)
    toff_f = jnp.zeros((n_t, 1, LANES), F32).at[:, 0, :N_EXPERTS].set(tile_off.astype(F32))
    i32 = lambda a: a.astype(jnp.int32).reshape(-1)
    return dict(run_len=i32(run), tile_off=i32(tile_off), run_off=i32(run_off),
                tail_off=i32(pstart + total), tail_len=i32(padded - total), toff_f=toff_f,
                block_e=i32(block_e), n_used=i32(pend[-1] // tb), n_slots=n_slots)


def _combine_body(rl_ref, to_ref, ro_ref, ys_hbm, x1_ref, mod_ref, gate_ref, pos_ref, lng_ref, lnb_ref,
                  o_ref, srt, sems, *, td):
    t = pl.program_id(0)
    n_t = pl.num_programs(0)
    slot = t % 2
    rows = srt.shape[1]
    run_bits = _chunk_bits(td)

    def run_copies(wait, tile, s):
        def one(e, c):
            i = tile * N_EXPERTS + e
            _chunked_copies(wait, ys_hbm, srt.at[s], ro_ref[i], to_ref[i], rl_ref[i], run_bits, sems.at[s])
            return c
        lax.fori_loop(0, N_EXPERTS, one, 0)

    @pl.when(t == 0)
    def _():
        srt[...] = jnp.zeros_like(srt)
        run_copies(False, 0, 0)

    @pl.when(t + 1 < n_t)
    def _():
        run_copies(False, t + 1, 1 - slot)

    run_copies(True, t, slot)

    gate = gate_ref[...]
    pos = pos_ref[...]
    col = lax.broadcasted_iota(jnp.int32, (td, rows), 1)
    weights = jnp.zeros((td, rows), F32)
    for k in range(TOP_K):
        weights = jnp.where(col == pos[:, k:k + 1], gate[:, k:k + 1], weights)
    f = jnp.dot(weights.astype(BF16), srt[slot].astype(BF16), preferred_element_type=F32)
    mod = mod_ref[...]
    o_ref[...] = _layer_norm(DEEPNORM_ALPHA * x1_ref[...] + mod[5:6] * f, lng_ref[...], lnb_ref[...])


def _combine(x1, mods, gate128, pos128, ys, rt, ln_g, ln_b, seq_len, td):
    n = x1.shape[0]
    tpb = seq_len // td
    bm = mods.shape[0]
    mod_map = (lambda i, *_: (i // tpb, 0, 0)) if bm > 1 else (lambda i, *_: (0, 0, 0))
    grid_spec = pltpu.PrefetchScalarGridSpec(
        num_scalar_prefetch=3,
        grid=(n // td,),
        in_specs=[pl.BlockSpec(memory_space=pl.ANY),
                  pl.BlockSpec((td, D), lambda i, *_: (i, 0)),
                  pl.BlockSpec((None, 6, D), mod_map),
                  pl.BlockSpec((td, LANES), lambda i, *_: (i, 0)),
                  pl.BlockSpec((td, LANES), lambda i, *_: (i, 0)),
                  pl.BlockSpec((1, D), lambda i, *_: (0, 0)),
                  pl.BlockSpec((1, D), lambda i, *_: (0, 0))],
        out_specs=pl.BlockSpec((td, D), lambda i, *_: (i, 0)),
        scratch_shapes=[pltpu.VMEM((2, _sorted_rows(td), D), F32), pltpu.SemaphoreType.DMA((2,))],
    )
    return pl.pallas_call(
        functools.partial(_combine_body, td=td),
        grid_spec=grid_spec,
        out_shape=jax.ShapeDtypeStruct((n, D), F32),
        compiler_params=_params(("arbitrary",)),
        name="combine_ln",
    )(rt["run_len"], rt["tile_off"], rt["run_off"], ys, x1, mods, gate128, pos128, ln_g, ln_b)


def _rope_tables(seq_len):
    pos = jnp.arange(seq_len, dtype=jnp.int32)
    row = (pos // GRID_W).astype(F32)
    col = (pos % GRID_W).astype(F32)
    n_freq = HEAD_DIM // 4
    freqs = ROPE_BASE ** (-jnp.arange(n_freq, dtype=F32) / n_freq)
    ar, ac = row[:, None] * freqs, col[:, None] * freqs
    zero = jnp.zeros_like(ar)
    c = jnp.concatenate([jnp.cos(ar), jnp.cos(ar), jnp.cos(ac), jnp.cos(ac)], axis=1)
    sa = jnp.concatenate([-jnp.sin(ar), zero, -jnp.sin(ac), zero], axis=1)
    sb = jnp.concatenate([zero, jnp.sin(ar), zero, jnp.sin(ac)], axis=1)
    rep = MXU_N // HEAD_DIM
    return tuple(jnp.tile(t, (1, rep)) for t in (c, sa, sb))


def _layer_params(l, w):
    w_router = jnp.zeros((D, LANES), BF16).at[:, :N_EXPERTS].set(w["w_router"][l].astype(BF16))
    b_router = jnp.full((1, LANES), NEG_INF, F32).at[0, :N_EXPERTS].set(w["b_router"][l])
    return dict(
        w_in=w["w_in"][l].astype(BF16),
        w_o_attn=w["w_o_attn"][l].astype(BF16), w_o_rg=w["w_o_rg"][l].astype(BF16),
        w_o_four=w["w_o_four"][l].astype(BF16), w_merge=w["w_merge"][l].astype(BF16),
        b_merge=w["b_merge"][l].reshape(1, 3 * D), w_out=w["w_out"][l].astype(BF16),
        ln1_g=w["ln1_g"][l].reshape(1, D), ln1_b=w["ln1_b"][l].reshape(1, D),
        ln2_g=w["ln2_g"][l].reshape(1, D), ln2_b=w["ln2_b"][l].reshape(1, D),
        w_router=w_router, b_router=b_router,
        w_gu=w["w_gate_up"][l].astype(BF16), b_gu=w["b_gate_up"][l].reshape(N_EXPERTS, 1, -1),
        w_dn=w["w_down"][l].astype(BF16), b_dn=w["b_down"][l].reshape(N_EXPERTS, 1, D),
        wg=_gate_weights(w["rg_w_a"][l], w["rg_w_i"][l]),
        conv_w=w["conv_w"][l], conv_b=w["conv_b"][l], b_a=w["rg_b_a"][l], b_i=w["rg_b_i"][l],
        lam=w["rg_lambda"][l], sink=w["attn_sink"][l],
    )


def _row_tile(seq_len, want):
    return min(want, seq_len)


def _ffn(x1, u2, eid128, gate128, mods, p, seq_len, tb, td):
    rt = _route(eid128[:, :TOP_K], x1.shape[0], td, tb)
    xs, pos128 = _dispatch(u2, eid128, rt, td, tb)
    ys = _moe(xs, rt["block_e"], rt["n_used"], p["w_gu"], p["b_gu"], p["w_dn"], p["b_dn"], tb)
    return _combine(x1, mods, gate128, pos128, ys, rt, p["ln2_g"], p["ln2_b"], seq_len, td)


def kernel(x, c, ctx, c_ctx, w_mod, b_mod, w_in, attn_sink, w_o_attn, conv_w, conv_b, rg_w_a, rg_b_a, rg_w_i, rg_b_i, rg_lambda, w_o_rg, w_o_four, w_merge, b_merge, w_out, ln1_g, ln1_b, w_router, b_router, w_gate_up, b_gate_up, w_down, b_down, ln2_g, ln2_b):
    w = dict(w_mod=w_mod, b_mod=b_mod, w_in=w_in, attn_sink=attn_sink, w_o_attn=w_o_attn, conv_w=conv_w,
             conv_b=conv_b, rg_w_a=rg_w_a, rg_b_a=rg_b_a, rg_w_i=rg_w_i, rg_b_i=rg_b_i, rg_lambda=rg_lambda,
             w_o_rg=w_o_rg, w_o_four=w_o_four, w_merge=w_merge, b_merge=b_merge, w_out=w_out, ln1_g=ln1_g,
             ln1_b=ln1_b, w_router=w_router, b_router=b_router, w_gate_up=w_gate_up, b_gate_up=b_gate_up,
             w_down=w_down, b_down=b_down, ln2_g=ln2_g, ln2_b=ln2_b)
    bsz, seq_len, _ = x.shape
    ctx_len = ctx.shape[1]
    n_lat, n_ctx = bsz * seq_len, bsz * ctx_len
    tm_lat, tm_ctx = _row_tile(seq_len, 512), _row_tile(ctx_len, 256)
    dft_lat, dft_ctx = _row_tile(seq_len, 1024), _row_tile(ctx_len, 256)
    rg_lat, rg_ctx = _row_tile(seq_len, 256), _row_tile(ctx_len, 256)
    tb_lat, tb_ctx = 512, 128

    rope = _rope_tables(seq_len)
    cw, sw = _dft_mats(F_GW, F_GW ** -0.5)
    csw = jnp.concatenate([cw, sw], axis=1).astype(BF16)
    use_fft = seq_len % (FFT_L1 * FFT_GRP) == 0
    if use_fft:
        fft_tabs = _fft_tables(seq_len)
    else:
        cl, sl = _dft_mats(seq_len, seq_len ** -0.5)
        cl, nsl = cl.astype(BF16), (-sl).astype(BF16)
    cc, sc = _dft_mats(ctx_len, ctx_len ** -0.5)
    cc, nsc = cc.astype(BF16), (-sc).astype(BF16)

    cond = jnp.zeros((16, D), F32).at[:bsz].set(c).at[bsz].set(c_ctx)
    x2 = x.reshape(n_lat, D)
    ctx2 = ctx.reshape(n_ctx, D)
    zero_h = jnp.zeros((bsz, 1, D), F32)

    for l in range(DEPTH):
        last = l == DEPTH - 1
        p = _layer_params(l, w)
        m = _adaln(cond, w_mod[l], b_mod[l]).reshape(16, 6, D)
        mods_lat, mods_ctx = m[:bsz], m[bsz:bsz + 1]

        zc = _proj(ctx2, mods_ctx, p["w_in"], csw, None, ctx_len, tm_ctx)
        z = _proj(x2, mods_lat, p["w_in"], csw, rope, seq_len, tm_lat)
        zc3 = zc.reshape(bsz, ctx_len, Z_COLS)
        z3 = z.reshape(bsz, seq_len, Z_COLS)

        rg = lambda zz, h0, d, tile: _rglru(zz, p["conv_w"], p["conv_b"], p["wg"], p["b_a"], p["b_i"],
                                            p["lam"], h0, direction=d, tile=tile)
        hcf, endf = rg(zc3, zero_h, 0, rg_ctx)
        hcb, endb = rg(zc3, zero_h, 1, rg_ctx)
        hf, _ = rg(z3, endf, 0, rg_lat)
        hb, _ = rg(z3, endb, 1, rg_lat)

        attn = _attention(z3, zc3, p["sink"])
        four = _seq_fft(z, bsz, seq_len, fft_tabs) if use_fft else _seq_dft(z3, cl, nsl, dft_lat)
        x1, u2, eid, gate = _merge(x2, mods_lat, attn.reshape(n_lat, D), z, hf.reshape(n_lat, D),
                                   hb.reshape(n_lat, D), four.reshape(n_lat, D), p, seq_len, tm_lat)
        x2 = _ffn(x1, u2, eid, gate, mods_lat, p, seq_len, tb_lat, tm_lat)

        if not last:
            attn_c = _ctx_attention(zc3, p["sink"])
            four_c = _seq_dft(zc3, cc, nsc, dft_ctx)
            c1, uc2, eid_c, gate_c = _merge(ctx2, mods_ctx, attn_c.reshape(n_ctx, D), zc,
                                            hcf.reshape(n_ctx, D), hcb.reshape(n_ctx, D),
                                            four_c.reshape(n_ctx, D), p, ctx_len, tm_ctx)
            ctx2 = _ffn(c1, uc2, eid_c, gate_c, mods_ctx, p, ctx_len, tb_ctx, tm_ctx)

    return x2.reshape(bsz, seq_len, D)
```

```python
import functools
import math

import jax
import jax.numpy as jnp
from jax import lax
from jax.experimental import pallas as pl
from jax.experimental.pallas import tpu as pltpu

F32 = jnp.float32
BF16 = jnp.bfloat16

D = 1024
HEAD_DIM = 64
N_HEADS = 16
N_KV = 4
Q_PER_KV = 4
WINDOW = 128
QBLK = 128
assert WINDOW == QBLK
GRID_W = 64
ROPE_BASE = 10000.0
RG_C = 8.0
CONV_LEFT = 2
F_GROUPS = 4
F_GW = 256
N_EXPERTS = 32
TOP_K = 4
SWIGLU_LIMIT = 7.0
SWIGLU_ALPHA = 1.702
LN_EPS = 1e-5
DEPTH = 2
DEEPNORM_ALPHA = (2 * DEPTH) ** 0.25
NEG_INF = -1e30

Q_OFF, K_OFF, V_OFF, XR_OFF, GR_OFF, XF_OFF, IN_COLS = 0, 1024, 1280, 1536, 2560, 3584, 4608
ZQ, ZXR, ZGR, ZXC, ZXS, ZK, ZV, Z_COLS = 0, 1024, 2048, 3072, 4096, 5120, 5376, 5632

VMEM_LIMIT_V7X = 56 * 1024 * 1024
LANES = 128
MXU_N = 256


def _params(sem, vmem=VMEM_LIMIT_V7X):
    return pltpu.CompilerParams(dimension_semantics=sem, vmem_limit_bytes=vmem)


def _resident(shape):
    nd = len(shape)
    return pl.BlockSpec(shape, lambda *_: (0,) * nd, pipeline_mode=pl.Buffered(1))


def _sigmoid(x):
    return 1.0 / (1.0 + jnp.exp(-x))


def _layer_norm(r, g, b):
    mu = jnp.mean(r, axis=-1, keepdims=True)
    rc = r - mu
    var = jnp.mean(rc * rc, axis=-1, keepdims=True)
    return rc * lax.rsqrt(var + LN_EPS) * g + b


def _adaln_body(c_ref, w_ref, b_ref, o_ref):
    c = c_ref[...]
    s = (c * _sigmoid(c)).astype(BF16)
    o_ref[...] = jnp.dot(s, w_ref[...].astype(BF16), preferred_element_type=F32) + b_ref[...]


def _adaln(cond, w_mod, b_mod):
    rows, n = cond.shape[0], w_mod.shape[1]
    tn = 1024
    return pl.pallas_call(
        _adaln_body,
        grid=(n // tn,),
        in_specs=[pl.BlockSpec((rows, D), lambda j: (0, 0)),
                  pl.BlockSpec((D, tn), lambda j: (0, j)),
                  pl.BlockSpec((1, tn), lambda j: (0, j))],
        out_specs=pl.BlockSpec((rows, tn), lambda j: (0, j)),
        out_shape=jax.ShapeDtypeStruct((rows, n), F32),
        compiler_params=_params(("arbitrary",)),
        name="adaln",
    )(cond, w_mod, b_mod.reshape(1, n))


def _proj_dst(j):
    if j < 4:
        return ZQ + j * MXU_N
    if j == 4:
        return ZK
    if j == 5:
        return ZV
    if j < 10:
        return ZXR + (j - 6) * MXU_N
    if j < 14:
        return ZGR + (j - 10) * MXU_N
    return None


def _proj_body(x_ref, mod_ref, w_ref, csw_ref, *rest, rope):
    if rope:
        c_ref, sa_ref, sb_ref, o_ref = rest
    else:
        (o_ref,) = rest
    mod = mod_ref[...]
    u = (x_ref[...] * (1.0 + mod[1:2]) + mod[0:1]).astype(BF16)
    for j in range(IN_COLS // MXU_N):
        acc = jnp.dot(u, w_ref[:, j * MXU_N:(j + 1) * MXU_N], preferred_element_type=F32)
        if rope and j < 5:
            acc = (acc * c_ref[...] + pltpu.roll(acc, MXU_N - 16, 1) * sa_ref[...]
                   + pltpu.roll(acc, 16, 1) * sb_ref[...])
        dst = _proj_dst(j)
        if dst is not None:
            o_ref[:, dst:dst + MXU_N] = acc.astype(BF16)
        else:
            g = j - 14
            t = jnp.dot(acc.astype(BF16), csw_ref[...], preferred_element_type=F32)
            o_ref[:, ZXC + g * F_GW:ZXC + (g + 1) * F_GW] = t[:, :F_GW].astype(BF16)
            o_ref[:, ZXS + g * F_GW:ZXS + (g + 1) * F_GW] = t[:, F_GW:].astype(BF16)


def _proj(x2, mods, w_in, csw, rope_tabs, seq_len, tm):
    n = x2.shape[0]
    tpb = seq_len // tm
    bm = mods.shape[0]
    mod_map = (lambda i: (i // tpb, 0, 0)) if bm > 1 else (lambda i: (0, 0, 0))
    in_specs = [pl.BlockSpec((tm, D), lambda i: (i, 0)),
                pl.BlockSpec((None, 6, D), mod_map),
                _resident((D, IN_COLS)),
                _resident((F_GW, 2 * F_GW))]
    args = [x2, mods, w_in, csw]
    if rope_tabs is not None:
        in_specs += [pl.BlockSpec((tm, MXU_N), lambda i: (i % tpb, 0))] * 3
        args += list(rope_tabs)
    return pl.pallas_call(
        functools.partial(_proj_body, rope=rope_tabs is not None),
        grid=(n // tm,),
        in_specs=in_specs,
        out_specs=pl.BlockSpec((tm, Z_COLS), lambda i: (i, 0)),
        out_shape=jax.ShapeDtypeStruct((n, Z_COLS), BF16),
        compiler_params=_params(("parallel",)),
        name="proj_rope" if rope_tabs is not None else "proj_ctx",
    )(*args)


def _group_attention(sink_ref, q_ref, o_ref, g, parts):
    rows = q_ref.shape[0]
    key_low = g % 2 == 0
    lane = lax.broadcasted_iota(jnp.int32, (1, LANES), 1)
    keep = (lane < HEAD_DIM) if key_low else (lane >= HEAD_DIM)
    kv_lanes = slice((g // 2) * LANES, (g // 2 + 1) * LANES)

    blocks = []
    for j in range(2):
        q32 = q_ref[:, (2 * g + j) * LANES:(2 * g + j + 1) * LANES].astype(F32) * (HEAD_DIM ** -0.5)
        same, rot = q32.astype(BF16), pltpu.roll(q32, HEAD_DIM, 1).astype(BF16)
        blocks += [same, rot] if key_low else [rot, same]
    qs = jnp.concatenate(blocks, axis=0)

    row = lax.broadcasted_iota(jnp.int32, (Q_PER_KV * rows, 1), 0)
    sink = jnp.full((Q_PER_KV * rows, 1), sink_ref[g * Q_PER_KV + Q_PER_KV - 1], F32)
    for r in range(Q_PER_KV - 2, -1, -1):
        sink = jnp.where(row < (r + 1) * rows, sink_ref[g * Q_PER_KV + r], sink)

    def lane_blocks(a):
        return [a[:, c * LANES:(c + 1) * LANES] for c in range(a.shape[1] // LANES)]

    scores = []
    m_fold = None
    for k_ref, _, bias in parts:
        k2 = jnp.where(keep, k_ref[:, kv_lanes], jnp.zeros((), BF16))
        s = lax.dot_general(qs, k2, (((1,), (1,)), ((), ())), preferred_element_type=F32)
        if bias is not None:
            s = s + bias
        scores.append(s)
        for blk in lane_blocks(s):
            m_fold = blk if m_fold is None else jnp.maximum(m_fold, blk)
    m = jnp.maximum(sink, jnp.max(m_fold, axis=-1, keepdims=True))
    p_fold = None
    out = None
    for s, (_, v_ref, _) in zip(scores, parts):
        p = jnp.exp(s - m)
        for blk in lane_blocks(p):
            p_fold = blk if p_fold is None else p_fold + blk
        v2 = jnp.where(keep, v_ref[:, kv_lanes], jnp.zeros((), BF16))
        o = jnp.dot(p.astype(BF16), v2, preferred_element_type=F32)
        out = o if out is None else out + o
    den = jnp.exp(sink - m) + jnp.sum(p_fold, axis=-1, keepdims=True)
    out = out / den
    for j in range(2):
        first, second = out[2 * j * rows:(2 * j + 1) * rows], out[(2 * j + 1) * rows:(2 * j + 2) * rows]
        if key_low:
            both = first + pltpu.roll(second, HEAD_DIM, 1)
        else:
            both = pltpu.roll(first, HEAD_DIM, 1) + second
        o_ref[:, (2 * g + j) * LANES:(2 * g + j + 1) * LANES] = both.astype(o_ref.dtype)


def _attn_body(sink_ref, q_ref, kl_ref, km_ref, kr_ref, vl_ref, vm_ref, vr_ref, kc_ref, vc_ref,
               o_ref, *, n_blk):
    n = pl.program_id(1)
    i = lax.broadcasted_iota(jnp.int32, (Q_PER_KV * QBLK, QBLK), 0) % QBLK
    j = lax.broadcasted_iota(jnp.int32, (Q_PER_KV * QBLK, QBLK), 1)
    bias_prev = jnp.where((j >= i) & (n > 0), 0.0, NEG_INF).astype(F32)
    bias_next = jnp.where((j <= i) & (n < n_blk - 1), 0.0, NEG_INF).astype(F32)
    parts = ((kl_ref, vl_ref, bias_prev), (km_ref, vm_ref, None), (kr_ref, vr_ref, bias_next),
             (kc_ref, vc_ref, None))
    for g in range(N_KV):
        _group_attention(sink_ref, q_ref, o_ref, g, parts)


def _attention(z3, zc3, sink):
    bsz, seq_len, _ = z3.shape
    ctx_len = zc3.shape[1]
    nb = seq_len // QBLK
    kcol, vcol = ZK // MXU_N, ZV // MXU_N

    def blk(col, off):
        return pl.BlockSpec((None, QBLK, MXU_N),
                            lambda b, n: (b, jnp.clip(n + off, 0, nb - 1), col))

    return pl.pallas_call(
        functools.partial(_attn_body, n_blk=nb),
        grid=(bsz, nb),
        in_specs=[pl.BlockSpec(memory_space=pltpu.SMEM),
                  pl.BlockSpec((None, QBLK, D), lambda b, n: (b, n, 0)),
                  blk(kcol, -1), blk(kcol, 0), blk(kcol, 1),
                  blk(vcol, -1), blk(vcol, 0), blk(vcol, 1),
                  pl.BlockSpec((None, ctx_len, MXU_N), lambda b, n: (b, 0, kcol)),
                  pl.BlockSpec((None, ctx_len, MXU_N), lambda b, n: (b, 0, vcol))],
        out_specs=pl.BlockSpec((None, QBLK, D), lambda b, n: (b, n, 0)),
        out_shape=jax.ShapeDtypeStruct((bsz, seq_len, D), BF16),
        compiler_params=_params(("parallel", "parallel")),
        name="window_attn",
    )(sink, z3, z3, z3, z3, z3, z3, z3, zc3, zc3)


def _ctx_attn_body(sink_ref, q_ref, kc_ref, vc_ref, o_ref):
    for g in range(N_KV):
        _group_attention(sink_ref, q_ref, o_ref, g, ((kc_ref, vc_ref, None),))


def _ctx_attention(zc3, sink):
    bsz, ctx_len, _ = zc3.shape
    kcol, vcol = ZK // MXU_N, ZV // MXU_N
    return pl.pallas_call(
        _ctx_attn_body,
        grid=(bsz,),
        in_specs=[pl.BlockSpec(memory_space=pltpu.SMEM),
                  pl.BlockSpec((None, ctx_len, D), lambda b: (b, 0, 0)),
                  pl.BlockSpec((None, ctx_len, MXU_N), lambda b: (b, 0, kcol)),
                  pl.BlockSpec((None, ctx_len, MXU_N), lambda b: (b, 0, vcol))],
        out_specs=pl.BlockSpec((None, ctx_len, D), lambda b: (b, 0, 0)),
        out_shape=jax.ShapeDtypeStruct((bsz, ctx_len, D), BF16),
        compiler_params=_params(("parallel",)),
        name="ctx_attn",
    )(sink, zc3, zc3, zc3)


RG_CB = 512
RG_HALO = 16


def _rglru_body(x_ref, xp_ref, xn_ref, cw_ref, cb_ref, wg_ref, ba_ref, bi_ref, lam_ref, h0_ref,
                h_ref, hend_ref, a_scr, b_scr, xc_scr, carry_scr, *, reverse, n_t):
    t = pl.program_id(2)
    t_idx = (n_t - 1 - t) if reverse else t
    rows = x_ref.shape[0]
    n_grp = rows // 8
    x = x_ref[...].astype(F32)

    cw = cw_ref[...]

    def taps(xm2, xm1, x0, xp1):
        return cb_ref[...] + xm2 * cw[0:1] + xm1 * cw[1:2] + x0 * cw[2:3] + xp1 * cw[3:4]

    x_m1, x_m2, x_p1 = pltpu.roll(x, 1, 0), pltpu.roll(x, 2, 0), pltpu.roll(x, rows - 1, 0)
    xc_scr[...] = taps(x_m2, x_m1, x, x_p1)
    prev = xp_ref[...].astype(F32) * jnp.where(t_idx > 0, 1.0, 0.0)
    nxt = xn_ref[...].astype(F32) * jnp.where(t_idx < n_t - 1, 1.0, 0.0)
    p2, p1, n0 = prev[RG_HALO - 2:RG_HALO - 1], prev[RG_HALO - 1:RG_HALO], nxt[0:1]
    r8 = lax.broadcasted_iota(jnp.int32, (8, RG_CB), 0)
    head, tail = x[0:8], x[rows - 8:rows]
    h_m1 = jnp.where(r8 == 0, p1, pltpu.roll(head, 1, 0))
    h_m2 = jnp.where(r8 == 0, p2, jnp.where(r8 == 1, p1, pltpu.roll(head, 2, 0)))
    t_p1 = jnp.where(r8 == 7, n0, pltpu.roll(tail, 7, 0))
    xc_scr[0:8, :] = taps(h_m2, h_m1, head, x_p1[0:8])
    xc_scr[rows - 8:rows, :] = taps(x_m2[rows - 8:rows], x_m1[rows - 8:rows], tail, t_p1)

    lam = lam_ref[...]
    softplus_neg_lam = jnp.maximum(-lam, 0.0) + jnp.log1p(jnp.exp(-jnp.abs(lam)))
    sub = lax.broadcasted_iota(jnp.int32, (n_grp, 8, LANES), 1)
    for j in range(RG_CB // LANES):
        sl = slice(j * LANES, (j + 1) * LANES)
        xj = xc_scr[:, sl]
        gates = jnp.dot(xj.astype(BF16), wg_ref[j], preferred_element_type=F32)
        r = _sigmoid(gates[:, :LANES] + ba_ref[:, sl])
        ig = _sigmoid(gates[:, LANES:] + bi_ref[:, sl])
        log_a = -RG_C * r * softplus_neg_lam[:, sl]
        a = jnp.exp(log_a)
        th = jnp.tanh(log_a)
        b = jnp.sqrt(-2.0 * th / (1.0 - th)) * (ig * xj)
        a = a.reshape(n_grp, 8, LANES)
        b = b.reshape(n_grp, 8, LANES)
        for d in (1, 2, 4):
            shift = 8 - d if reverse else d
            ok = (sub < 8 - d) if reverse else (sub >= d)
            a_s, b_s = pltpu.roll(a, shift, 1), pltpu.roll(b, shift, 1)
            b = jnp.where(ok, a * b_s + b, b)
            a = jnp.where(ok, a * a_s, a)
        a_scr[:, sl] = a.reshape(rows, LANES)
        b_scr[:, sl] = b.reshape(rows, LANES)

    @pl.when(t == 0)
    def _():
        carry_scr[...] = jnp.broadcast_to(h0_ref[...], (8, RG_CB))

    def group(i, carry):
        g = (n_grp - 1 - i) if reverse else i
        off = pl.multiple_of(g * 8, 8)
        h = a_scr[pl.ds(off, 8), :] * carry + b_scr[pl.ds(off, 8), :]
        b_scr[pl.ds(off, 8), :] = h
        last = h[0:1] if reverse else h[7:8]
        return jnp.broadcast_to(last, (8, RG_CB))

    carry = lax.fori_loop(0, n_grp, group, carry_scr[...])
    carry_scr[...] = carry
    h_ref[...] = b_scr[...].astype(h_ref.dtype)
    hend_ref[...] = carry[0:1]


def _rglru(z3, conv_w, conv_b, wg, b_a, b_i, lam, h0, *, direction, tile):
    bsz, seq_len, _ = z3.shape
    n_t = seq_len // tile
    reverse = direction == 1
    n_cb = D // RG_CB
    xcol = ZXR // RG_CB
    hpt = tile // RG_HALO
    n_halo = seq_len // RG_HALO

    def tix(t):
        return (n_t - 1 - t) if reverse else t

    vec = lambda: pl.BlockSpec((None, 1, RG_CB), lambda b, c, t: (direction, 0, c))
    return pl.pallas_call(
        functools.partial(_rglru_body, reverse=reverse, n_t=n_t),
        grid=(bsz, n_cb, n_t),
        in_specs=[
            pl.BlockSpec((None, tile, RG_CB), lambda b, c, t: (b, tix(t), xcol + c)),
            pl.BlockSpec((None, RG_HALO, RG_CB),
                         lambda b, c, t: (b, jnp.maximum(tix(t) * hpt - 1, 0), xcol + c)),
            pl.BlockSpec((None, RG_HALO, RG_CB),
                         lambda b, c, t: (b, jnp.minimum((tix(t) + 1) * hpt, n_halo - 1), xcol + c)),
            pl.BlockSpec((4, RG_CB), lambda b, c, t: (0, c)),
            pl.BlockSpec((1, RG_CB), lambda b, c, t: (0, c)),
            pl.BlockSpec((None, RG_CB // LANES, LANES, 2 * LANES), lambda b, c, t: (direction, c, 0, 0)),
            vec(), vec(), vec(),
            pl.BlockSpec((None, 1, RG_CB), lambda b, c, t: (b, 0, c)),
        ],
        out_specs=[pl.BlockSpec((None, tile, RG_CB), lambda b, c, t: (b, tix(t), c)),
                   pl.BlockSpec((None, 1, RG_CB), lambda b, c, t: (b, 0, c))],
        out_shape=[jax.ShapeDtypeStruct((bsz, seq_len, D), BF16),
                   jax.ShapeDtypeStruct((bsz, 1, D), F32)],
        scratch_shapes=[pltpu.VMEM((tile, RG_CB), F32), pltpu.VMEM((tile, RG_CB), F32),
                        pltpu.VMEM((tile, RG_CB), F32), pltpu.VMEM((8, RG_CB), F32)],
        compiler_params=_params(("parallel", "parallel", "arbitrary")),
        name="rglru_bwd" if reverse else "rglru_fwd",
    )(z3, z3, z3, conv_w, conv_b.reshape(1, D), wg, b_a.reshape(2, 1, D), b_i.reshape(2, 1, D),
      lam.reshape(2, 1, D), h0)


def _gate_weights(w_a, w_i):
    def pair(w):
        w = w.reshape(2, 8, 2, 64, 64)
        z = jnp.zeros_like(w[:, :, 0])
        top = jnp.concatenate([w[:, :, 0], z], axis=-1)
        bot = jnp.concatenate([z, w[:, :, 1]], axis=-1)
        return jnp.concatenate([top, bot], axis=-2)
    return jnp.concatenate([pair(w_a), pair(w_i)], axis=-1).astype(BF16)


def _seqdft_body(c_ref, s_ref, xc_ref, xs_ref, o_ref, acc_ref):
    k = pl.program_id(2)

    @pl.when(k == 0)
    def _():
        acc_ref[...] = jnp.zeros_like(acc_ref)

    acc_ref[...] += (jnp.dot(c_ref[...], xc_ref[...], preferred_element_type=F32)
                     + jnp.dot(s_ref[...], xs_ref[...], preferred_element_type=F32))

    @pl.when(k == pl.num_programs(2) - 1)
    def _():
        o_ref[...] = acc_ref[...].astype(o_ref.dtype)


def _seq_dft(z3, cmat, nsmat, tile):
    bsz, seq_len, _ = z3.shape
    nt = seq_len // tile
    return pl.pallas_call(
        _seqdft_body,
        grid=(bsz, nt, nt),
        in_specs=[pl.BlockSpec((tile, tile), lambda b, i, k: (i, k)),
                  pl.BlockSpec((tile, tile), lambda b, i, k: (i, k)),
                  pl.BlockSpec((None, tile, D), lambda b, i, k: (b, k, ZXC // D)),
                  pl.BlockSpec((None, tile, D), lambda b, i, k: (b, k, ZXS // D))],
        out_specs=pl.BlockSpec((None, tile, D), lambda b, i, k: (b, i, 0)),
        out_shape=jax.ShapeDtypeStruct((bsz, seq_len, D), BF16),
        scratch_shapes=[pltpu.VMEM((tile, D), F32)],
        compiler_params=_params(("parallel", "parallel", "arbitrary")),
        name="seq_dft",
    )(cmat, nsmat, z3, z3)


def _dft_mats(n, scale):
    idx = jnp.arange(n, dtype=jnp.int32)
    ang = ((idx[:, None] * idx[None, :]) % n).astype(F32) * (2.0 * math.pi / n)
    return (jnp.cos(ang) * scale), (jnp.sin(ang) * scale)


FFT_L1 = 32
FFT_GRP = 16
FFT_K1B = 8


def _fft1_body(wa_ref, wb_ref, xc_ref, xs_ref, o_ref):
    rows = FFT_L1 * FFT_GRP
    xc = xc_ref[...].reshape(rows, D)
    xs = xs_ref[...].reshape(rows, D)
    a = (jnp.dot(wa_ref[...], xc, preferred_element_type=F32)
         + jnp.dot(wb_ref[...], xs, preferred_element_type=F32))
    a = a.astype(BF16)
    o_ref[:, :, :D] = a[:rows].reshape(FFT_L1, FFT_GRP, D)
    o_ref[:, :, D:] = a[rows:].reshape(FFT_L1, FFT_GRP, D)


def _fft2_body(m_ref, a_ref, o_ref):
    l2 = m_ref.shape[1]
    for j in range(FFT_K1B):
        o_ref[j] = (jnp.dot(m_ref[j, :, :l2], a_ref[j, :, :D], preferred_element_type=F32)
                    + jnp.dot(m_ref[j, :, l2:], a_ref[j, :, D:], preferred_element_type=F32)
                    ).astype(o_ref.dtype)


def _fft_tables(seq_len):
    l1, l2, g = FFT_L1, seq_len // FFT_L1, FFT_GRP
    i1 = jnp.arange(l1, dtype=jnp.int32)
    ang1 = ((i1[:, None] * i1[None, :]) % l1).astype(F32) * (2.0 * math.pi / l1)
    w1r, w1i = jnp.cos(ang1), -jnp.sin(ang1)
    eye = jnp.eye(g, dtype=F32)
    kron = lambda w: jnp.kron(w, eye)
    wa = jnp.concatenate([kron(w1r), kron(w1i)], axis=0).astype(BF16)
    wb = jnp.concatenate([kron(w1i), kron(-w1r)], axis=0).astype(BF16)
    i2 = jnp.arange(l2, dtype=jnp.int32)
    num = (i2[None, :, None] * i2[None, None, :] * l1 + i1[:, None, None] * i2[None, None, :]) % seq_len
    ang2 = num.astype(F32) * (2.0 * math.pi / seq_len)
    scale = seq_len ** -0.5
    m2 = jnp.concatenate([jnp.cos(ang2) * scale, jnp.sin(ang2) * scale], axis=2).astype(BF16)
    return wa, wb, m2


def _seq_fft(z, bsz, seq_len, tabs):
    wa, wb, m2 = tabs
    l1, l2, g = FFT_L1, seq_len // FFT_L1, FFT_GRP
    z4 = z.reshape(bsz, l1, l2, Z_COLS)
    rows = l1 * g
    a4 = pl.pallas_call(
        _fft1_body,
        grid=(bsz, l2 // g),
        in_specs=[_resident((2 * rows, rows)), _resident((2 * rows, rows)),
                  pl.BlockSpec((None, l1, g, D), lambda b, t: (b, 0, t, ZXC // D)),
                  pl.BlockSpec((None, l1, g, D), lambda b, t: (b, 0, t, ZXS // D))],
        out_specs=pl.BlockSpec((None, l1, g, 2 * D), lambda b, t: (b, 0, t, 0)),
        out_shape=jax.ShapeDtypeStruct((bsz, l1, l2, 2 * D), BF16),
        compiler_params=_params(("parallel", "parallel")),
        name="seq_fft_stage1",
    )(wa, wb, z4, z4)
    yp = pl.pallas_call(
        _fft2_body,
        grid=(l1 // FFT_K1B, bsz),
        in_specs=[pl.BlockSpec((FFT_K1B, l2, 2 * l2), lambda k, b: (k, 0, 0)),
                  pl.BlockSpec((None, FFT_K1B, l2, 2 * D), lambda k, b: (b, k, 0, 0))],
        out_specs=pl.BlockSpec((None, FFT_K1B, l2, D), lambda k, b: (b, k, 0, 0)),
        out_shape=jax.ShapeDtypeStruct((bsz, l1, l2, D), BF16),
        compiler_params=_params(("parallel", "parallel")),
        name="seq_fft_stage2",
    )(m2, a4)
    return yp.transpose(0, 2, 1, 3).reshape(bsz, seq_len, D)


def _gelu_tanh(x):
    return 0.5 * x * (1.0 + jnp.tanh(math.sqrt(2.0 / math.pi) * (x + 0.044715 * (x * x * x))))


def _merge_body(x_ref, mod_ref, attn_ref, gr_ref, hf_ref, hb_ref, four_ref,
                woa_ref, wor_ref, wof_ref, wm_ref, bm_ref, wout_ref, lng_ref, lnb_ref, wr_ref, br_ref,
                x1_ref, u2_ref, eid_ref, gate_ref, cnt_ref):
    mod = mod_ref[...]
    x = x_ref[...]
    u = (x * (1.0 + mod[1:2]) + mod[0:1]).astype(BF16)
    rg_in = (_gelu_tanh(gr_ref[...].astype(F32))
             * (hf_ref[...].astype(F32) + hb_ref[...].astype(F32))).astype(BF16)
    branches = ((attn_ref[...], woa_ref), (rg_in, wor_ref), (four_ref[...], wof_ref))
    merged = None
    for j, (inp, w_ref) in enumerate(branches):
        y = jnp.dot(inp, w_ref[...], preferred_element_type=F32)
        g = _sigmoid(jnp.dot(u, wm_ref[:, j * D:(j + 1) * D], preferred_element_type=F32)
                     + bm_ref[:, j * D:(j + 1) * D])
        merged = g * y if merged is None else merged + g * y
    y = jnp.dot(merged.astype(BF16), wout_ref[...], preferred_element_type=F32)
    x1 = _layer_norm(DEEPNORM_ALPHA * x + mod[2:3] * y, lng_ref[...], lnb_ref[...])
    x1_ref[...] = x1
    u2 = (x1 * (1.0 + mod[4:5]) + mod[3:4]).astype(BF16)
    u2_ref[...] = u2

    logits = jnp.dot(u2, wr_ref[...], preferred_element_type=F32) + br_ref[...]
    lane = lax.broadcasted_iota(jnp.int32, logits.shape, 1)
    vals, idxs = [], []
    for _ in range(TOP_K):
        m = jnp.max(logits, axis=-1, keepdims=True)
        idx = jnp.min(jnp.where(logits == m, lane, LANES), axis=-1, keepdims=True)
        vals.append(m)
        idxs.append(idx)
        logits = jnp.where(lane == idx, -3.0e38, logits)
    exps = [jnp.exp(v - vals[0]) for v in vals]
    den = exps[0] + exps[1] + exps[2] + exps[3]
    eid = jnp.zeros(lane.shape, jnp.int32)
    gate = jnp.zeros(lane.shape, F32)
    member = jnp.zeros(lane.shape, F32)
    for k in range(TOP_K):
        eid = jnp.where(lane == k, idxs[k], eid)
        gate = jnp.where(lane == k, exps[k] / den, gate)
        member = member + jnp.where(lane == idxs[k], 1.0, 0.0)
    eid_ref[...] = eid
    gate_ref[...] = gate
    cnt_ref[...] = jnp.sum(member, axis=0, keepdims=True)


def _merge(x2, mods, attn, z, hf, hb, four, p, seq_len, tm):
    n = x2.shape[0]
    tpb = seq_len // tm
    bm = mods.shape[0]
    mod_map = (lambda i: (i // tpb, 0, 0)) if bm > 1 else (lambda i: (0, 0, 0))
    act = lambda: pl.BlockSpec((tm, D), lambda i: (i, 0))
    return pl.pallas_call(
        _merge_body,
        grid=(n // tm,),
        in_specs=[act(), pl.BlockSpec((None, 6, D), mod_map), act(),
                  pl.BlockSpec((tm, D), lambda i: (i, ZGR // D)), act(), act(), act(),
                  _resident((D, D)), _resident((D, D)), _resident((D, D)),
                  _resident((D, 3 * D)), _resident((1, 3 * D)), _resident((D, D)),
                  _resident((1, D)), _resident((1, D)), _resident((D, LANES)), _resident((1, LANES))],
        out_specs=[act(), act(), pl.BlockSpec((tm, LANES), lambda i: (i, 0)),
                   pl.BlockSpec((tm, LANES), lambda i: (i, 0)),
                   pl.BlockSpec((None, 1, LANES), lambda i: (i, 0, 0))],
        out_shape=[jax.ShapeDtypeStruct((n, D), F32), jax.ShapeDtypeStruct((n, D), BF16),
                   jax.ShapeDtypeStruct((n, LANES), jnp.int32), jax.ShapeDtypeStruct((n, LANES), F32),
                   jax.ShapeDtypeStruct((n // tm, 1, LANES), F32)],
        compiler_params=_params(("parallel",)),
        name="merge_ln_router",
    )(x2, mods, attn, z, hf, hb, four, p["w_o_attn"], p["w_o_rg"], p["w_o_four"], p["w_merge"],
      p["b_merge"], p["w_out"], p["ln1_g"], p["ln1_b"], p["w_router"], p["b_router"])


RUN_ALIGN = 8


def _sorted_rows(td):
    return TOP_K * td + N_EXPERTS * RUN_ALIGN


def _chunk_bits(max_len):
    bits, b = [], RUN_ALIGN
    while b <= max_len:
        bits.append(b)
        b *= 2
    return tuple(reversed(bits))


def _chunked_copies(wait, src_ref, dst_ref, src_off, dst_off, length, bits, sem):
    for bit in bits:
        @pl.when((length & bit) != 0)
        def _():
            off = length & (-2 * bit)
            cp = pltpu.make_async_copy(src_ref.at[pl.ds(pl.multiple_of(src_off + off, RUN_ALIGN), bit)],
                                       dst_ref.at[pl.ds(pl.multiple_of(dst_off + off, RUN_ALIGN), bit)], sem)
            if wait:
                cp.wait()
            else:
                cp.start()


def _slot_positions(eid, toff, ltri):
    lane = lax.broadcasted_iota(jnp.int32, eid.shape, 1)
    hits = [eid[:, k:k + 1] == lane for k in range(TOP_K)]
    member = jnp.zeros(eid.shape, F32)
    for h in hits:
        member = member + jnp.where(h, 1.0, 0.0)
    rank = jnp.dot(ltri, member.astype(BF16), preferred_element_type=F32)
    base = toff + rank
    return [jnp.sum(jnp.where(h, base, 0.0), axis=-1, keepdims=True) for h in hits]


def _dispatch_body(rl_ref, to_ref, ro_ref, tlo_ref, tll_ref, nu_ref, tt_ref, u_ref, eid_ref, toff_ref, ltri_ref,
                   xs_hbm, pos_ref, srt, sems, *, td, tb):
    t = pl.program_id(0)
    n_t = pl.num_programs(0)
    slot = t % 2
    rows = srt.shape[1]
    run_bits = _chunk_bits(td)

    def run_copies(wait, tile, s):
        def one(e, c):
            i = tile * N_EXPERTS + e
            _chunked_copies(wait, srt.at[s], xs_hbm, to_ref[i], ro_ref[i], rl_ref[i], run_bits, sems.at[s])
            return c
        lax.fori_loop(0, N_EXPERTS, one, 0)

    @pl.when(t == 0)
    def _():
        srt[1, :tb, :] = jnp.zeros((tb, D), F32)
        tail_bits = _chunk_bits(tb)
        for wait in (False, True):
            def one(e, c):
                _chunked_copies(wait, srt.at[1], xs_hbm, 0, tlo_ref[e], tll_ref[e], tail_bits, sems.at[1])
                return c
            lax.fori_loop(0, N_EXPERTS, one, 0)
        n_blocks = xs_hbm.shape[0] // tb
        for wait in (False, True):
            def blk(j, c):
                cp = pltpu.make_async_copy(srt.at[1, pl.ds(0, tb)],
                                           xs_hbm.at[pl.ds(pl.multiple_of(j * tb, RUN_ALIGN), tb)], sems.at[1])
                if wait:
                    cp.wait()
                else:
                    cp.start()
                return c
            lax.fori_loop(nu_ref[0], n_blocks, blk, 0)

    pos = _slot_positions(eid_ref[...], toff_ref[...], ltri_ref[...])
    lane = lax.broadcasted_iota(jnp.int32, (td, LANES), 1)
    pos_out = jnp.zeros((td, LANES), jnp.int32)
    for k in range(TOP_K):
        pos_out = jnp.where(lane == k, pos[k].astype(jnp.int32), pos_out)
    pos_ref[...] = pos_out

    col = lax.broadcasted_iota(jnp.int32, (td, rows), 1).astype(F32)
    hit = col == pos[0]
    for k in range(1, TOP_K):
        hit = hit | (col == pos[k])
    onehot_t = jnp.where(hit, 1.0, 0.0).astype(BF16)
    srt[slot] = lax.dot_general(onehot_t, u_ref[...], (((0,), (0,)), ((), ())),
                                preferred_element_type=F32)
    run_copies(False, t, slot)

    def wait_runs(tile, s):
        _chunked_copies(True, srt.at[s], xs_hbm, 0, 0, tt_ref[tile], _chunk_bits(rows), sems.at[s])

    @pl.when(t >= 1)
    def _():
        wait_runs(t - 1, 1 - slot)

    @pl.when(t == n_t - 1)
    def _():
        wait_runs(t, slot)


def _dispatch(u2, eid128, rt, td, tb):
    n_tok = u2.shape[0]
    n_t = n_tok // td
    rows = _sorted_rows(td)
    ltri = (jnp.arange(td)[:, None] > jnp.arange(td)[None, :]).astype(BF16)
    grid_spec = pltpu.PrefetchScalarGridSpec(
        num_scalar_prefetch=7,
        grid=(n_t,),
        in_specs=[pl.BlockSpec((td, D), lambda t, *_: (t, 0)),
                  pl.BlockSpec((td, LANES), lambda t, *_: (t, 0)),
                  pl.BlockSpec((None, 1, LANES), lambda t, *_: (t, 0, 0)),
                  pl.BlockSpec((td, td), lambda t, *_: (0, 0), pipeline_mode=pl.Buffered(1))],
        out_specs=[pl.BlockSpec(memory_space=pl.ANY),
                   pl.BlockSpec((td, LANES), lambda t, *_: (t, 0))],
        scratch_shapes=[pltpu.VMEM((2, rows, D), F32), pltpu.SemaphoreType.DMA((2,))],
    )
    return pl.pallas_call(
        functools.partial(_dispatch_body, td=td, tb=tb),
        grid_spec=grid_spec,
        out_shape=[jax.ShapeDtypeStruct((rt["n_slots"], D), F32),
                   jax.ShapeDtypeStruct((n_tok, LANES), jnp.int32)],
        compiler_params=_params(("arbitrary",)),
        name="moe_dispatch",
    )(rt["run_len"], rt["tile_off"], rt["run_off"], rt["tail_off"], rt["tail_len"], rt["n_used"], rt["tile_tot"],
      u2, eid128, rt["toff_f"], ltri)


def _moe_body(be_ref, nused_ref, xs_ref, wgu_ref, bgu_ref, wdn_ref, bdn_ref, ys_ref):
    i = pl.program_id(0)

    @pl.when(i < nused_ref[0])
    def _():
        x = xs_ref[...].astype(BF16)
        h = jnp.dot(x, wgu_ref[...], preferred_element_type=F32) + bgu_ref[...]
        half = h.shape[1] // 2
        x_glu = jnp.minimum(h[:, :half], SWIGLU_LIMIT)
        x_lin = jnp.clip(h[:, half:], -SWIGLU_LIMIT, SWIGLU_LIMIT)
        act = (x_glu * _sigmoid(SWIGLU_ALPHA * x_glu) * (x_lin + 1.0)).astype(BF16)
        ys_ref[...] = jnp.dot(act, wdn_ref[...], preferred_element_type=F32) + bdn_ref[...]

    @pl.when(i >= nused_ref[0])
    def _():
        ys_ref[...] = jnp.zeros_like(ys_ref)


def _moe(xs, block_e, n_used, w_gu, b_gu, w_dn, b_dn, tb):
    n_blocks = xs.shape[0] // tb
    d_ff2 = w_gu.shape[2]
    live = lambda i, nu: jnp.minimum(i, nu[0] - 1)
    grid_spec = pltpu.PrefetchScalarGridSpec(
        num_scalar_prefetch=2,
        grid=(n_blocks,),
        in_specs=[pl.BlockSpec((tb, D), lambda i, be, nu: (live(i, nu), 0)),
                  pl.BlockSpec((None, D, d_ff2), lambda i, be, nu: (be[live(i, nu)], 0, 0)),
                  pl.BlockSpec((None, 1, d_ff2), lambda i, be, nu: (be[live(i, nu)], 0, 0)),
                  pl.BlockSpec((None, d_ff2 // 2, D), lambda i, be, nu: (be[live(i, nu)], 0, 0)),
                  pl.BlockSpec((None, 1, D), lambda i, be, nu: (be[live(i, nu)], 0, 0))],
        out_specs=pl.BlockSpec((tb, D), lambda i, be, nu: (i, 0)),
    )
    return pl.pallas_call(
        _moe_body,
        grid_spec=grid_spec,
        out_shape=jax.ShapeDtypeStruct(xs.shape, F32),
        compiler_params=_params(("arbitrary",)),
        name="moe_experts",
    )(block_e, n_used, xs, w_gu, b_gu, w_dn, b_dn)


def _route(cnt128, n_tok, td, tb):
    n_t = n_tok // td
    cnt = cnt128[:, 0, :N_EXPERTS].astype(jnp.int32)
    run = (cnt + RUN_ALIGN - 1) // RUN_ALIGN * RUN_ALIGN
    tile_off = jnp.cumsum(run, axis=1) - run
    total = run.sum(axis=0)
    padded = (total + tb - 1) // tb * tb
    pend = jnp.cumsum(padded)
    pstart = pend - padded
    run_off = pstart[None, :] + jnp.cumsum(run, axis=0) - run
    worst = TOP_K * n_tok + n_t * N_EXPERTS * RUN_ALIGN + N_EXPERTS * tb
    n_slots = (worst + tb - 1) // tb * tb
    first_slot = jnp.arange(n_slots // tb, dtype=jnp.int32) * tb
    block_e = jnp.minimum(jnp.sum((pend[None, :] <= first_slot[:, None]).astype(jnp.int32), axis=1),
                          N_EXPERTS - 1)
    toff_f = jnp.zeros((n_t, 1, LANES), F32).at[:, 0, :N_EXPERTS].set(tile_off.astype(F32))
    i32 = lambda a: a.astype(jnp.int32).reshape(-1)
    return dict(run_len=i32(run), tile_off=i32(tile_off), run_off=i32(run_off),
                tail_off=i32(pstart + total), tail_len=i32(padded - total), toff_f=toff_f,
                tile_tot=i32(run.sum(axis=1)),
                block_e=i32(block_e), n_used=i32(pend[-1] // tb), n_slots=n_slots)


def _combine_body(rl_ref, to_ref, ro_ref, tt_ref, ys_hbm, x1_ref, mod_ref, gate_ref, pos_ref, lng_ref, lnb_ref,
                  o_ref, srt, sems, *, td):
    t = pl.program_id(0)
    n_t = pl.num_programs(0)
    slot = t % 2
    rows = srt.shape[1]
    run_bits = _chunk_bits(td)

    def run_copies(wait, tile, s):
        def one(e, c):
            i = tile * N_EXPERTS + e
            _chunked_copies(wait, ys_hbm, srt.at[s], ro_ref[i], to_ref[i], rl_ref[i], run_bits, sems.at[s])
            return c
        lax.fori_loop(0, N_EXPERTS, one, 0)

    @pl.when(t == 0)
    def _():
        srt[...] = jnp.zeros_like(srt)
        run_copies(False, 0, 0)

    @pl.when(t + 1 < n_t)
    def _():
        run_copies(False, t + 1, 1 - slot)

    _chunked_copies(True, ys_hbm, srt.at[slot], 0, 0, tt_ref[t], _chunk_bits(rows), sems.at[slot])

    gate = gate_ref[...]
    pos = pos_ref[...]
    col = lax.broadcasted_iota(jnp.int32, (td, rows), 1)
    weights = jnp.zeros((td, rows), F32)
    for k in range(TOP_K):
        weights = jnp.where(col == pos[:, k:k + 1], gate[:, k:k + 1], weights)
    f = jnp.dot(weights.astype(BF16), srt[slot].astype(BF16), preferred_element_type=F32)
    mod = mod_ref[...]
    o_ref[...] = _layer_norm(DEEPNORM_ALPHA * x1_ref[...] + mod[5:6] * f, lng_ref[...], lnb_ref[...])


def _combine(x1, mods, gate128, pos128, ys, rt, ln_g, ln_b, seq_len, td):
    n = x1.shape[0]
    tpb = seq_len // td
    bm = mods.shape[0]
    mod_map = (lambda i, *_: (i // tpb, 0, 0)) if bm > 1 else (lambda i, *_: (0, 0, 0))
    grid_spec = pltpu.PrefetchScalarGridSpec(
        num_scalar_prefetch=4,
        grid=(n // td,),
        in_specs=[pl.BlockSpec(memory_space=pl.ANY),
                  pl.BlockSpec((td, D), lambda i, *_: (i, 0)),
                  pl.BlockSpec((None, 6, D), mod_map),
                  pl.BlockSpec((td, LANES), lambda i, *_: (i, 0)),
                  pl.BlockSpec((td, LANES), lambda i, *_: (i, 0)),
                  pl.BlockSpec((1, D), lambda i, *_: (0, 0)),
                  pl.BlockSpec((1, D), lambda i, *_: (0, 0))],
        out_specs=pl.BlockSpec((td, D), lambda i, *_: (i, 0)),
        scratch_shapes=[pltpu.VMEM((2, _sorted_rows(td), D), F32), pltpu.SemaphoreType.DMA((2,))],
    )
    return pl.pallas_call(
        functools.partial(_combine_body, td=td),
        grid_spec=grid_spec,
        out_shape=jax.ShapeDtypeStruct((n, D), F32),
        compiler_params=_params(("arbitrary",)),
        name="combine_ln",
    )(rt["run_len"], rt["tile_off"], rt["run_off"], rt["tile_tot"], ys, x1, mods, gate128, pos128,
      ln_g, ln_b)


def _rope_tables(seq_len):
    pos = jnp.arange(seq_len, dtype=jnp.int32)
    row = (pos // GRID_W).astype(F32)
    col = (pos % GRID_W).astype(F32)
    n_freq = HEAD_DIM // 4
    freqs = ROPE_BASE ** (-jnp.arange(n_freq, dtype=F32) / n_freq)
    ar, ac = row[:, None] * freqs, col[:, None] * freqs
    zero = jnp.zeros_like(ar)
    c = jnp.concatenate([jnp.cos(ar), jnp.cos(ar), jnp.cos(ac), jnp.cos(ac)], axis=1)
    sa = jnp.concatenate([-jnp.sin(ar), zero, -jnp.sin(ac), zero], axis=1)
    sb = jnp.concatenate([zero, jnp.sin(ar), zero, jnp.sin(ac)], axis=1)
    rep = MXU_N // HEAD_DIM
    return tuple(jnp.tile(t, (1, rep)) for t in (c, sa, sb))


def _layer_params(l, w):
    w_router = jnp.zeros((D, LANES), BF16).at[:, :N_EXPERTS].set(w["w_router"][l].astype(BF16))
    b_router = jnp.full((1, LANES), NEG_INF, F32).at[0, :N_EXPERTS].set(w["b_router"][l])
    return dict(
        w_in=w["w_in"][l].astype(BF16),
        w_o_attn=w["w_o_attn"][l].astype(BF16), w_o_rg=w["w_o_rg"][l].astype(BF16),
        w_o_four=w["w_o_four"][l].astype(BF16), w_merge=w["w_merge"][l].astype(BF16),
        b_merge=w["b_merge"][l].reshape(1, 3 * D), w_out=w["w_out"][l].astype(BF16),
        ln1_g=w["ln1_g"][l].reshape(1, D), ln1_b=w["ln1_b"][l].reshape(1, D),
        ln2_g=w["ln2_g"][l].reshape(1, D), ln2_b=w["ln2_b"][l].reshape(1, D),
        w_router=w_router, b_router=b_router,
        w_gu=w["w_gate_up"][l].astype(BF16), b_gu=w["b_gate_up"][l].reshape(N_EXPERTS, 1, -1),
        w_dn=w["w_down"][l].astype(BF16), b_dn=w["b_down"][l].reshape(N_EXPERTS, 1, D),
        wg=_gate_weights(w["rg_w_a"][l], w["rg_w_i"][l]),
        conv_w=w["conv_w"][l], conv_b=w["conv_b"][l], b_a=w["rg_b_a"][l], b_i=w["rg_b_i"][l],
        lam=w["rg_lambda"][l], sink=w["attn_sink"][l],
    )


def _row_tile(seq_len, want):
    return min(want, seq_len)


def _ffn(x1, u2, eid128, gate128, cnt128, mods, p, seq_len, tb, td):
    rt = _route(cnt128, x1.shape[0], td, tb)
    xs, pos128 = _dispatch(u2, eid128, rt, td, tb)
    ys = _moe(xs, rt["block_e"], rt["n_used"], p["w_gu"], p["b_gu"], p["w_dn"], p["b_dn"], tb)
    return _combine(x1, mods, gate128, pos128, ys, rt, p["ln2_g"], p["ln2_b"], seq_len, td)


def kernel(x, c, ctx, c_ctx, w_mod, b_mod, w_in, attn_sink, w_o_attn, conv_w, conv_b, rg_w_a, rg_b_a, rg_w_i, rg_b_i, rg_lambda, w_o_rg, w_o_four, w_merge, b_merge, w_out, ln1_g, ln1_b, w_router, b_router, w_gate_up, b_gate_up, w_down, b_down, ln2_g, ln2_b):
    w = dict(w_mod=w_mod, b_mod=b_mod, w_in=w_in, attn_sink=attn_sink, w_o_attn=w_o_attn, conv_w=conv_w,
             conv_b=conv_b, rg_w_a=rg_w_a, rg_b_a=rg_b_a, rg_w_i=rg_w_i, rg_b_i=rg_b_i, rg_lambda=rg_lambda,
             w_o_rg=w_o_rg, w_o_four=w_o_four, w_merge=w_merge, b_merge=b_merge, w_out=w_out, ln1_g=ln1_g,
             ln1_b=ln1_b, w_router=w_router, b_router=b_router, w_gate_up=w_gate_up, b_gate_up=b_gate_up,
             w_down=w_down, b_down=b_down, ln2_g=ln2_g, ln2_b=ln2_b)
    bsz, seq_len, _ = x.shape
    ctx_len = ctx.shape[1]
    n_lat, n_ctx = bsz * seq_len, bsz * ctx_len
    tm_lat, tm_ctx = _row_tile(seq_len, 512), _row_tile(ctx_len, 256)
    dft_lat, dft_ctx = _row_tile(seq_len, 1024), _row_tile(ctx_len, 256)
    rg_lat, rg_ctx = _row_tile(seq_len, 256), _row_tile(ctx_len, 256)
    tb_lat, tb_ctx = 512, 128

    rope = _rope_tables(seq_len)
    cw, sw = _dft_mats(F_GW, F_GW ** -0.5)
    csw = jnp.concatenate([cw, sw], axis=1).astype(BF16)
    use_fft = seq_len % (FFT_L1 * FFT_GRP) == 0
    if use_fft:
        fft_tabs = _fft_tables(seq_len)
    else:
        cl, sl = _dft_mats(seq_len, seq_len ** -0.5)
        cl, nsl = cl.astype(BF16), (-sl).astype(BF16)
    cc, sc = _dft_mats(ctx_len, ctx_len ** -0.5)
    cc, nsc = cc.astype(BF16), (-sc).astype(BF16)

    cond = jnp.zeros((16, D), F32).at[:bsz].set(c).at[bsz].set(c_ctx)
    x2 = x.reshape(n_lat, D)
    ctx2 = ctx.reshape(n_ctx, D)
    zero_h = jnp.zeros((bsz, 1, D), F32)

    for l in range(DEPTH):
        last = l == DEPTH - 1
        p = _layer_params(l, w)
        m = _adaln(cond, w_mod[l], b_mod[l]).reshape(16, 6, D)
        mods_lat, mods_ctx = m[:bsz], m[bsz:bsz + 1]

        zc = _proj(ctx2, mods_ctx, p["w_in"], csw, None, ctx_len, tm_ctx)
        z = _proj(x2, mods_lat, p["w_in"], csw, rope, seq_len, tm_lat)
        zc3 = zc.reshape(bsz, ctx_len, Z_COLS)
        z3 = z.reshape(bsz, seq_len, Z_COLS)

        rg = lambda zz, h0, d, tile: _rglru(zz, p["conv_w"], p["conv_b"], p["wg"], p["b_a"], p["b_i"],
                                            p["lam"], h0, direction=d, tile=tile)
        hcf, endf = rg(zc3, zero_h, 0, rg_ctx)
        hcb, endb = rg(zc3, zero_h, 1, rg_ctx)
        hf, _ = rg(z3, endf, 0, rg_lat)
        hb, _ = rg(z3, endb, 1, rg_lat)

        attn = _attention(z3, zc3, p["sink"])
        four = _seq_fft(z, bsz, seq_len, fft_tabs) if use_fft else _seq_dft(z3, cl, nsl, dft_lat)
        x1, u2, eid, gate, cnt = _merge(x2, mods_lat, attn.reshape(n_lat, D), z, hf.reshape(n_lat, D),
                                   hb.reshape(n_lat, D), four.reshape(n_lat, D), p, seq_len, tm_lat)
        x2 = _ffn(x1, u2, eid, gate, cnt, mods_lat, p, seq_len, tb_lat, tm_lat)

        if not last:
            attn_c = _ctx_attention(zc3, p["sink"])
            four_c = _seq_dft(zc3, cc, nsc, dft_ctx)
            c1, uc2, eid_c, gate_c, cnt_c = _merge(ctx2, mods_ctx, attn_c.reshape(n_ctx, D), zc,
                                            hcf.reshape(n_ctx, D), hcb.reshape(n_ctx, D),
                                            four_c.reshape(n_ctx, D), p, ctx_len, tm_ctx)
            ctx2 = _ffn(c1, uc2, eid_c, gate_c, cnt_c, mods_ctx, p, ctx_len, tb_ctx, tm_ctx)

    return x2.reshape(bsz, seq_len, D)
```

```python
import functools
import math

import jax
import jax.numpy as jnp
from jax import lax
from jax.experimental import pallas as pl
from jax.experimental.pallas import tpu as pltpu

F32 = jnp.float32
BF16 = jnp.bfloat16

D = 1024
HEAD_DIM = 64
N_HEADS = 16
N_KV = 4
Q_PER_KV = 4
WINDOW = 128
QBLK = 128
assert WINDOW == QBLK
GRID_W = 64
ROPE_BASE = 10000.0
RG_C = 8.0
CONV_LEFT = 2
F_GROUPS = 4
F_GW = 256
N_EXPERTS = 32
TOP_K = 4
SWIGLU_LIMIT = 7.0
SWIGLU_ALPHA = 1.702
LN_EPS = 1e-5
DEPTH = 2
DEEPNORM_ALPHA = (2 * DEPTH) ** 0.25
NEG_INF = -1e30

Q_OFF, K_OFF, V_OFF, XR_OFF, GR_OFF, XF_OFF, IN_COLS = 0, 1024, 1280, 1536, 2560, 3584, 4608
ZQ, ZXR, ZGR, ZXC, ZXS, ZK, ZV, Z_COLS = 0, 1024, 2048, 3072, 4096, 5120, 5376, 5632

VMEM_LIMIT_V7X = 56 * 1024 * 1024
LANES = 128
MXU_N = 256


def _params(sem, vmem=VMEM_LIMIT_V7X):
    return pltpu.CompilerParams(dimension_semantics=sem, vmem_limit_bytes=vmem)


def _resident(shape):
    nd = len(shape)
    return pl.BlockSpec(shape, lambda *_: (0,) * nd, pipeline_mode=pl.Buffered(1))


def _sigmoid(x):
    return 1.0 / (1.0 + jnp.exp(-x))


def _layer_norm(r, g, b):
    mu = jnp.mean(r, axis=-1, keepdims=True)
    rc = r - mu
    var = jnp.mean(rc * rc, axis=-1, keepdims=True)
    return rc * lax.rsqrt(var + LN_EPS) * g + b


def _adaln_body(c_ref, w_ref, b_ref, o_ref):
    c = c_ref[...]
    s = (c * _sigmoid(c)).astype(BF16)
    o_ref[...] = jnp.dot(s, w_ref[...].astype(BF16), preferred_element_type=F32) + b_ref[...]


def _adaln(cond, w_mod, b_mod):
    rows, n = cond.shape[0], w_mod.shape[1]
    tn = 1024
    return pl.pallas_call(
        _adaln_body,
        grid=(n // tn,),
        in_specs=[pl.BlockSpec((rows, D), lambda j: (0, 0)),
                  pl.BlockSpec((D, tn), lambda j: (0, j)),
                  pl.BlockSpec((1, tn), lambda j: (0, j))],
        out_specs=pl.BlockSpec((rows, tn), lambda j: (0, j)),
        out_shape=jax.ShapeDtypeStruct((rows, n), F32),
        compiler_params=_params(("arbitrary",)),
        name="adaln",
    )(cond, w_mod, b_mod.reshape(1, n))


def _proj_dst(j):
    if j < 4:
        return ZQ + j * MXU_N
    if j == 4:
        return ZK
    if j == 5:
        return ZV
    if j < 10:
        return ZXR + (j - 6) * MXU_N
    if j < 14:
        return ZGR + (j - 10) * MXU_N
    return None


def _proj_body(x_ref, mod_ref, w_ref, csw_ref, *rest, rope):
    if rope:
        c_ref, sa_ref, sb_ref, o_ref = rest
    else:
        (o_ref,) = rest
    mod = mod_ref[...]
    u = (x_ref[...] * (1.0 + mod[1:2]) + mod[0:1]).astype(BF16)
    for j in range(IN_COLS // MXU_N):
        acc = jnp.dot(u, w_ref[:, j * MXU_N:(j + 1) * MXU_N], preferred_element_type=F32)
        if rope and j < 5:
            acc = (acc * c_ref[...] + pltpu.roll(acc, MXU_N - 16, 1) * sa_ref[...]
                   + pltpu.roll(acc, 16, 1) * sb_ref[...])
        dst = _proj_dst(j)
        if dst is not None:
            o_ref[:, dst:dst + MXU_N] = acc.astype(BF16)
        else:
            g = j - 14
            t = jnp.dot(acc.astype(BF16), csw_ref[...], preferred_element_type=F32)
            o_ref[:, ZXC + g * F_GW:ZXC + (g + 1) * F_GW] = t[:, :F_GW].astype(BF16)
            o_ref[:, ZXS + g * F_GW:ZXS + (g + 1) * F_GW] = t[:, F_GW:].astype(BF16)


def _proj(x2, mods, w_in, csw, rope_tabs, seq_len, tm):
    n = x2.shape[0]
    tpb = seq_len // tm
    bm = mods.shape[0]
    mod_map = (lambda i: (i // tpb, 0, 0)) if bm > 1 else (lambda i: (0, 0, 0))
    in_specs = [pl.BlockSpec((tm, D), lambda i: (i, 0)),
                pl.BlockSpec((None, 6, D), mod_map),
                _resident((D, IN_COLS)),
                _resident((F_GW, 2 * F_GW))]
    args = [x2, mods, w_in, csw]
    if rope_tabs is not None:
        in_specs += [pl.BlockSpec((tm, MXU_N), lambda i: (i % tpb, 0))] * 3
        args += list(rope_tabs)
    return pl.pallas_call(
        functools.partial(_proj_body, rope=rope_tabs is not None),
        grid=(n // tm,),
        in_specs=in_specs,
        out_specs=pl.BlockSpec((tm, Z_COLS), lambda i: (i, 0)),
        out_shape=jax.ShapeDtypeStruct((n, Z_COLS), BF16),
        compiler_params=_params(("parallel",)),
        name="proj_rope" if rope_tabs is not None else "proj_ctx",
    )(*args)


def _group_attention(sink_ref, q_ref, o_ref, g, parts):
    rows = q_ref.shape[0]
    key_low = g % 2 == 0
    lane = lax.broadcasted_iota(jnp.int32, (1, LANES), 1)
    keep = (lane < HEAD_DIM) if key_low else (lane >= HEAD_DIM)
    kv_lanes = slice((g // 2) * LANES, (g // 2 + 1) * LANES)

    blocks = []
    for j in range(2):
        q32 = q_ref[:, (2 * g + j) * LANES:(2 * g + j + 1) * LANES].astype(F32) * (HEAD_DIM ** -0.5)
        same, rot = q32.astype(BF16), pltpu.roll(q32, HEAD_DIM, 1).astype(BF16)
        blocks += [same, rot] if key_low else [rot, same]
    qs = jnp.concatenate(blocks, axis=0)

    row = lax.broadcasted_iota(jnp.int32, (Q_PER_KV * rows, 1), 0)
    sink = jnp.full((Q_PER_KV * rows, 1), sink_ref[g * Q_PER_KV + Q_PER_KV - 1], F32)
    for r in range(Q_PER_KV - 2, -1, -1):
        sink = jnp.where(row < (r + 1) * rows, sink_ref[g * Q_PER_KV + r], sink)

    def lane_blocks(a):
        return [a[:, c * LANES:(c + 1) * LANES] for c in range(a.shape[1] // LANES)]

    scores = []
    m_fold = None
    for k_ref, _, bias in parts:
        k2 = jnp.where(keep, k_ref[:, kv_lanes], jnp.zeros((), BF16))
        s = lax.dot_general(qs, k2, (((1,), (1,)), ((), ())), preferred_element_type=F32)
        if bias is not None:
            s = s + bias
        scores.append(s)
        for blk in lane_blocks(s):
            m_fold = blk if m_fold is None else jnp.maximum(m_fold, blk)
    m = jnp.maximum(sink, jnp.max(m_fold, axis=-1, keepdims=True))
    p_fold = None
    out = None
    for s, (_, v_ref, _) in zip(scores, parts):
        p = jnp.exp(s - m)
        for blk in lane_blocks(p):
            p_fold = blk if p_fold is None else p_fold + blk
        v2 = jnp.where(keep, v_ref[:, kv_lanes], jnp.zeros((), BF16))
        o = jnp.dot(p.astype(BF16), v2, preferred_element_type=F32)
        out = o if out is None else out + o
    den = jnp.exp(sink - m) + jnp.sum(p_fold, axis=-1, keepdims=True)
    out = out / den
    for j in range(2):
        first, second = out[2 * j * rows:(2 * j + 1) * rows], out[(2 * j + 1) * rows:(2 * j + 2) * rows]
        if key_low:
            both = first + pltpu.roll(second, HEAD_DIM, 1)
        else:
            both = pltpu.roll(first, HEAD_DIM, 1) + second
        o_ref[:, (2 * g + j) * LANES:(2 * g + j + 1) * LANES] = both.astype(o_ref.dtype)


def _attn_body(sink_ref, q_ref, kl_ref, km_ref, kr_ref, vl_ref, vm_ref, vr_ref, kc_ref, vc_ref,
               o_ref, *, n_blk):
    n = pl.program_id(1)
    i = lax.broadcasted_iota(jnp.int32, (Q_PER_KV * QBLK, QBLK), 0) % QBLK
    j = lax.broadcasted_iota(jnp.int32, (Q_PER_KV * QBLK, QBLK), 1)
    bias_prev = jnp.where((j >= i) & (n > 0), 0.0, NEG_INF).astype(F32)
    bias_next = jnp.where((j <= i) & (n < n_blk - 1), 0.0, NEG_INF).astype(F32)
    parts = ((kl_ref, vl_ref, bias_prev), (km_ref, vm_ref, None), (kr_ref, vr_ref, bias_next),
             (kc_ref, vc_ref, None))
    for g in range(N_KV):
        _group_attention(sink_ref, q_ref, o_ref, g, parts)


def _attention(z3, zc3, sink):
    bsz, seq_len, _ = z3.shape
    ctx_len = zc3.shape[1]
    nb = seq_len // QBLK
    kcol, vcol = ZK // MXU_N, ZV // MXU_N

    def blk(col, off):
        return pl.BlockSpec((None, QBLK, MXU_N),
                            lambda b, n: (b, jnp.clip(n + off, 0, nb - 1), col))

    return pl.pallas_call(
        functools.partial(_attn_body, n_blk=nb),
        grid=(bsz, nb),
        in_specs=[pl.BlockSpec(memory_space=pltpu.SMEM),
                  pl.BlockSpec((None, QBLK, D), lambda b, n: (b, n, 0)),
                  blk(kcol, -1), blk(kcol, 0), blk(kcol, 1),
                  blk(vcol, -1), blk(vcol, 0), blk(vcol, 1),
                  pl.BlockSpec((None, ctx_len, MXU_N), lambda b, n: (b, 0, kcol)),
                  pl.BlockSpec((None, ctx_len, MXU_N), lambda b, n: (b, 0, vcol))],
        out_specs=pl.BlockSpec((None, QBLK, D), lambda b, n: (b, n, 0)),
        out_shape=jax.ShapeDtypeStruct((bsz, seq_len, D), BF16),
        compiler_params=_params(("parallel", "parallel")),
        name="window_attn",
    )(sink, z3, z3, z3, z3, z3, z3, z3, zc3, zc3)


def _ctx_attn_body(sink_ref, q_ref, kc_ref, vc_ref, o_ref):
    for g in range(N_KV):
        _group_attention(sink_ref, q_ref, o_ref, g, ((kc_ref, vc_ref, None),))


def _ctx_attention(zc3, sink):
    bsz, ctx_len, _ = zc3.shape
    kcol, vcol = ZK // MXU_N, ZV // MXU_N
    return pl.pallas_call(
        _ctx_attn_body,
        grid=(bsz,),
        in_specs=[pl.BlockSpec(memory_space=pltpu.SMEM),
                  pl.BlockSpec((None, ctx_len, D), lambda b: (b, 0, 0)),
                  pl.BlockSpec((None, ctx_len, MXU_N), lambda b: (b, 0, kcol)),
                  pl.BlockSpec((None, ctx_len, MXU_N), lambda b: (b, 0, vcol))],
        out_specs=pl.BlockSpec((None, ctx_len, D), lambda b: (b, 0, 0)),
        out_shape=jax.ShapeDtypeStruct((bsz, ctx_len, D), BF16),
        compiler_params=_params(("parallel",)),
        name="ctx_attn",
    )(sink, zc3, zc3, zc3)


RG_CB = 512
RG_HALO = 16


def _rglru_body(x_ref, xp_ref, xn_ref, cw_ref, cb_ref, wg_ref, ba_ref, bi_ref, lam_ref, h0_ref,
                h_ref, hend_ref, a_scr, b_scr, xc_scr, carry_scr, *, reverse, n_t):
    t = pl.program_id(2)
    t_idx = (n_t - 1 - t) if reverse else t
    rows = x_ref.shape[0]
    n_grp = rows // 8
    x = x_ref[...].astype(F32)

    cw = cw_ref[...]

    def taps(xm2, xm1, x0, xp1):
        return cb_ref[...] + xm2 * cw[0:1] + xm1 * cw[1:2] + x0 * cw[2:3] + xp1 * cw[3:4]

    x_m1, x_m2, x_p1 = pltpu.roll(x, 1, 0), pltpu.roll(x, 2, 0), pltpu.roll(x, rows - 1, 0)
    xc_scr[...] = taps(x_m2, x_m1, x, x_p1)
    prev = xp_ref[...].astype(F32) * jnp.where(t_idx > 0, 1.0, 0.0)
    nxt = xn_ref[...].astype(F32) * jnp.where(t_idx < n_t - 1, 1.0, 0.0)
    p2, p1, n0 = prev[RG_HALO - 2:RG_HALO - 1], prev[RG_HALO - 1:RG_HALO], nxt[0:1]
    r8 = lax.broadcasted_iota(jnp.int32, (8, RG_CB), 0)
    head, tail = x[0:8], x[rows - 8:rows]
    h_m1 = jnp.where(r8 == 0, p1, pltpu.roll(head, 1, 0))
    h_m2 = jnp.where(r8 == 0, p2, jnp.where(r8 == 1, p1, pltpu.roll(head, 2, 0)))
    t_p1 = jnp.where(r8 == 7, n0, pltpu.roll(tail, 7, 0))
    xc_scr[0:8, :] = taps(h_m2, h_m1, head, x_p1[0:8])
    xc_scr[rows - 8:rows, :] = taps(x_m2[rows - 8:rows], x_m1[rows - 8:rows], tail, t_p1)

    lam = lam_ref[...]
    softplus_neg_lam = jnp.maximum(-lam, 0.0) + jnp.log1p(jnp.exp(-jnp.abs(lam)))
    sub = lax.broadcasted_iota(jnp.int32, (n_grp, 8, LANES), 1)
    for j in range(RG_CB // LANES):
        sl = slice(j * LANES, (j + 1) * LANES)
        xj = xc_scr[:, sl]
        gates = jnp.dot(xj.astype(BF16), wg_ref[j], preferred_element_type=F32)
        r = _sigmoid(gates[:, :LANES] + ba_ref[:, sl])
        ig = _sigmoid(gates[:, LANES:] + bi_ref[:, sl])
        log_a = -RG_C * r * softplus_neg_lam[:, sl]
        a = jnp.exp(log_a)
        th = jnp.tanh(log_a)
        b = jnp.sqrt(-2.0 * th / (1.0 - th)) * (ig * xj)
        a = a.reshape(n_grp, 8, LANES)
        b = b.reshape(n_grp, 8, LANES)
        for d in (1, 2, 4):
            shift = 8 - d if reverse else d
            ok = (sub < 8 - d) if reverse else (sub >= d)
            a_s, b_s = pltpu.roll(a, shift, 1), pltpu.roll(b, shift, 1)
            b = jnp.where(ok, a * b_s + b, b)
            a = jnp.where(ok, a * a_s, a)
        a_scr[:, sl] = a.reshape(rows, LANES)
        b_scr[:, sl] = b.reshape(rows, LANES)

    @pl.when(t == 0)
    def _():
        carry_scr[...] = jnp.broadcast_to(h0_ref[...], (8, RG_CB))

    def group(i, carry):
        g = (n_grp - 1 - i) if reverse else i
        off = pl.multiple_of(g * 8, 8)
        h = a_scr[pl.ds(off, 8), :] * carry + b_scr[pl.ds(off, 8), :]
        b_scr[pl.ds(off, 8), :] = h
        last = h[0:1] if reverse else h[7:8]
        return jnp.broadcast_to(last, (8, RG_CB))

    carry = lax.fori_loop(0, n_grp, group, carry_scr[...])
    carry_scr[...] = carry
    h_ref[...] = b_scr[...].astype(h_ref.dtype)
    hend_ref[...] = carry[0:1]


def _rglru(z3, conv_w, conv_b, wg, b_a, b_i, lam, h0, *, direction, tile):
    bsz, seq_len, _ = z3.shape
    n_t = seq_len // tile
    reverse = direction == 1
    n_cb = D // RG_CB
    xcol = ZXR // RG_CB
    hpt = tile // RG_HALO
    n_halo = seq_len // RG_HALO

    def tix(t):
        return (n_t - 1 - t) if reverse else t

    vec = lambda: pl.BlockSpec((None, 1, RG_CB), lambda b, c, t: (direction, 0, c))
    return pl.pallas_call(
        functools.partial(_rglru_body, reverse=reverse, n_t=n_t),
        grid=(bsz, n_cb, n_t),
        in_specs=[
            pl.BlockSpec((None, tile, RG_CB), lambda b, c, t: (b, tix(t), xcol + c)),
            pl.BlockSpec((None, RG_HALO, RG_CB),
                         lambda b, c, t: (b, jnp.maximum(tix(t) * hpt - 1, 0), xcol + c)),
            pl.BlockSpec((None, RG_HALO, RG_CB),
                         lambda b, c, t: (b, jnp.minimum((tix(t) + 1) * hpt, n_halo - 1), xcol + c)),
            pl.BlockSpec((4, RG_CB), lambda b, c, t: (0, c)),
            pl.BlockSpec((1, RG_CB), lambda b, c, t: (0, c)),
            pl.BlockSpec((None, RG_CB // LANES, LANES, 2 * LANES), lambda b, c, t: (direction, c, 0, 0)),
            vec(), vec(), vec(),
            pl.BlockSpec((None, 1, RG_CB), lambda b, c, t: (b, 0, c)),
        ],
        out_specs=[pl.BlockSpec((None, tile, RG_CB), lambda b, c, t: (b, tix(t), c)),
                   pl.BlockSpec((None, 1, RG_CB), lambda b, c, t: (b, 0, c))],
        out_shape=[jax.ShapeDtypeStruct((bsz, seq_len, D), BF16),
                   jax.ShapeDtypeStruct((bsz, 1, D), F32)],
        scratch_shapes=[pltpu.VMEM((tile, RG_CB), F32), pltpu.VMEM((tile, RG_CB), F32),
                        pltpu.VMEM((tile, RG_CB), F32), pltpu.VMEM((8, RG_CB), F32)],
        compiler_params=_params(("parallel", "parallel", "arbitrary")),
        name="rglru_bwd" if reverse else "rglru_fwd",
    )(z3, z3, z3, conv_w, conv_b.reshape(1, D), wg, b_a.reshape(2, 1, D), b_i.reshape(2, 1, D),
      lam.reshape(2, 1, D), h0)


def _gate_weights(w_a, w_i):
    def pair(w):
        w = w.reshape(2, 8, 2, 64, 64)
        z = jnp.zeros_like(w[:, :, 0])
        top = jnp.concatenate([w[:, :, 0], z], axis=-1)
        bot = jnp.concatenate([z, w[:, :, 1]], axis=-1)
        return jnp.concatenate([top, bot], axis=-2)
    return jnp.concatenate([pair(w_a), pair(w_i)], axis=-1).astype(BF16)


def _seqdft_body(c_ref, s_ref, xc_ref, xs_ref, o_ref, acc_ref):
    k = pl.program_id(2)

    @pl.when(k == 0)
    def _():
        acc_ref[...] = jnp.zeros_like(acc_ref)

    acc_ref[...] += (jnp.dot(c_ref[...], xc_ref[...], preferred_element_type=F32)
                     + jnp.dot(s_ref[...], xs_ref[...], preferred_element_type=F32))

    @pl.when(k == pl.num_programs(2) - 1)
    def _():
        o_ref[...] = acc_ref[...].astype(o_ref.dtype)


def _seq_dft(z3, cmat, nsmat, tile):
    bsz, seq_len, _ = z3.shape
    nt = seq_len // tile
    return pl.pallas_call(
        _seqdft_body,
        grid=(bsz, nt, nt),
        in_specs=[pl.BlockSpec((tile, tile), lambda b, i, k: (i, k)),
                  pl.BlockSpec((tile, tile), lambda b, i, k: (i, k)),
                  pl.BlockSpec((None, tile, D), lambda b, i, k: (b, k, ZXC // D)),
                  pl.BlockSpec((None, tile, D), lambda b, i, k: (b, k, ZXS // D))],
        out_specs=pl.BlockSpec((None, tile, D), lambda b, i, k: (b, i, 0)),
        out_shape=jax.ShapeDtypeStruct((bsz, seq_len, D), BF16),
        scratch_shapes=[pltpu.VMEM((tile, D), F32)],
        compiler_params=_params(("parallel", "parallel", "arbitrary")),
        name="seq_dft",
    )(cmat, nsmat, z3, z3)


def _dft_mats(n, scale):
    idx = jnp.arange(n, dtype=jnp.int32)
    ang = ((idx[:, None] * idx[None, :]) % n).astype(F32) * (2.0 * math.pi / n)
    return (jnp.cos(ang) * scale), (jnp.sin(ang) * scale)


FFT_L1 = 32
FFT_GRP = 16
FFT_K1B = 8


def _fft1_body(wa_ref, wb_ref, xc_ref, xs_ref, o_ref):
    rows = FFT_L1 * FFT_GRP
    xc = xc_ref[...].reshape(rows, D)
    xs = xs_ref[...].reshape(rows, D)
    a = (jnp.dot(wa_ref[...], xc, preferred_element_type=F32)
         + jnp.dot(wb_ref[...], xs, preferred_element_type=F32))
    a = a.astype(BF16)
    o_ref[:, :, :D] = a[:rows].reshape(FFT_L1, FFT_GRP, D)
    o_ref[:, :, D:] = a[rows:].reshape(FFT_L1, FFT_GRP, D)


def _fft2_body(m_ref, a_ref, o_ref):
    l2 = m_ref.shape[1]
    for j in range(FFT_K1B):
        o_ref[j] = (jnp.dot(m_ref[j, :, :l2], a_ref[j, :, :D], preferred_element_type=F32)
                    + jnp.dot(m_ref[j, :, l2:], a_ref[j, :, D:], preferred_element_type=F32)
                    ).astype(o_ref.dtype)


def _fft_tables(seq_len):
    l1, l2, g = FFT_L1, seq_len // FFT_L1, FFT_GRP
    i1 = jnp.arange(l1, dtype=jnp.int32)
    ang1 = ((i1[:, None] * i1[None, :]) % l1).astype(F32) * (2.0 * math.pi / l1)
    w1r, w1i = jnp.cos(ang1), -jnp.sin(ang1)
    eye = jnp.eye(g, dtype=F32)
    kron = lambda w: jnp.kron(w, eye)
    wa = jnp.concatenate([kron(w1r), kron(w1i)], axis=0).astype(BF16)
    wb = jnp.concatenate([kron(w1i), kron(-w1r)], axis=0).astype(BF16)
    i2 = jnp.arange(l2, dtype=jnp.int32)
    num = (i2[None, :, None] * i2[None, None, :] * l1 + i1[:, None, None] * i2[None, None, :]) % seq_len
    ang2 = num.astype(F32) * (2.0 * math.pi / seq_len)
    scale = seq_len ** -0.5
    m2 = jnp.concatenate([jnp.cos(ang2) * scale, jnp.sin(ang2) * scale], axis=2).astype(BF16)
    return wa, wb, m2


def _seq_fft(z, bsz, seq_len, tabs):
    wa, wb, m2 = tabs
    l1, l2, g = FFT_L1, seq_len // FFT_L1, FFT_GRP
    z4 = z.reshape(bsz, l1, l2, Z_COLS)
    rows = l1 * g
    a4 = pl.pallas_call(
        _fft1_body,
        grid=(bsz, l2 // g),
        in_specs=[_resident((2 * rows, rows)), _resident((2 * rows, rows)),
                  pl.BlockSpec((None, l1, g, D), lambda b, t: (b, 0, t, ZXC // D)),
                  pl.BlockSpec((None, l1, g, D), lambda b, t: (b, 0, t, ZXS // D))],
        out_specs=pl.BlockSpec((None, l1, g, 2 * D), lambda b, t: (b, 0, t, 0)),
        out_shape=jax.ShapeDtypeStruct((bsz, l1, l2, 2 * D), BF16),
        compiler_params=_params(("parallel", "parallel")),
        name="seq_fft_stage1",
    )(wa, wb, z4, z4)
    yp = pl.pallas_call(
        _fft2_body,
        grid=(l1 // FFT_K1B, bsz),
        in_specs=[pl.BlockSpec((FFT_K1B, l2, 2 * l2), lambda k, b: (k, 0, 0)),
                  pl.BlockSpec((None, FFT_K1B, l2, 2 * D), lambda k, b: (b, k, 0, 0))],
        out_specs=pl.BlockSpec((None, FFT_K1B, l2, D), lambda k, b: (b, k, 0, 0)),
        out_shape=jax.ShapeDtypeStruct((bsz, l1, l2, D), BF16),
        compiler_params=_params(("parallel", "parallel")),
        name="seq_fft_stage2",
    )(m2, a4)
    return yp.transpose(0, 2, 1, 3).reshape(bsz, seq_len, D)


def _gelu_tanh(x):
    return 0.5 * x * (1.0 + jnp.tanh(math.sqrt(2.0 / math.pi) * (x + 0.044715 * (x * x * x))))


MERGE_ROW_SPLIT = 2


def _merge_body(x_ref, mod_ref, attn_ref, gr_ref, hf_ref, hb_ref, four_ref,
                woa_ref, wor_ref, wof_ref, wm_ref, bm_ref, wout_ref, lng_ref, lnb_ref, wr_ref, br_ref,
                x1_ref, u2_ref, eid_ref, gate_ref, cnt_ref):
    mod = mod_ref[...]
    rows = x_ref.shape[0] // MERGE_ROW_SPLIT
    for s in range(MERGE_ROW_SPLIT):
        rs = slice(s * rows, (s + 1) * rows)
        x = x_ref[rs, :]
        u = (x * (1.0 + mod[1:2]) + mod[0:1]).astype(BF16)
        rg_in = (_gelu_tanh(gr_ref[rs, :].astype(F32))
                 * (hf_ref[rs, :].astype(F32) + hb_ref[rs, :].astype(F32))).astype(BF16)
        branches = ((attn_ref[rs, :], woa_ref), (rg_in, wor_ref), (four_ref[rs, :], wof_ref))
        merged = None
        for j, (inp, w_ref) in enumerate(branches):
            y = jnp.dot(inp, w_ref[...], preferred_element_type=F32)
            g = _sigmoid(jnp.dot(u, wm_ref[:, j * D:(j + 1) * D], preferred_element_type=F32)
                         + bm_ref[:, j * D:(j + 1) * D])
            merged = g * y if merged is None else merged + g * y
        y = jnp.dot(merged.astype(BF16), wout_ref[...], preferred_element_type=F32)
        x1 = _layer_norm(DEEPNORM_ALPHA * x + mod[2:3] * y, lng_ref[...], lnb_ref[...])
        x1_ref[rs, :] = x1
        u2_ref[rs, :] = (x1 * (1.0 + mod[4:5]) + mod[3:4]).astype(BF16)

    logits = jnp.dot(u2_ref[...], wr_ref[...], preferred_element_type=F32) + br_ref[...]
    lane = lax.broadcasted_iota(jnp.int32, logits.shape, 1)
    vals, idxs = [], []
    for _ in range(TOP_K):
        m = jnp.max(logits, axis=-1, keepdims=True)
        idx = jnp.min(jnp.where(logits == m, lane, LANES), axis=-1, keepdims=True)
        vals.append(m)
        idxs.append(idx)
        logits = jnp.where(lane == idx, -3.0e38, logits)
    exps = [jnp.exp(v - vals[0]) for v in vals]
    den = exps[0] + exps[1] + exps[2] + exps[3]
    eid = jnp.zeros(lane.shape, jnp.int32)
    gate = jnp.zeros(lane.shape, F32)
    member = jnp.zeros(lane.shape, F32)
    for k in range(TOP_K):
        eid = jnp.where(lane == k, idxs[k], eid)
        gate = jnp.where(lane == k, exps[k] / den, gate)
        member = member + jnp.where(lane == idxs[k], 1.0, 0.0)
    eid_ref[...] = eid
    gate_ref[...] = gate
    cnt_ref[...] = jnp.sum(member, axis=0, keepdims=True)


def _merge(x2, mods, attn, z, hf, hb, four, p, seq_len, tm):
    n = x2.shape[0]
    tpb = seq_len // tm
    bm = mods.shape[0]
    mod_map = (lambda i: (i // tpb, 0, 0)) if bm > 1 else (lambda i: (0, 0, 0))
    act = lambda: pl.BlockSpec((tm, D), lambda i: (i, 0))
    return pl.pallas_call(
        _merge_body,
        grid=(n // tm,),
        in_specs=[act(), pl.BlockSpec((None, 6, D), mod_map), act(),
                  pl.BlockSpec((tm, D), lambda i: (i, ZGR // D)), act(), act(), act(),
                  _resident((D, D)), _resident((D, D)), _resident((D, D)),
                  _resident((D, 3 * D)), _resident((1, 3 * D)), _resident((D, D)),
                  _resident((1, D)), _resident((1, D)), _resident((D, LANES)), _resident((1, LANES))],
        out_specs=[act(), act(), pl.BlockSpec((tm, LANES), lambda i: (i, 0)),
                   pl.BlockSpec((tm, LANES), lambda i: (i, 0)),
                   pl.BlockSpec((None, 1, LANES), lambda i: (i, 0, 0))],
        out_shape=[jax.ShapeDtypeStruct((n, D), F32), jax.ShapeDtypeStruct((n, D), BF16),
                   jax.ShapeDtypeStruct((n, LANES), jnp.int32), jax.ShapeDtypeStruct((n, LANES), F32),
                   jax.ShapeDtypeStruct((n // tm, 1, LANES), F32)],
        compiler_params=_params(("parallel",)),
        name="merge_ln_router",
    )(x2, mods, attn, z, hf, hb, four, p["w_o_attn"], p["w_o_rg"], p["w_o_four"], p["w_merge"],
      p["b_merge"], p["w_out"], p["ln1_g"], p["ln1_b"], p["w_router"], p["b_router"])


RUN_ALIGN = 8


def _sorted_rows(td):
    return TOP_K * td + N_EXPERTS * RUN_ALIGN


def _chunk_bits(max_len):
    bits, b = [], RUN_ALIGN
    while b <= max_len:
        bits.append(b)
        b *= 2
    return tuple(reversed(bits))


def _chunked_copies(wait, src_ref, dst_ref, src_off, dst_off, length, bits, sem):
    for bit in bits:
        @pl.when((length & bit) != 0)
        def _():
            off = length & (-2 * bit)
            cp = pltpu.make_async_copy(src_ref.at[pl.ds(pl.multiple_of(src_off + off, RUN_ALIGN), bit)],
                                       dst_ref.at[pl.ds(pl.multiple_of(dst_off + off, RUN_ALIGN), bit)], sem)
            if wait:
                cp.wait()
            else:
                cp.start()


def _slot_positions(eid, toff, ltri):
    lane = lax.broadcasted_iota(jnp.int32, eid.shape, 1)
    hits = [eid[:, k:k + 1] == lane for k in range(TOP_K)]
    member = jnp.zeros(eid.shape, F32)
    for h in hits:
        member = member + jnp.where(h, 1.0, 0.0)
    rank = jnp.dot(ltri, member.astype(BF16), preferred_element_type=F32)
    base = toff + rank
    return [jnp.sum(jnp.where(h, base, 0.0), axis=-1, keepdims=True) for h in hits]


def _dispatch_body(rl_ref, to_ref, ro_ref, tlo_ref, tll_ref, nu_ref, tt_ref, u_ref, eid_ref, toff_ref, ltri_ref,
                   xs_hbm, pos_ref, srt, sems, *, td, tb):
    t = pl.program_id(0)
    n_t = pl.num_programs(0)
    slot = t % 2
    rows = srt.shape[1]
    run_bits = _chunk_bits(td)

    def run_copies(wait, tile, s):
        def one(e, c):
            i = tile * N_EXPERTS + e
            _chunked_copies(wait, srt.at[s], xs_hbm, to_ref[i], ro_ref[i], rl_ref[i], run_bits, sems.at[s])
            return c
        lax.fori_loop(0, N_EXPERTS, one, 0)

    @pl.when(t == 0)
    def _():
        srt[1, :tb, :] = jnp.zeros((tb, D), F32)
        tail_bits = _chunk_bits(tb)
        for wait in (False, True):
            def one(e, c):
                _chunked_copies(wait, srt.at[1], xs_hbm, 0, tlo_ref[e], tll_ref[e], tail_bits, sems.at[1])
                return c
            lax.fori_loop(0, N_EXPERTS, one, 0)
        n_blocks = xs_hbm.shape[0] // tb
        for wait in (False, True):
            def blk(j, c):
                cp = pltpu.make_async_copy(srt.at[1, pl.ds(0, tb)],
                                           xs_hbm.at[pl.ds(pl.multiple_of(j * tb, RUN_ALIGN), tb)], sems.at[1])
                if wait:
                    cp.wait()
                else:
                    cp.start()
                return c
            lax.fori_loop(nu_ref[0], n_blocks, blk, 0)

    pos = _slot_positions(eid_ref[...], toff_ref[...], ltri_ref[...])
    lane = lax.broadcasted_iota(jnp.int32, (td, LANES), 1)
    pos_out = jnp.zeros((td, LANES), jnp.int32)
    for k in range(TOP_K):
        pos_out = jnp.where(lane == k, pos[k].astype(jnp.int32), pos_out)
    pos_ref[...] = pos_out

    col = lax.broadcasted_iota(jnp.int32, (td, rows), 1).astype(F32)
    hit = col == pos[0]
    for k in range(1, TOP_K):
        hit = hit | (col == pos[k])
    onehot_t = jnp.where(hit, 1.0, 0.0).astype(BF16)
    srt[slot] = lax.dot_general(onehot_t, u_ref[...], (((0,), (0,)), ((), ())),
                                preferred_element_type=F32)
    run_copies(False, t, slot)

    def wait_runs(tile, s):
        _chunked_copies(True, srt.at[s], xs_hbm, 0, 0, tt_ref[tile], _chunk_bits(rows), sems.at[s])

    @pl.when(t >= 1)
    def _():
        wait_runs(t - 1, 1 - slot)

    @pl.when(t == n_t - 1)
    def _():
        wait_runs(t, slot)


def _dispatch(u2, eid128, rt, td, tb):
    n_tok = u2.shape[0]
    n_t = n_tok // td
    rows = _sorted_rows(td)
    ltri = (jnp.arange(td)[:, None] > jnp.arange(td)[None, :]).astype(BF16)
    grid_spec = pltpu.PrefetchScalarGridSpec(
        num_scalar_prefetch=7,
        grid=(n_t,),
        in_specs=[pl.BlockSpec((td, D), lambda t, *_: (t, 0)),
                  pl.BlockSpec((td, LANES), lambda t, *_: (t, 0)),
                  pl.BlockSpec((None, 1, LANES), lambda t, *_: (t, 0, 0)),
                  pl.BlockSpec((td, td), lambda t, *_: (0, 0), pipeline_mode=pl.Buffered(1))],
        out_specs=[pl.BlockSpec(memory_space=pl.ANY),
                   pl.BlockSpec((td, LANES), lambda t, *_: (t, 0))],
        scratch_shapes=[pltpu.VMEM((2, rows, D), F32), pltpu.SemaphoreType.DMA((2,))],
    )
    return pl.pallas_call(
        functools.partial(_dispatch_body, td=td, tb=tb),
        grid_spec=grid_spec,
        out_shape=[jax.ShapeDtypeStruct((rt["n_slots"], D), F32),
                   jax.ShapeDtypeStruct((n_tok, LANES), jnp.int32)],
        compiler_params=_params(("arbitrary",)),
        name="moe_dispatch",
    )(rt["run_len"], rt["tile_off"], rt["run_off"], rt["tail_off"], rt["tail_len"], rt["n_used"], rt["tile_tot"],
      u2, eid128, rt["toff_f"], ltri)


MOE_ROW_SPLIT = 2


def _moe_body(be_ref, nused_ref, xs_ref, wgu_ref, bgu_ref, wdn_ref, bdn_ref, ys_ref, wgu_bf, wdn_bf):
    i = pl.program_id(0)
    n_used = nused_ref[0]

    @pl.when((i < n_used) & ((i == 0) | (be_ref[i] != be_ref[jnp.maximum(i - 1, 0)])))
    def _():
        wgu_bf[...] = wgu_ref[...].astype(BF16)
        wdn_bf[...] = wdn_ref[...].astype(BF16)

    @pl.when(i < n_used)
    def _():
        rows = xs_ref.shape[0] // MOE_ROW_SPLIT
        for s in range(MOE_ROW_SPLIT):
            rs = slice(s * rows, (s + 1) * rows)
            x = xs_ref[rs, :].astype(BF16)
            h = jnp.dot(x, wgu_bf[...], preferred_element_type=F32) + bgu_ref[...]
            half = h.shape[1] // 2
            x_glu = jnp.minimum(h[:, :half], SWIGLU_LIMIT)
            x_lin = jnp.clip(h[:, half:], -SWIGLU_LIMIT, SWIGLU_LIMIT)
            act = (x_glu * _sigmoid(SWIGLU_ALPHA * x_glu) * (x_lin + 1.0)).astype(BF16)
            ys_ref[rs, :] = jnp.dot(act, wdn_bf[...], preferred_element_type=F32) + bdn_ref[...]

    @pl.when(i >= n_used)
    def _():
        ys_ref[...] = jnp.zeros_like(ys_ref)


def _moe(xs, block_e, n_used, layer, w_gu, b_gu, w_dn, b_dn, tb):
    n_blocks = xs.shape[0] // tb
    d_ff2 = w_gu.shape[3]
    live = lambda i, nu: jnp.minimum(i, nu[0] - 1)
    grid_spec = pltpu.PrefetchScalarGridSpec(
        num_scalar_prefetch=2,
        grid=(n_blocks,),
        in_specs=[pl.BlockSpec((tb, D), lambda i, be, nu: (live(i, nu), 0)),
                  pl.BlockSpec((None, None, D, d_ff2), lambda i, be, nu: (layer, be[live(i, nu)], 0, 0)),
                  pl.BlockSpec((None, None, 1, d_ff2), lambda i, be, nu: (layer, be[live(i, nu)], 0, 0)),
                  pl.BlockSpec((None, None, d_ff2 // 2, D), lambda i, be, nu: (layer, be[live(i, nu)], 0, 0)),
                  pl.BlockSpec((None, None, 1, D), lambda i, be, nu: (layer, be[live(i, nu)], 0, 0))],
        out_specs=pl.BlockSpec((tb, D), lambda i, be, nu: (i, 0)),
        scratch_shapes=[pltpu.VMEM((D, d_ff2), BF16), pltpu.VMEM((d_ff2 // 2, D), BF16)],
    )
    return pl.pallas_call(
        _moe_body,
        grid_spec=grid_spec,
        out_shape=jax.ShapeDtypeStruct(xs.shape, F32),
        compiler_params=_params(("arbitrary",)),
        name="moe_experts",
    )(block_e, n_used, xs, w_gu, b_gu, w_dn, b_dn)


def _route(cnt128, n_tok, td, tb):
    n_t = n_tok // td
    cnt = cnt128[:, 0, :N_EXPERTS].astype(jnp.int32)
    run = (cnt + RUN_ALIGN - 1) // RUN_ALIGN * RUN_ALIGN
    tile_off = jnp.cumsum(run, axis=1) - run
    total = run.sum(axis=0)
    padded = (total + tb - 1) // tb * tb
    pend = jnp.cumsum(padded)
    pstart = pend - padded
    run_off = pstart[None, :] + jnp.cumsum(run, axis=0) - run
    worst = TOP_K * n_tok + n_t * N_EXPERTS * RUN_ALIGN + N_EXPERTS * tb
    n_slots = (worst + tb - 1) // tb * tb
    first_slot = jnp.arange(n_slots // tb, dtype=jnp.int32) * tb
    block_e = jnp.minimum(jnp.sum((pend[None, :] <= first_slot[:, None]).astype(jnp.int32), axis=1),
                          N_EXPERTS - 1)
    toff_f = jnp.zeros((n_t, 1, LANES), F32).at[:, 0, :N_EXPERTS].set(tile_off.astype(F32))
    i32 = lambda a: a.astype(jnp.int32).reshape(-1)
    return dict(run_len=i32(run), tile_off=i32(tile_off), run_off=i32(run_off),
                tail_off=i32(pstart + total), tail_len=i32(padded - total), toff_f=toff_f,
                tile_tot=i32(run.sum(axis=1)),
                block_e=i32(block_e), n_used=i32(pend[-1] // tb), n_slots=n_slots)


def _combine_body(rl_ref, to_ref, ro_ref, tt_ref, ys_hbm, x1_ref, mod_ref, gate_ref, pos_ref, lng_ref, lnb_ref,
                  o_ref, srt, sems, *, td):
    t = pl.program_id(0)
    n_t = pl.num_programs(0)
    slot = t % 2
    rows = srt.shape[1]
    run_bits = _chunk_bits(td)

    def run_copies(wait, tile, s):
        def one(e, c):
            i = tile * N_EXPERTS + e
            _chunked_copies(wait, ys_hbm, srt.at[s], ro_ref[i], to_ref[i], rl_ref[i], run_bits, sems.at[s])
            return c
        lax.fori_loop(0, N_EXPERTS, one, 0)

    @pl.when(t == 0)
    def _():
        srt[...] = jnp.zeros_like(srt)
        run_copies(False, 0, 0)

    @pl.when(t + 1 < n_t)
    def _():
        run_copies(False, t + 1, 1 - slot)

    _chunked_copies(True, ys_hbm, srt.at[slot], 0, 0, tt_ref[t], _chunk_bits(rows), sems.at[slot])

    gate = gate_ref[...]
    pos = pos_ref[...]
    col = lax.broadcasted_iota(jnp.int32, (td, rows), 1)
    weights = jnp.zeros((td, rows), F32)
    for k in range(TOP_K):
        weights = jnp.where(col == pos[:, k:k + 1], gate[:, k:k + 1], weights)
    f = jnp.dot(weights.astype(BF16), srt[slot].astype(BF16), preferred_element_type=F32)
    mod = mod_ref[...]
    o_ref[...] = _layer_norm(DEEPNORM_ALPHA * x1_ref[...] + mod[5:6] * f, lng_ref[...], lnb_ref[...])


def _combine(x1, mods, gate128, pos128, ys, rt, ln_g, ln_b, seq_len, td):
    n = x1.shape[0]
    tpb = seq_len // td
    bm = mods.shape[0]
    mod_map = (lambda i, *_: (i // tpb, 0, 0)) if bm > 1 else (lambda i, *_: (0, 0, 0))
    grid_spec = pltpu.PrefetchScalarGridSpec(
        num_scalar_prefetch=4,
        grid=(n // td,),
        in_specs=[pl.BlockSpec(memory_space=pl.ANY),
                  pl.BlockSpec((td, D), lambda i, *_: (i, 0)),
                  pl.BlockSpec((None, 6, D), mod_map),
                  pl.BlockSpec((td, LANES), lambda i, *_: (i, 0)),
                  pl.BlockSpec((td, LANES), lambda i, *_: (i, 0)),
                  pl.BlockSpec((1, D), lambda i, *_: (0, 0)),
                  pl.BlockSpec((1, D), lambda i, *_: (0, 0))],
        out_specs=pl.BlockSpec((td, D), lambda i, *_: (i, 0)),
        scratch_shapes=[pltpu.VMEM((2, _sorted_rows(td), D), F32), pltpu.SemaphoreType.DMA((2,))],
    )
    return pl.pallas_call(
        functools.partial(_combine_body, td=td),
        grid_spec=grid_spec,
        out_shape=jax.ShapeDtypeStruct((n, D), F32),
        compiler_params=_params(("arbitrary",)),
        name="combine_ln",
    )(rt["run_len"], rt["tile_off"], rt["run_off"], rt["tile_tot"], ys, x1, mods, gate128, pos128,
      ln_g, ln_b)


def _rope_tables(seq_len):
    pos = jnp.arange(seq_len, dtype=jnp.int32)
    row = (pos // GRID_W).astype(F32)
    col = (pos % GRID_W).astype(F32)
    n_freq = HEAD_DIM // 4
    freqs = ROPE_BASE ** (-jnp.arange(n_freq, dtype=F32) / n_freq)
    ar, ac = row[:, None] * freqs, col[:, None] * freqs
    zero = jnp.zeros_like(ar)
    c = jnp.concatenate([jnp.cos(ar), jnp.cos(ar), jnp.cos(ac), jnp.cos(ac)], axis=1)
    sa = jnp.concatenate([-jnp.sin(ar), zero, -jnp.sin(ac), zero], axis=1)
    sb = jnp.concatenate([zero, jnp.sin(ar), zero, jnp.sin(ac)], axis=1)
    rep = MXU_N // HEAD_DIM
    return tuple(jnp.tile(t, (1, rep)) for t in (c, sa, sb))


def _layer_params(l, w):
    w_router = jnp.zeros((D, LANES), BF16).at[:, :N_EXPERTS].set(w["w_router"][l].astype(BF16))
    b_router = jnp.full((1, LANES), NEG_INF, F32).at[0, :N_EXPERTS].set(w["b_router"][l])
    return dict(
        w_in=w["w_in"][l].astype(BF16),
        w_o_attn=w["w_o_attn"][l].astype(BF16), w_o_rg=w["w_o_rg"][l].astype(BF16),
        w_o_four=w["w_o_four"][l].astype(BF16), w_merge=w["w_merge"][l].astype(BF16),
        b_merge=w["b_merge"][l].reshape(1, 3 * D), w_out=w["w_out"][l].astype(BF16),
        ln1_g=w["ln1_g"][l].reshape(1, D), ln1_b=w["ln1_b"][l].reshape(1, D),
        ln2_g=w["ln2_g"][l].reshape(1, D), ln2_b=w["ln2_b"][l].reshape(1, D),
        w_router=w_router, b_router=b_router,
        layer=l, w_gu=w["w_gate_up"], b_gu=w["b_gate_up"].reshape(DEPTH, N_EXPERTS, 1, -1),
        w_dn=w["w_down"], b_dn=w["b_down"].reshape(DEPTH, N_EXPERTS, 1, D),
        wg=_gate_weights(w["rg_w_a"][l], w["rg_w_i"][l]),
        conv_w=w["conv_w"][l], conv_b=w["conv_b"][l], b_a=w["rg_b_a"][l], b_i=w["rg_b_i"][l],
        lam=w["rg_lambda"][l], sink=w["attn_sink"][l],
    )


def _row_tile(seq_len, want):
    return min(want, seq_len)


def _ffn(x1, u2, eid128, gate128, cnt128, mods, p, seq_len, tb, td):
    rt = _route(cnt128, x1.shape[0], td, tb)
    xs, pos128 = _dispatch(u2, eid128, rt, td, tb)
    ys = _moe(xs, rt["block_e"], rt["n_used"], p["layer"], p["w_gu"], p["b_gu"], p["w_dn"], p["b_dn"], tb)
    return _combine(x1, mods, gate128, pos128, ys, rt, p["ln2_g"], p["ln2_b"], seq_len, td)


def kernel(x, c, ctx, c_ctx, w_mod, b_mod, w_in, attn_sink, w_o_attn, conv_w, conv_b, rg_w_a, rg_b_a, rg_w_i, rg_b_i, rg_lambda, w_o_rg, w_o_four, w_merge, b_merge, w_out, ln1_g, ln1_b, w_router, b_router, w_gate_up, b_gate_up, w_down, b_down, ln2_g, ln2_b):
    w = dict(w_mod=w_mod, b_mod=b_mod, w_in=w_in, attn_sink=attn_sink, w_o_attn=w_o_attn, conv_w=conv_w,
             conv_b=conv_b, rg_w_a=rg_w_a, rg_b_a=rg_b_a, rg_w_i=rg_w_i, rg_b_i=rg_b_i, rg_lambda=rg_lambda,
             w_o_rg=w_o_rg, w_o_four=w_o_four, w_merge=w_merge, b_merge=b_merge, w_out=w_out, ln1_g=ln1_g,
             ln1_b=ln1_b, w_router=w_router, b_router=b_router, w_gate_up=w_gate_up, b_gate_up=b_gate_up,
             w_down=w_down, b_down=b_down, ln2_g=ln2_g, ln2_b=ln2_b)
    bsz, seq_len, _ = x.shape
    ctx_len = ctx.shape[1]
    n_lat, n_ctx = bsz * seq_len, bsz * ctx_len
    tm_lat, tm_ctx = _row_tile(seq_len, 512), _row_tile(ctx_len, 256)
    dft_lat, dft_ctx = _row_tile(seq_len, 1024), _row_tile(ctx_len, 256)
    rg_lat, rg_ctx = _row_tile(seq_len, 256), _row_tile(ctx_len, 256)
    tb_lat, tb_ctx = 512, 128

    rope = _rope_tables(seq_len)
    cw, sw = _dft_mats(F_GW, F_GW ** -0.5)
    csw = jnp.concatenate([cw, sw], axis=1).astype(BF16)
    use_fft = seq_len % (FFT_L1 * FFT_GRP) == 0
    if use_fft:
        fft_tabs = _fft_tables(seq_len)
    else:
        cl, sl = _dft_mats(seq_len, seq_len ** -0.5)
        cl, nsl = cl.astype(BF16), (-sl).astype(BF16)
    cc, sc = _dft_mats(ctx_len, ctx_len ** -0.5)
    cc, nsc = cc.astype(BF16), (-sc).astype(BF16)

    cond = jnp.zeros((16, D), F32).at[:bsz].set(c).at[bsz].set(c_ctx)
    x2 = x.reshape(n_lat, D)
    ctx2 = ctx.reshape(n_ctx, D)
    zero_h = jnp.zeros((bsz, 1, D), F32)

    for l in range(DEPTH):
        last = l == DEPTH - 1
        p = _layer_params(l, w)
        m = _adaln(cond, w_mod[l], b_mod[l]).reshape(16, 6, D)
        mods_lat, mods_ctx = m[:bsz], m[bsz:bsz + 1]

        zc = _proj(ctx2, mods_ctx, p["w_in"], csw, None, ctx_len, tm_ctx)
        z = _proj(x2, mods_lat, p["w_in"], csw, rope, seq_len, tm_lat)
        zc3 = zc.reshape(bsz, ctx_len, Z_COLS)
        z3 = z.reshape(bsz, seq_len, Z_COLS)

        rg = lambda zz, h0, d, tile: _rglru(zz, p["conv_w"], p["conv_b"], p["wg"], p["b_a"], p["b_i"],
                                            p["lam"], h0, direction=d, tile=tile)
        hcf, endf = rg(zc3, zero_h, 0, rg_ctx)
        hcb, endb = rg(zc3, zero_h, 1, rg_ctx)
        hf, _ = rg(z3, endf, 0, rg_lat)
        hb, _ = rg(z3, endb, 1, rg_lat)

        attn = _attention(z3, zc3, p["sink"])
        four = _seq_fft(z, bsz, seq_len, fft_tabs) if use_fft else _seq_dft(z3, cl, nsl, dft_lat)
        x1, u2, eid, gate, cnt = _merge(x2, mods_lat, attn.reshape(n_lat, D), z, hf.reshape(n_lat, D),
                                   hb.reshape(n_lat, D), four.reshape(n_lat, D), p, seq_len, tm_lat)
        x2 = _ffn(x1, u2, eid, gate, cnt, mods_lat, p, seq_len, tb_lat, tm_lat)

        if not last:
            attn_c = _ctx_attention(zc3, p["sink"])
            four_c = _seq_dft(zc3, cc, nsc, dft_ctx)
            c1, uc2, eid_c, gate_c, cnt_c = _merge(ctx2, mods_ctx, attn_c.reshape(n_ctx, D), zc,
                                            hcf.reshape(n_ctx, D), hcb.reshape(n_ctx, D),
                                            four_c.reshape(n_ctx, D), p, ctx_len, tm_ctx)
            ctx2 = _ffn(c1, uc2, eid_c, gate_c, cnt_c, mods_ctx, p, ctx_len, tb_ctx, tm_ctx)

    return x2.reshape(bsz, seq_len, D)
```

```python
import functools
import math

import jax
import jax.numpy as jnp
from jax import lax
from jax.experimental import pallas as pl
from jax.experimental.pallas import tpu as pltpu

F32 = jnp.float32
BF16 = jnp.bfloat16

D = 1024
HEAD_DIM = 64
N_HEADS = 16
N_KV = 4
Q_PER_KV = 4
WINDOW = 128
QBLK = 128
assert WINDOW == QBLK
GRID_W = 64
ROPE_BASE = 10000.0
RG_C = 8.0
CONV_LEFT = 2
F_GROUPS = 4
F_GW = 256
N_EXPERTS = 32
TOP_K = 4
SWIGLU_LIMIT = 7.0
SWIGLU_ALPHA = 1.702
LN_EPS = 1e-5
DEPTH = 2
DEEPNORM_ALPHA = (2 * DEPTH) ** 0.25
NEG_INF = -1e30

Q_OFF, K_OFF, V_OFF, XR_OFF, GR_OFF, XF_OFF, IN_COLS = 0, 1024, 1280, 1536, 2560, 3584, 4608
ZQ, ZXR, ZGR, ZXC, ZXS, ZK, ZV, Z_COLS = 0, 1024, 2048, 3072, 4096, 5120, 5376, 5632

VMEM_LIMIT_V7X = 56 * 1024 * 1024
LANES = 128
MXU_N = 256


def _params(sem, vmem=VMEM_LIMIT_V7X):
    return pltpu.CompilerParams(dimension_semantics=sem, vmem_limit_bytes=vmem)


def _resident(shape):
    nd = len(shape)
    return pl.BlockSpec(shape, lambda *_: (0,) * nd, pipeline_mode=pl.Buffered(1))


def _sigmoid(x):
    return 1.0 / (1.0 + jnp.exp(-x))


def _layer_norm(r, g, b):
    mu = jnp.mean(r, axis=-1, keepdims=True)
    rc = r - mu
    var = jnp.mean(rc * rc, axis=-1, keepdims=True)
    return rc * lax.rsqrt(var + LN_EPS) * g + b


def _adaln_body(c_ref, w_ref, b_ref, o_ref):
    c = c_ref[...]
    s = (c * _sigmoid(c)).astype(BF16)
    o_ref[...] = jnp.dot(s, w_ref[...].astype(BF16), preferred_element_type=F32) + b_ref[...]


def _adaln(cond, w_mod, b_mod):
    rows, n = cond.shape[0], w_mod.shape[1]
    tn = 1024
    return pl.pallas_call(
        _adaln_body,
        grid=(n // tn,),
        in_specs=[pl.BlockSpec((rows, D), lambda j: (0, 0)),
                  pl.BlockSpec((D, tn), lambda j: (0, j)),
                  pl.BlockSpec((1, tn), lambda j: (0, j))],
        out_specs=pl.BlockSpec((rows, tn), lambda j: (0, j)),
        out_shape=jax.ShapeDtypeStruct((rows, n), F32),
        compiler_params=_params(("arbitrary",)),
        name="adaln",
    )(cond, w_mod, b_mod.reshape(1, n))


def _proj_dst(j):
    if j < 4:
        return ZQ + j * MXU_N
    if j == 4:
        return ZK
    if j == 5:
        return ZV
    if j < 10:
        return ZXR + (j - 6) * MXU_N
    if j < 14:
        return ZGR + (j - 10) * MXU_N
    return None


def _proj_body(x_ref, mod_ref, w_ref, csw_ref, *rest, rope):
    if rope:
        c_ref, sa_ref, sb_ref, o_ref = rest
    else:
        (o_ref,) = rest
    mod = mod_ref[...]
    u = (x_ref[...] * (1.0 + mod[1:2]) + mod[0:1]).astype(BF16)
    for j in range(IN_COLS // MXU_N):
        acc = jnp.dot(u, w_ref[:, j * MXU_N:(j + 1) * MXU_N], preferred_element_type=F32)
        if rope and j < 5:
            acc = (acc * c_ref[...] + pltpu.roll(acc, MXU_N - 16, 1) * sa_ref[...]
                   + pltpu.roll(acc, 16, 1) * sb_ref[...])
        dst = _proj_dst(j)
        if dst is not None:
            o_ref[:, dst:dst + MXU_N] = acc.astype(BF16)
        else:
            g = j - 14
            t = jnp.dot(acc.astype(BF16), csw_ref[...], preferred_element_type=F32)
            o_ref[:, ZXC + g * F_GW:ZXC + (g + 1) * F_GW] = t[:, :F_GW].astype(BF16)
            o_ref[:, ZXS + g * F_GW:ZXS + (g + 1) * F_GW] = t[:, F_GW:].astype(BF16)


def _proj(x2, mods, w_in, csw, rope_tabs, seq_len, tm):
    n = x2.shape[0]
    tpb = seq_len // tm
    bm = mods.shape[0]
    mod_map = (lambda i: (i // tpb, 0, 0)) if bm > 1 else (lambda i: (0, 0, 0))
    in_specs = [pl.BlockSpec((tm, D), lambda i: (i, 0)),
                pl.BlockSpec((None, 6, D), mod_map),
                _resident((D, IN_COLS)),
                _resident((F_GW, 2 * F_GW))]
    args = [x2, mods, w_in, csw]
    if rope_tabs is not None:
        in_specs += [pl.BlockSpec((tm, MXU_N), lambda i: (i % tpb, 0))] * 3
        args += list(rope_tabs)
    return pl.pallas_call(
        functools.partial(_proj_body, rope=rope_tabs is not None),
        grid=(n // tm,),
        in_specs=in_specs,
        out_specs=pl.BlockSpec((tm, Z_COLS), lambda i: (i, 0)),
        out_shape=jax.ShapeDtypeStruct((n, Z_COLS), BF16),
        compiler_params=_params(("parallel",)),
        name="proj_rope" if rope_tabs is not None else "proj_ctx",
    )(*args)


def _group_attention(sink_ref, q_ref, o_ref, g, parts):
    rows = q_ref.shape[0]
    key_low = g % 2 == 0
    lane = lax.broadcasted_iota(jnp.int32, (1, LANES), 1)
    keep = (lane < HEAD_DIM) if key_low else (lane >= HEAD_DIM)
    kv_lanes = slice((g // 2) * LANES, (g // 2 + 1) * LANES)

    blocks = []
    for j in range(2):
        q32 = q_ref[:, (2 * g + j) * LANES:(2 * g + j + 1) * LANES].astype(F32) * (HEAD_DIM ** -0.5)
        same, rot = q32.astype(BF16), pltpu.roll(q32, HEAD_DIM, 1).astype(BF16)
        blocks += [same, rot] if key_low else [rot, same]
    qs = jnp.concatenate(blocks, axis=0)

    row = lax.broadcasted_iota(jnp.int32, (Q_PER_KV * rows, 1), 0)
    sink = jnp.full((Q_PER_KV * rows, 1), sink_ref[g * Q_PER_KV + Q_PER_KV - 1], F32)
    for r in range(Q_PER_KV - 2, -1, -1):
        sink = jnp.where(row < (r + 1) * rows, sink_ref[g * Q_PER_KV + r], sink)

    def lane_blocks(a):
        return [a[:, c * LANES:(c + 1) * LANES] for c in range(a.shape[1] // LANES)]

    scores = []
    m_fold = None
    for k_ref, _, bias in parts:
        k2 = jnp.where(keep, k_ref[:, kv_lanes], jnp.zeros((), BF16))
        s = lax.dot_general(qs, k2, (((1,), (1,)), ((), ())), preferred_element_type=F32)
        if bias is not None:
            s = s + bias
        scores.append(s)
        for blk in lane_blocks(s):
            m_fold = blk if m_fold is None else jnp.maximum(m_fold, blk)
    m = jnp.maximum(sink, jnp.max(m_fold, axis=-1, keepdims=True))
    p_fold = None
    out = None
    for s, (_, v_ref, _) in zip(scores, parts):
        p = jnp.exp(s - m)
        for blk in lane_blocks(p):
            p_fold = blk if p_fold is None else p_fold + blk
        v2 = jnp.where(keep, v_ref[:, kv_lanes], jnp.zeros((), BF16))
        o = jnp.dot(p.astype(BF16), v2, preferred_element_type=F32)
        out = o if out is None else out + o
    den = jnp.exp(sink - m) + jnp.sum(p_fold, axis=-1, keepdims=True)
    out = out / den
    for j in range(2):
        first, second = out[2 * j * rows:(2 * j + 1) * rows], out[(2 * j + 1) * rows:(2 * j + 2) * rows]
        if key_low:
            both = first + pltpu.roll(second, HEAD_DIM, 1)
        else:
            both = pltpu.roll(first, HEAD_DIM, 1) + second
        o_ref[:, (2 * g + j) * LANES:(2 * g + j + 1) * LANES] = both.astype(o_ref.dtype)


def _attn_body(sink_ref, q_ref, kl_ref, km_ref, kr_ref, vl_ref, vm_ref, vr_ref, kc_ref, vc_ref,
               o_ref, *, n_blk):
    n = pl.program_id(1)
    i = lax.broadcasted_iota(jnp.int32, (Q_PER_KV * QBLK, QBLK), 0) % QBLK
    j = lax.broadcasted_iota(jnp.int32, (Q_PER_KV * QBLK, QBLK), 1)
    bias_prev = jnp.where((j >= i) & (n > 0), 0.0, NEG_INF).astype(F32)
    bias_next = jnp.where((j <= i) & (n < n_blk - 1), 0.0, NEG_INF).astype(F32)
    parts = ((kl_ref, vl_ref, bias_prev), (km_ref, vm_ref, None), (kr_ref, vr_ref, bias_next),
             (kc_ref, vc_ref, None))
    for g in range(N_KV):
        _group_attention(sink_ref, q_ref, o_ref, g, parts)


def _attention(z3, zc3, sink):
    bsz, seq_len, _ = z3.shape
    ctx_len = zc3.shape[1]
    nb = seq_len // QBLK
    kcol, vcol = ZK // MXU_N, ZV // MXU_N

    def blk(col, off):
        return pl.BlockSpec((None, QBLK, MXU_N),
                            lambda b, n: (b, jnp.clip(n + off, 0, nb - 1), col))

    return pl.pallas_call(
        functools.partial(_attn_body, n_blk=nb),
        grid=(bsz, nb),
        in_specs=[pl.BlockSpec(memory_space=pltpu.SMEM),
                  pl.BlockSpec((None, QBLK, D), lambda b, n: (b, n, 0)),
                  blk(kcol, -1), blk(kcol, 0), blk(kcol, 1),
                  blk(vcol, -1), blk(vcol, 0), blk(vcol, 1),
                  pl.BlockSpec((None, ctx_len, MXU_N), lambda b, n: (b, 0, kcol)),
                  pl.BlockSpec((None, ctx_len, MXU_N), lambda b, n: (b, 0, vcol))],
        out_specs=pl.BlockSpec((None, QBLK, D), lambda b, n: (b, n, 0)),
        out_shape=jax.ShapeDtypeStruct((bsz, seq_len, D), BF16),
        compiler_params=_params(("parallel", "parallel")),
        name="window_attn",
    )(sink, z3, z3, z3, z3, z3, z3, z3, zc3, zc3)


def _ctx_attn_body(sink_ref, q_ref, kc_ref, vc_ref, o_ref):
    for g in range(N_KV):
        _group_attention(sink_ref, q_ref, o_ref, g, ((kc_ref, vc_ref, None),))


def _ctx_attention(zc3, sink):
    bsz, ctx_len, _ = zc3.shape
    kcol, vcol = ZK // MXU_N, ZV // MXU_N
    return pl.pallas_call(
        _ctx_attn_body,
        grid=(bsz,),
        in_specs=[pl.BlockSpec(memory_space=pltpu.SMEM),
                  pl.BlockSpec((None, ctx_len, D), lambda b: (b, 0, 0)),
                  pl.BlockSpec((None, ctx_len, MXU_N), lambda b: (b, 0, kcol)),
                  pl.BlockSpec((None, ctx_len, MXU_N), lambda b: (b, 0, vcol))],
        out_specs=pl.BlockSpec((None, ctx_len, D), lambda b: (b, 0, 0)),
        out_shape=jax.ShapeDtypeStruct((bsz, ctx_len, D), BF16),
        compiler_params=_params(("parallel",)),
        name="ctx_attn",
    )(sink, zc3, zc3, zc3)


RG_CB = 512
RG_HALO = 16


def _rglru_body(x_ref, xp_ref, xn_ref, cw_ref, cb_ref, wg_ref, ba_ref, bi_ref, lam_ref, h0_ref,
                h_ref, hend_ref, a_scr, b_scr, xc_scr, carry_scr, *, reverse, n_t):
    t = pl.program_id(2)
    t_idx = (n_t - 1 - t) if reverse else t
    rows = x_ref.shape[0]
    n_grp = rows // 8
    x = x_ref[...].astype(F32)

    cw = cw_ref[...]

    def taps(xm2, xm1, x0, xp1):
        return cb_ref[...] + xm2 * cw[0:1] + xm1 * cw[1:2] + x0 * cw[2:3] + xp1 * cw[3:4]

    x_m1, x_m2, x_p1 = pltpu.roll(x, 1, 0), pltpu.roll(x, 2, 0), pltpu.roll(x, rows - 1, 0)
    xc_scr[...] = taps(x_m2, x_m1, x, x_p1)
    prev = xp_ref[...].astype(F32) * jnp.where(t_idx > 0, 1.0, 0.0)
    nxt = xn_ref[...].astype(F32) * jnp.where(t_idx < n_t - 1, 1.0, 0.0)
    p2, p1, n0 = prev[RG_HALO - 2:RG_HALO - 1], prev[RG_HALO - 1:RG_HALO], nxt[0:1]
    r8 = lax.broadcasted_iota(jnp.int32, (8, RG_CB), 0)
    head, tail = x[0:8], x[rows - 8:rows]
    h_m1 = jnp.where(r8 == 0, p1, pltpu.roll(head, 1, 0))
    h_m2 = jnp.where(r8 == 0, p2, jnp.where(r8 == 1, p1, pltpu.roll(head, 2, 0)))
    t_p1 = jnp.where(r8 == 7, n0, pltpu.roll(tail, 7, 0))
    xc_scr[0:8, :] = taps(h_m2, h_m1, head, x_p1[0:8])
    xc_scr[rows - 8:rows, :] = taps(x_m2[rows - 8:rows], x_m1[rows - 8:rows], tail, t_p1)

    lam = lam_ref[...]
    softplus_neg_lam = jnp.maximum(-lam, 0.0) + jnp.log1p(jnp.exp(-jnp.abs(lam)))
    sub = lax.broadcasted_iota(jnp.int32, (n_grp, 8, LANES), 1)
    for j in range(RG_CB // LANES):
        sl = slice(j * LANES, (j + 1) * LANES)
        xj = xc_scr[:, sl]
        gates = jnp.dot(xj.astype(BF16), wg_ref[j], preferred_element_type=F32)
        r = _sigmoid(gates[:, :LANES] + ba_ref[:, sl])
        ig = _sigmoid(gates[:, LANES:] + bi_ref[:, sl])
        log_a = -RG_C * r * softplus_neg_lam[:, sl]
        a = jnp.exp(log_a)
        th = jnp.tanh(log_a)
        b = jnp.sqrt(-2.0 * th / (1.0 - th)) * (ig * xj)
        a = a.reshape(n_grp, 8, LANES)
        b = b.reshape(n_grp, 8, LANES)
        for d in (1, 2, 4):
            shift = 8 - d if reverse else d
            ok = (sub < 8 - d) if reverse else (sub >= d)
            a_s, b_s = pltpu.roll(a, shift, 1), pltpu.roll(b, shift, 1)
            b = jnp.where(ok, a * b_s + b, b)
            a = jnp.where(ok, a * a_s, a)
        a_scr[:, sl] = a.reshape(rows, LANES)
        b_scr[:, sl] = b.reshape(rows, LANES)

    @pl.when(t == 0)
    def _():
        carry_scr[...] = jnp.broadcast_to(h0_ref[...], (8, RG_CB))

    def group(i, carry):
        g = (n_grp - 1 - i) if reverse else i
        off = pl.multiple_of(g * 8, 8)
        h = a_scr[pl.ds(off, 8), :] * carry + b_scr[pl.ds(off, 8), :]
        b_scr[pl.ds(off, 8), :] = h
        last = h[0:1] if reverse else h[7:8]
        return jnp.broadcast_to(last, (8, RG_CB))

    carry = lax.fori_loop(0, n_grp, group, carry_scr[...])
    carry_scr[...] = carry
    h_ref[...] = b_scr[...].astype(h_ref.dtype)
    hend_ref[...] = carry[0:1]


def _rglru(z3, conv_w, conv_b, wg, b_a, b_i, lam, h0, *, direction, tile):
    bsz, seq_len, _ = z3.shape
    n_t = seq_len // tile
    reverse = direction == 1
    n_cb = D // RG_CB
    xcol = ZXR // RG_CB
    hpt = tile // RG_HALO
    n_halo = seq_len // RG_HALO

    def tix(t):
        return (n_t - 1 - t) if reverse else t

    vec = lambda: pl.BlockSpec((None, 1, RG_CB), lambda b, c, t: (direction, 0, c))
    return pl.pallas_call(
        functools.partial(_rglru_body, reverse=reverse, n_t=n_t),
        grid=(bsz, n_cb, n_t),
        in_specs=[
            pl.BlockSpec((None, tile, RG_CB), lambda b, c, t: (b, tix(t), xcol + c)),
            pl.BlockSpec((None, RG_HALO, RG_CB),
                         lambda b, c, t: (b, jnp.maximum(tix(t) * hpt - 1, 0), xcol + c)),
            pl.BlockSpec((None, RG_HALO, RG_CB),
                         lambda b, c, t: (b, jnp.minimum((tix(t) + 1) * hpt, n_halo - 1), xcol + c)),
            pl.BlockSpec((4, RG_CB), lambda b, c, t: (0, c)),
            pl.BlockSpec((1, RG_CB), lambda b, c, t: (0, c)),
            pl.BlockSpec((None, RG_CB // LANES, LANES, 2 * LANES), lambda b, c, t: (direction, c, 0, 0)),
            vec(), vec(), vec(),
            pl.BlockSpec((None, 1, RG_CB), lambda b, c, t: (b, 0, c)),
        ],
        out_specs=[pl.BlockSpec((None, tile, RG_CB), lambda b, c, t: (b, tix(t), c)),
                   pl.BlockSpec((None, 1, RG_CB), lambda b, c, t: (b, 0, c))],
        out_shape=[jax.ShapeDtypeStruct((bsz, seq_len, D), BF16),
                   jax.ShapeDtypeStruct((bsz, 1, D), F32)],
        scratch_shapes=[pltpu.VMEM((tile, RG_CB), F32), pltpu.VMEM((tile, RG_CB), F32),
                        pltpu.VMEM((tile, RG_CB), F32), pltpu.VMEM((8, RG_CB), F32)],
        compiler_params=_params(("parallel", "parallel", "arbitrary")),
        name="rglru_bwd" if reverse else "rglru_fwd",
    )(z3, z3, z3, conv_w, conv_b.reshape(1, D), wg, b_a.reshape(2, 1, D), b_i.reshape(2, 1, D),
      lam.reshape(2, 1, D), h0)


def _gate_weights(w_a, w_i):
    def pair(w):
        w = w.reshape(2, 8, 2, 64, 64)
        z = jnp.zeros_like(w[:, :, 0])
        top = jnp.concatenate([w[:, :, 0], z], axis=-1)
        bot = jnp.concatenate([z, w[:, :, 1]], axis=-1)
        return jnp.concatenate([top, bot], axis=-2)
    return jnp.concatenate([pair(w_a), pair(w_i)], axis=-1).astype(BF16)


def _seqdft_body(c_ref, s_ref, xc_ref, xs_ref, o_ref, acc_ref):
    k = pl.program_id(2)

    @pl.when(k == 0)
    def _():
        acc_ref[...] = jnp.zeros_like(acc_ref)

    acc_ref[...] += (jnp.dot(c_ref[...], xc_ref[...], preferred_element_type=F32)
                     + jnp.dot(s_ref[...], xs_ref[...], preferred_element_type=F32))

    @pl.when(k == pl.num_programs(2) - 1)
    def _():
        o_ref[...] = acc_ref[...].astype(o_ref.dtype)


def _seq_dft(z3, cmat, nsmat, tile):
    bsz, seq_len, _ = z3.shape
    nt = seq_len // tile
    return pl.pallas_call(
        _seqdft_body,
        grid=(bsz, nt, nt),
        in_specs=[pl.BlockSpec((tile, tile), lambda b, i, k: (i, k)),
                  pl.BlockSpec((tile, tile), lambda b, i, k: (i, k)),
                  pl.BlockSpec((None, tile, D), lambda b, i, k: (b, k, ZXC // D)),
                  pl.BlockSpec((None, tile, D), lambda b, i, k: (b, k, ZXS // D))],
        out_specs=pl.BlockSpec((None, tile, D), lambda b, i, k: (b, i, 0)),
        out_shape=jax.ShapeDtypeStruct((bsz, seq_len, D), BF16),
        scratch_shapes=[pltpu.VMEM((tile, D), F32)],
        compiler_params=_params(("parallel", "parallel", "arbitrary")),
        name="seq_dft",
    )(cmat, nsmat, z3, z3)


def _dft_mats(n, scale):
    idx = jnp.arange(n, dtype=jnp.int32)
    ang = ((idx[:, None] * idx[None, :]) % n).astype(F32) * (2.0 * math.pi / n)
    return (jnp.cos(ang) * scale), (jnp.sin(ang) * scale)


FFT_L1 = 32
FFT_GRP = 16
FFT_K1B = 8


def _fft1_body(wa_ref, wb_ref, xc_ref, xs_ref, o_ref):
    rows = FFT_L1 * FFT_GRP
    xc = xc_ref[...].reshape(rows, D)
    xs = xs_ref[...].reshape(rows, D)
    a = (jnp.dot(wa_ref[...], xc, preferred_element_type=F32)
         + jnp.dot(wb_ref[...], xs, preferred_element_type=F32))
    a = a.astype(BF16)
    o_ref[:, :, :D] = a[:rows].reshape(FFT_L1, FFT_GRP, D)
    o_ref[:, :, D:] = a[rows:].reshape(FFT_L1, FFT_GRP, D)


def _fft2_body(m_ref, a_ref, o_ref):
    l2 = m_ref.shape[1]
    for j in range(FFT_K1B):
        o_ref[j] = (jnp.dot(m_ref[j, :, :l2], a_ref[j, :, :D], preferred_element_type=F32)
                    + jnp.dot(m_ref[j, :, l2:], a_ref[j, :, D:], preferred_element_type=F32)
                    ).astype(o_ref.dtype)


def _fft_tables(seq_len):
    l1, l2, g = FFT_L1, seq_len // FFT_L1, FFT_GRP
    i1 = jnp.arange(l1, dtype=jnp.int32)
    ang1 = ((i1[:, None] * i1[None, :]) % l1).astype(F32) * (2.0 * math.pi / l1)
    w1r, w1i = jnp.cos(ang1), -jnp.sin(ang1)
    eye = jnp.eye(g, dtype=F32)
    kron = lambda w: jnp.kron(w, eye)
    wa = jnp.concatenate([kron(w1r), kron(w1i)], axis=0).astype(BF16)
    wb = jnp.concatenate([kron(w1i), kron(-w1r)], axis=0).astype(BF16)
    i2 = jnp.arange(l2, dtype=jnp.int32)
    num = (i2[None, :, None] * i2[None, None, :] * l1 + i1[:, None, None] * i2[None, None, :]) % seq_len
    ang2 = num.astype(F32) * (2.0 * math.pi / seq_len)
    scale = seq_len ** -0.5
    m2 = jnp.concatenate([jnp.cos(ang2) * scale, jnp.sin(ang2) * scale], axis=2).astype(BF16)
    return wa, wb, m2


def _seq_fft(z, bsz, seq_len, tabs):
    wa, wb, m2 = tabs
    l1, l2, g = FFT_L1, seq_len // FFT_L1, FFT_GRP
    z4 = z.reshape(bsz, l1, l2, Z_COLS)
    rows = l1 * g
    a4 = pl.pallas_call(
        _fft1_body,
        grid=(bsz, l2 // g),
        in_specs=[_resident((2 * rows, rows)), _resident((2 * rows, rows)),
                  pl.BlockSpec((None, l1, g, D), lambda b, t: (b, 0, t, ZXC // D)),
                  pl.BlockSpec((None, l1, g, D), lambda b, t: (b, 0, t, ZXS // D))],
        out_specs=pl.BlockSpec((None, l1, g, 2 * D), lambda b, t: (b, 0, t, 0)),
        out_shape=jax.ShapeDtypeStruct((bsz, l1, l2, 2 * D), BF16),
        compiler_params=_params(("parallel", "parallel")),
        name="seq_fft_stage1",
    )(wa, wb, z4, z4)
    yp = pl.pallas_call(
        _fft2_body,
        grid=(l1 // FFT_K1B, bsz),
        in_specs=[pl.BlockSpec((FFT_K1B, l2, 2 * l2), lambda k, b: (k, 0, 0)),
                  pl.BlockSpec((None, FFT_K1B, l2, 2 * D), lambda k, b: (b, k, 0, 0))],
        out_specs=pl.BlockSpec((None, FFT_K1B, l2, D), lambda k, b: (b, k, 0, 0)),
        out_shape=jax.ShapeDtypeStruct((bsz, l1, l2, D), BF16),
        compiler_params=_params(("parallel", "parallel")),
        name="seq_fft_stage2",
    )(m2, a4)
    return yp.transpose(0, 2, 1, 3).reshape(bsz, seq_len, D)


def _gelu_tanh(x):
    return 0.5 * x * (1.0 + jnp.tanh(math.sqrt(2.0 / math.pi) * (x + 0.044715 * (x * x * x))))


MERGE_ROW_SPLIT = 2


def _merge_body(x_ref, mod_ref, attn_ref, gr_ref, hf_ref, hb_ref, four_ref,
                woa_ref, wor_ref, wof_ref, wm_ref, bm_ref, wout_ref, lng_ref, lnb_ref, wr_ref, br_ref,
                x1_ref, u2_ref, eid_ref, gate_ref, cnt_ref):
    mod = mod_ref[...]
    rows = x_ref.shape[0] // MERGE_ROW_SPLIT
    for s in range(MERGE_ROW_SPLIT):
        rs = slice(s * rows, (s + 1) * rows)
        x = x_ref[rs, :]
        u = (x * (1.0 + mod[1:2]) + mod[0:1]).astype(BF16)
        rg_in = (_gelu_tanh(gr_ref[rs, :].astype(F32))
                 * (hf_ref[rs, :].astype(F32) + hb_ref[rs, :].astype(F32))).astype(BF16)
        branches = ((attn_ref[rs, :], woa_ref), (rg_in, wor_ref), (four_ref[rs, :], wof_ref))
        merged = None
        for j, (inp, w_ref) in enumerate(branches):
            y = jnp.dot(inp, w_ref[...], preferred_element_type=F32)
            g = _sigmoid(jnp.dot(u, wm_ref[:, j * D:(j + 1) * D], preferred_element_type=F32)
                         + bm_ref[:, j * D:(j + 1) * D])
            merged = g * y if merged is None else merged + g * y
        y = jnp.dot(merged.astype(BF16), wout_ref[...], preferred_element_type=F32)
        x1 = _layer_norm(DEEPNORM_ALPHA * x + mod[2:3] * y, lng_ref[...], lnb_ref[...])
        x1_ref[rs, :] = x1
        u2_ref[rs, :] = (x1 * (1.0 + mod[4:5]) + mod[3:4]).astype(BF16)

    logits = jnp.dot(u2_ref[...], wr_ref[...], preferred_element_type=F32) + br_ref[...]
    lane = lax.broadcasted_iota(jnp.int32, logits.shape, 1)
    vals, idxs = [], []
    for _ in range(TOP_K):
        m = jnp.max(logits, axis=-1, keepdims=True)
        idx = jnp.min(jnp.where(logits == m, lane, LANES), axis=-1, keepdims=True)
        vals.append(m)
        idxs.append(idx)
        logits = jnp.where(lane == idx, -3.0e38, logits)
    exps = [jnp.exp(v - vals[0]) for v in vals]
    den = exps[0] + exps[1] + exps[2] + exps[3]
    eid = jnp.zeros(lane.shape, jnp.int32)
    gate = jnp.zeros(lane.shape, F32)
    member = jnp.zeros(lane.shape, F32)
    for k in range(TOP_K):
        eid = jnp.where(lane == k, idxs[k], eid)
        gate = jnp.where(lane == k, exps[k] / den, gate)
        member = member + jnp.where(lane == idxs[k], 1.0, 0.0)
    eid_ref[...] = eid
    gate_ref[...] = gate
    cnt_ref[...] = jnp.sum(member, axis=0, keepdims=True)


def _merge(x2, mods, attn, z, hf, hb, four, p, seq_len, tm):
    n = x2.shape[0]
    tpb = seq_len // tm
    bm = mods.shape[0]
    mod_map = (lambda i: (i // tpb, 0, 0)) if bm > 1 else (lambda i: (0, 0, 0))
    act = lambda: pl.BlockSpec((tm, D), lambda i: (i, 0))
    return pl.pallas_call(
        _merge_body,
        grid=(n // tm,),
        in_specs=[act(), pl.BlockSpec((None, 6, D), mod_map), act(),
                  pl.BlockSpec((tm, D), lambda i: (i, ZGR // D)), act(), act(), act(),
                  _resident((D, D)), _resident((D, D)), _resident((D, D)),
                  _resident((D, 3 * D)), _resident((1, 3 * D)), _resident((D, D)),
                  _resident((1, D)), _resident((1, D)), _resident((D, LANES)), _resident((1, LANES))],
        out_specs=[act(), act(), pl.BlockSpec((tm, LANES), lambda i: (i, 0)),
                   pl.BlockSpec((tm, LANES), lambda i: (i, 0)),
                   pl.BlockSpec((None, 1, LANES), lambda i: (i, 0, 0))],
        out_shape=[jax.ShapeDtypeStruct((n, D), F32), jax.ShapeDtypeStruct((n, D), BF16),
                   jax.ShapeDtypeStruct((n, LANES), jnp.int32), jax.ShapeDtypeStruct((n, LANES), F32),
                   jax.ShapeDtypeStruct((n // tm, 1, LANES), F32)],
        compiler_params=_params(("parallel",)),
        name="merge_ln_router",
    )(x2, mods, attn, z, hf, hb, four, p["w_o_attn"], p["w_o_rg"], p["w_o_four"], p["w_merge"],
      p["b_merge"], p["w_out"], p["ln1_g"], p["ln1_b"], p["w_router"], p["b_router"])


RUN_ALIGN = 8


def _sorted_rows(td):
    return TOP_K * td + N_EXPERTS * RUN_ALIGN


def _chunk_bits(max_len):
    bits, b = [], RUN_ALIGN
    while b <= max_len:
        bits.append(b)
        b *= 2
    return tuple(reversed(bits))


def _chunked_copies(wait, src_ref, dst_ref, src_off, dst_off, length, bits, sem):
    for bit in bits:
        @pl.when((length & bit) != 0)
        def _():
            off = length & (-2 * bit)
            cp = pltpu.make_async_copy(src_ref.at[pl.ds(pl.multiple_of(src_off + off, RUN_ALIGN), bit)],
                                       dst_ref.at[pl.ds(pl.multiple_of(dst_off + off, RUN_ALIGN), bit)], sem)
            if wait:
                cp.wait()
            else:
                cp.start()


def _slot_positions(eid, toff, ltri):
    lane = lax.broadcasted_iota(jnp.int32, eid.shape, 1)
    hits = [eid[:, k:k + 1] == lane for k in range(TOP_K)]
    member = jnp.zeros(eid.shape, F32)
    for h in hits:
        member = member + jnp.where(h, 1.0, 0.0)
    rank = jnp.dot(ltri, member.astype(BF16), preferred_element_type=F32)
    base = toff + rank
    return [jnp.sum(jnp.where(h, base, 0.0), axis=-1, keepdims=True) for h in hits]


def _dispatch_body(rl_ref, to_ref, ro_ref, tlo_ref, tll_ref, nu_ref, tt_ref, u_ref, eid_ref, toff_ref, ltri_ref,
                   xs_hbm, pos_ref, srt, sems, *, td, tb):
    t = pl.program_id(0)
    n_t = pl.num_programs(0)
    slot = t % 2
    rows = srt.shape[1]
    run_bits = _chunk_bits(td)

    def run_copies(wait, tile, s):
        def one(e, c):
            i = tile * N_EXPERTS + e
            _chunked_copies(wait, srt.at[s], xs_hbm, to_ref[i], ro_ref[i], rl_ref[i], run_bits, sems.at[s])
            return c
        lax.fori_loop(0, N_EXPERTS, one, 0)

    @pl.when(t == 0)
    def _():
        srt[1, :tb, :] = jnp.zeros((tb, D), F32)
        tail_bits = _chunk_bits(tb)
        for wait in (False, True):
            def one(e, c):
                _chunked_copies(wait, srt.at[1], xs_hbm, 0, tlo_ref[e], tll_ref[e], tail_bits, sems.at[1])
                return c
            lax.fori_loop(0, N_EXPERTS, one, 0)
        n_blocks = xs_hbm.shape[0] // tb
        for wait in (False, True):
            def blk(j, c):
                cp = pltpu.make_async_copy(srt.at[1, pl.ds(0, tb)],
                                           xs_hbm.at[pl.ds(pl.multiple_of(j * tb, RUN_ALIGN), tb)], sems.at[1])
                if wait:
                    cp.wait()
                else:
                    cp.start()
                return c
            lax.fori_loop(nu_ref[0], n_blocks, blk, 0)

    pos = _slot_positions(eid_ref[...], toff_ref[...], ltri_ref[...])
    lane = lax.broadcasted_iota(jnp.int32, (td, LANES), 1)
    pos = [p.astype(jnp.int32) for p in pos]
    pos_out = jnp.zeros((td, LANES), jnp.int32)
    for k in range(TOP_K):
        pos_out = jnp.where(lane == k, pos[k], pos_out)
    pos_ref[...] = pos_out

    col = lax.broadcasted_iota(jnp.int32, (td, rows), 1)
    hit = col == pos[0]
    for k in range(1, TOP_K):
        hit = hit | (col == pos[k])
    onehot_t = jnp.where(hit, 1.0, 0.0).astype(BF16)
    srt[slot] = lax.dot_general(onehot_t, u_ref[...], (((0,), (0,)), ((), ())),
                                preferred_element_type=F32)
    run_copies(False, t, slot)

    def wait_runs(tile, s):
        _chunked_copies(True, srt.at[s], xs_hbm, 0, 0, tt_ref[tile], _chunk_bits(rows), sems.at[s])

    @pl.when(t >= 1)
    def _():
        wait_runs(t - 1, 1 - slot)

    @pl.when(t == n_t - 1)
    def _():
        wait_runs(t, slot)


def _dispatch(u2, eid128, rt, td, tb):
    n_tok = u2.shape[0]
    n_t = n_tok // td
    rows = _sorted_rows(td)
    ltri = (jnp.arange(td)[:, None] > jnp.arange(td)[None, :]).astype(BF16)
    grid_spec = pltpu.PrefetchScalarGridSpec(
        num_scalar_prefetch=7,
        grid=(n_t,),
        in_specs=[pl.BlockSpec((td, D), lambda t, *_: (t, 0)),
                  pl.BlockSpec((td, LANES), lambda t, *_: (t, 0)),
                  pl.BlockSpec((None, 1, LANES), lambda t, *_: (t, 0, 0)),
                  pl.BlockSpec((td, td), lambda t, *_: (0, 0), pipeline_mode=pl.Buffered(1))],
        out_specs=[pl.BlockSpec(memory_space=pl.ANY),
                   pl.BlockSpec((td, LANES), lambda t, *_: (t, 0))],
        scratch_shapes=[pltpu.VMEM((2, rows, D), F32), pltpu.SemaphoreType.DMA((2,))],
    )
    return pl.pallas_call(
        functools.partial(_dispatch_body, td=td, tb=tb),
        grid_spec=grid_spec,
        out_shape=[jax.ShapeDtypeStruct((rt["n_slots"], D), F32),
                   jax.ShapeDtypeStruct((n_tok, LANES), jnp.int32)],
        compiler_params=_params(("arbitrary",)),
        name="moe_dispatch",
    )(rt["run_len"], rt["tile_off"], rt["run_off"], rt["tail_off"], rt["tail_len"], rt["n_used"], rt["tile_tot"],
      u2, eid128, rt["toff_f"], ltri)


MOE_ROW_SPLIT = 2


def _moe_body(be_ref, nused_ref, nb_ref, xs_ref, wgu_hbm, bgu_ref, wdn_hbm, bdn_ref, ys_ref,
              wgu_st, wdn_st, wgu_bf, wdn_bf, sems, *, layer):
    i = pl.program_id(0)
    n_used = nused_ref[0]

    def fetch(e):
        return (pltpu.make_async_copy(wgu_hbm.at[layer, e], wgu_st, sems.at[0]),
                pltpu.make_async_copy(wdn_hbm.at[layer, e], wdn_st, sems.at[1]))

    @pl.when(i == 0)
    def _():
        for cp in fetch(be_ref[0]):
            cp.start()

    @pl.when((i < n_used) & ((i == 0) | (be_ref[i] != be_ref[jnp.maximum(i - 1, 0)])))
    def _():
        e = be_ref[i]
        for cp in fetch(e):
            cp.wait()
        wgu_bf[...] = wgu_st[...].astype(BF16)
        wdn_bf[...] = wdn_st[...].astype(BF16)
        nxt = i + nb_ref[e]

        @pl.when(nxt < n_used)
        def _():
            for cp in fetch(be_ref[nxt]):
                cp.start()

    @pl.when(i < n_used)
    def _():
        rows = xs_ref.shape[0] // MOE_ROW_SPLIT
        for s in range(MOE_ROW_SPLIT):
            rs = slice(s * rows, (s + 1) * rows)
            x = xs_ref[rs, :].astype(BF16)
            h = jnp.dot(x, wgu_bf[...], preferred_element_type=F32) + bgu_ref[...]
            half = h.shape[1] // 2
            x_glu = jnp.minimum(h[:, :half], SWIGLU_LIMIT)
            x_lin = jnp.clip(h[:, half:], -SWIGLU_LIMIT, SWIGLU_LIMIT)
            act = (x_glu * _sigmoid(SWIGLU_ALPHA * x_glu) * (x_lin + 1.0)).astype(BF16)
            ys_ref[rs, :] = jnp.dot(act, wdn_bf[...], preferred_element_type=F32) + bdn_ref[...]

    @pl.when(i >= n_used)
    def _():
        ys_ref[...] = jnp.zeros_like(ys_ref)


def _moe(xs, rt, layer, w_gu, b_gu, w_dn, b_dn, tb):
    n_blocks = xs.shape[0] // tb
    d_ff2 = w_gu.shape[3]
    live = lambda i, nu: jnp.minimum(i, nu[0] - 1)
    grid_spec = pltpu.PrefetchScalarGridSpec(
        num_scalar_prefetch=3,
        grid=(n_blocks,),
        in_specs=[pl.BlockSpec((tb, D), lambda i, be, nu, nb: (live(i, nu), 0)),
                  pl.BlockSpec(memory_space=pl.ANY),
                  pl.BlockSpec((None, None, 1, d_ff2), lambda i, be, nu, nb: (layer, be[live(i, nu)], 0, 0)),
                  pl.BlockSpec(memory_space=pl.ANY),
                  pl.BlockSpec((None, None, 1, D), lambda i, be, nu, nb: (layer, be[live(i, nu)], 0, 0))],
        out_specs=pl.BlockSpec((tb, D), lambda i, be, nu, nb: (i, 0)),
        scratch_shapes=[pltpu.VMEM((D, d_ff2), F32), pltpu.VMEM((d_ff2 // 2, D), F32),
                        pltpu.VMEM((D, d_ff2), BF16), pltpu.VMEM((d_ff2 // 2, D), BF16),
                        pltpu.SemaphoreType.DMA((2,))],
    )
    return pl.pallas_call(
        functools.partial(_moe_body, layer=layer),
        grid_spec=grid_spec,
        out_shape=jax.ShapeDtypeStruct(xs.shape, F32),
        compiler_params=_params(("arbitrary",)),
        name="moe_experts",
    )(rt["block_e"], rt["n_used"], rt["blocks_per_e"], xs, w_gu, b_gu, w_dn, b_dn)


def _route(cnt128, n_tok, td, tb):
    n_t = n_tok // td
    cnt = cnt128[:, 0, :N_EXPERTS].astype(jnp.int32)
    run = (cnt + RUN_ALIGN - 1) // RUN_ALIGN * RUN_ALIGN
    tile_off = jnp.cumsum(run, axis=1) - run
    total = run.sum(axis=0)
    padded = (total + tb - 1) // tb * tb
    pend = jnp.cumsum(padded)
    pstart = pend - padded
    run_off = pstart[None, :] + jnp.cumsum(run, axis=0) - run
    worst = TOP_K * n_tok + n_t * N_EXPERTS * RUN_ALIGN + N_EXPERTS * tb
    n_slots = (worst + tb - 1) // tb * tb
    first_slot = jnp.arange(n_slots // tb, dtype=jnp.int32) * tb
    block_e = jnp.minimum(jnp.sum((pend[None, :] <= first_slot[:, None]).astype(jnp.int32), axis=1),
                          N_EXPERTS - 1)
    toff_f = jnp.zeros((n_t, 1, LANES), F32).at[:, 0, :N_EXPERTS].set(tile_off.astype(F32))
    i32 = lambda a: a.astype(jnp.int32).reshape(-1)
    return dict(run_len=i32(run), tile_off=i32(tile_off), run_off=i32(run_off),
                tail_off=i32(pstart + total), tail_len=i32(padded - total), toff_f=toff_f,
                tile_tot=i32(run.sum(axis=1)), blocks_per_e=i32(padded // tb),
                block_e=i32(block_e), n_used=i32(pend[-1] // tb), n_slots=n_slots)


def _combine_body(rl_ref, to_ref, ro_ref, tt_ref, ys_hbm, x1_ref, mod_ref, gate_ref, pos_ref, lng_ref, lnb_ref,
                  o_ref, srt, sems, *, td):
    t = pl.program_id(0)
    n_t = pl.num_programs(0)
    slot = t % 2
    rows = srt.shape[1]
    run_bits = _chunk_bits(td)

    def run_copies(wait, tile, s):
        def one(e, c):
            i = tile * N_EXPERTS + e
            _chunked_copies(wait, ys_hbm, srt.at[s], ro_ref[i], to_ref[i], rl_ref[i], run_bits, sems.at[s])
            return c
        lax.fori_loop(0, N_EXPERTS, one, 0)

    @pl.when(t == 0)
    def _():
        srt[...] = jnp.zeros_like(srt)
        run_copies(False, 0, 0)

    @pl.when(t + 1 < n_t)
    def _():
        run_copies(False, t + 1, 1 - slot)

    _chunked_copies(True, ys_hbm, srt.at[slot], 0, 0, tt_ref[t], _chunk_bits(rows), sems.at[slot])

    gate = gate_ref[...]
    pos = pos_ref[...]
    col = lax.broadcasted_iota(jnp.int32, (td, rows), 1)
    weights = jnp.zeros((td, rows), F32)
    for k in range(TOP_K):
        weights = jnp.where(col == pos[:, k:k + 1], gate[:, k:k + 1], weights)
    f = jnp.dot(weights.astype(BF16), srt[slot].astype(BF16), preferred_element_type=F32)
    mod = mod_ref[...]
    o_ref[...] = _layer_norm(DEEPNORM_ALPHA * x1_ref[...] + mod[5:6] * f, lng_ref[...], lnb_ref[...])


def _combine(x1, mods, gate128, pos128, ys, rt, ln_g, ln_b, seq_len, td):
    n = x1.shape[0]
    tpb = seq_len // td
    bm = mods.shape[0]
    mod_map = (lambda i, *_: (i // tpb, 0, 0)) if bm > 1 else (lambda i, *_: (0, 0, 0))
    grid_spec = pltpu.PrefetchScalarGridSpec(
        num_scalar_prefetch=4,
        grid=(n // td,),
        in_specs=[pl.BlockSpec(memory_space=pl.ANY),
                  pl.BlockSpec((td, D), lambda i, *_: (i, 0)),
                  pl.BlockSpec((None, 6, D), mod_map),
                  pl.BlockSpec((td, LANES), lambda i, *_: (i, 0)),
                  pl.BlockSpec((td, LANES), lambda i, *_: (i, 0)),
                  pl.BlockSpec((1, D), lambda i, *_: (0, 0)),
                  pl.BlockSpec((1, D), lambda i, *_: (0, 0))],
        out_specs=pl.BlockSpec((td, D), lambda i, *_: (i, 0)),
        scratch_shapes=[pltpu.VMEM((2, _sorted_rows(td), D), F32), pltpu.SemaphoreType.DMA((2,))],
    )
    return pl.pallas_call(
        functools.partial(_combine_body, td=td),
        grid_spec=grid_spec,
        out_shape=jax.ShapeDtypeStruct((n, D), F32),
        compiler_params=_params(("arbitrary",)),
        name="combine_ln",
    )(rt["run_len"], rt["tile_off"], rt["run_off"], rt["tile_tot"], ys, x1, mods, gate128, pos128,
      ln_g, ln_b)


def _rope_tables(seq_len):
    pos = jnp.arange(seq_len, dtype=jnp.int32)
    row = (pos // GRID_W).astype(F32)
    col = (pos % GRID_W).astype(F32)
    n_freq = HEAD_DIM // 4
    freqs = ROPE_BASE ** (-jnp.arange(n_freq, dtype=F32) / n_freq)
    ar, ac = row[:, None] * freqs, col[:, None] * freqs
    zero = jnp.zeros_like(ar)
    c = jnp.concatenate([jnp.cos(ar), jnp.cos(ar), jnp.cos(ac), jnp.cos(ac)], axis=1)
    sa = jnp.concatenate([-jnp.sin(ar), zero, -jnp.sin(ac), zero], axis=1)
    sb = jnp.concatenate([zero, jnp.sin(ar), zero, jnp.sin(ac)], axis=1)
    rep = MXU_N // HEAD_DIM
    return tuple(jnp.tile(t, (1, rep)) for t in (c, sa, sb))


def _layer_params(l, w):
    w_router = jnp.zeros((D, LANES), BF16).at[:, :N_EXPERTS].set(w["w_router"][l].astype(BF16))
    b_router = jnp.full((1, LANES), NEG_INF, F32).at[0, :N_EXPERTS].set(w["b_router"][l])
    return dict(
        w_in=w["w_in"][l].astype(BF16),
        w_o_attn=w["w_o_attn"][l].astype(BF16), w_o_rg=w["w_o_rg"][l].astype(BF16),
        w_o_four=w["w_o_four"][l].astype(BF16), w_merge=w["w_merge"][l].astype(BF16),
        b_merge=w["b_merge"][l].reshape(1, 3 * D), w_out=w["w_out"][l].astype(BF16),
        ln1_g=w["ln1_g"][l].reshape(1, D), ln1_b=w["ln1_b"][l].reshape(1, D),
        ln2_g=w["ln2_g"][l].reshape(1, D), ln2_b=w["ln2_b"][l].reshape(1, D),
        w_router=w_router, b_router=b_router,
        layer=l, w_gu=w["w_gate_up"], b_gu=w["b_gate_up"].reshape(DEPTH, N_EXPERTS, 1, -1),
        w_dn=w["w_down"], b_dn=w["b_down"].reshape(DEPTH, N_EXPERTS, 1, D),
        wg=_gate_weights(w["rg_w_a"][l], w["rg_w_i"][l]),
        conv_w=w["conv_w"][l], conv_b=w["conv_b"][l], b_a=w["rg_b_a"][l], b_i=w["rg_b_i"][l],
        lam=w["rg_lambda"][l], sink=w["attn_sink"][l],
    )


def _row_tile(seq_len, want):
    return min(want, seq_len)


def _ffn(x1, u2, eid128, gate128, cnt128, mods, p, seq_len, tb, td):
    rt = _route(cnt128, x1.shape[0], td, tb)
    xs, pos128 = _dispatch(u2, eid128, rt, td, tb)
    ys = _moe(xs, rt, p["layer"], p["w_gu"], p["b_gu"], p["w_dn"], p["b_dn"], tb)
    return _combine(x1, mods, gate128, pos128, ys, rt, p["ln2_g"], p["ln2_b"], seq_len, td)


def kernel(x, c, ctx, c_ctx, w_mod, b_mod, w_in, attn_sink, w_o_attn, conv_w, conv_b, rg_w_a, rg_b_a, rg_w_i, rg_b_i, rg_lambda, w_o_rg, w_o_four, w_merge, b_merge, w_out, ln1_g, ln1_b, w_router, b_router, w_gate_up, b_gate_up, w_down, b_down, ln2_g, ln2_b):
    w = dict(w_mod=w_mod, b_mod=b_mod, w_in=w_in, attn_sink=attn_sink, w_o_attn=w_o_attn, conv_w=conv_w,
             conv_b=conv_b, rg_w_a=rg_w_a, rg_b_a=rg_b_a, rg_w_i=rg_w_i, rg_b_i=rg_b_i, rg_lambda=rg_lambda,
             w_o_rg=w_o_rg, w_o_four=w_o_four, w_merge=w_merge, b_merge=b_merge, w_out=w_out, ln1_g=ln1_g,
             ln1_b=ln1_b, w_router=w_router, b_router=b_router, w_gate_up=w_gate_up, b_gate_up=b_gate_up,
             w_down=w_down, b_down=b_down, ln2_g=ln2_g, ln2_b=ln2_b)
    bsz, seq_len, _ = x.shape
    ctx_len = ctx.shape[1]
    n_lat, n_ctx = bsz * seq_len, bsz * ctx_len
    tm_lat, tm_ctx = _row_tile(seq_len, 512), _row_tile(ctx_len, 256)
    dft_lat, dft_ctx = _row_tile(seq_len, 1024), _row_tile(ctx_len, 256)
    rg_lat, rg_ctx = _row_tile(seq_len, 512), _row_tile(ctx_len, 256)
    tb_lat, tb_ctx = 512, 128

    rope = _rope_tables(seq_len)
    cw, sw = _dft_mats(F_GW, F_GW ** -0.5)
    csw = jnp.concatenate([cw, sw], axis=1).astype(BF16)
    use_fft = seq_len % (FFT_L1 * FFT_GRP) == 0
    if use_fft:
        fft_tabs = _fft_tables(seq_len)
    else:
        cl, sl = _dft_mats(seq_len, seq_len ** -0.5)
        cl, nsl = cl.astype(BF16), (-sl).astype(BF16)
    cc, sc = _dft_mats(ctx_len, ctx_len ** -0.5)
    cc, nsc = cc.astype(BF16), (-sc).astype(BF16)

    cond = jnp.zeros((16, D), F32).at[:bsz].set(c).at[bsz].set(c_ctx)
    x2 = x.reshape(n_lat, D)
    ctx2 = ctx.reshape(n_ctx, D)
    zero_h = jnp.zeros((bsz, 1, D), F32)

    for l in range(DEPTH):
        last = l == DEPTH - 1
        p = _layer_params(l, w)
        m = _adaln(cond, w_mod[l], b_mod[l]).reshape(16, 6, D)
        mods_lat, mods_ctx = m[:bsz], m[bsz:bsz + 1]

        zc = _proj(ctx2, mods_ctx, p["w_in"], csw, None, ctx_len, tm_ctx)
        z = _proj(x2, mods_lat, p["w_in"], csw, rope, seq_len, tm_lat)
        zc3 = zc.reshape(bsz, ctx_len, Z_COLS)
        z3 = z.reshape(bsz, seq_len, Z_COLS)

        rg = lambda zz, h0, d, tile: _rglru(zz, p["conv_w"], p["conv_b"], p["wg"], p["b_a"], p["b_i"],
                                            p["lam"], h0, direction=d, tile=tile)
        hcf, endf = rg(zc3, zero_h, 0, rg_ctx)
        hcb, endb = rg(zc3, zero_h, 1, rg_ctx)
        hf, _ = rg(z3, endf, 0, rg_lat)
        hb, _ = rg(z3, endb, 1, rg_lat)

        attn = _attention(z3, zc3, p["sink"])
        four = _seq_fft(z, bsz, seq_len, fft_tabs) if use_fft else _seq_dft(z3, cl, nsl, dft_lat)
        x1, u2, eid, gate, cnt = _merge(x2, mods_lat, attn.reshape(n_lat, D), z, hf.reshape(n_lat, D),
                                   hb.reshape(n_lat, D), four.reshape(n_lat, D), p, seq_len, tm_lat)
        x2 = _ffn(x1, u2, eid, gate, cnt, mods_lat, p, seq_len, tb_lat, tm_lat)

        if not last:
            attn_c = _ctx_attention(zc3, p["sink"])
            four_c = _seq_dft(zc3, cc, nsc, dft_ctx)
            c1, uc2, eid_c, gate_c, cnt_c = _merge(ctx2, mods_ctx, attn_c.reshape(n_ctx, D), zc,
                                            hcf.reshape(n_ctx, D), hcb.reshape(n_ctx, D),
                                            four_c.reshape(n_ctx, D), p, ctx_len, tm_ctx)
            ctx2 = _ffn(c1, uc2, eid_c, gate_c, cnt_c, mods_ctx, p, ctx_len, tb_ctx, tm_ctx)

    return x2.reshape(bsz, seq_len, D)
```

```python
import functools
import math

import jax
import jax.numpy as jnp
from jax import lax
from jax.experimental import pallas as pl
from jax.experimental.pallas import tpu as pltpu

F32 = jnp.float32
BF16 = jnp.bfloat16

D = 1024
HEAD_DIM = 64
N_HEADS = 16
N_KV = 4
Q_PER_KV = 4
WINDOW = 128
QBLK = 128
assert WINDOW == QBLK
GRID_W = 64
ROPE_BASE = 10000.0
RG_C = 8.0
CONV_LEFT = 2
F_GROUPS = 4
F_GW = 256
N_EXPERTS = 32
TOP_K = 4
SWIGLU_LIMIT = 7.0
SWIGLU_ALPHA = 1.702
LN_EPS = 1e-5
DEPTH = 2
DEEPNORM_ALPHA = (2 * DEPTH) ** 0.25
NEG_INF = -1e30

Q_OFF, K_OFF, V_OFF, XR_OFF, GR_OFF, XF_OFF, IN_COLS = 0, 1024, 1280, 1536, 2560, 3584, 4608
ZQ, ZXR, ZGR, ZXC, ZXS, ZK, ZV, Z_COLS = 0, 1024, 2048, 3072, 4096, 5120, 5376, 5632

VMEM_LIMIT_V7X = 56 * 1024 * 1024
LANES = 128
MXU_N = 256


def _params(sem, vmem=VMEM_LIMIT_V7X):
    return pltpu.CompilerParams(dimension_semantics=sem, vmem_limit_bytes=vmem)


def _resident(shape):
    nd = len(shape)
    return pl.BlockSpec(shape, lambda *_: (0,) * nd, pipeline_mode=pl.Buffered(1))


def _sigmoid(x):
    return 1.0 / (1.0 + jnp.exp(-x))


def _layer_norm(r, g, b):
    mu = jnp.mean(r, axis=-1, keepdims=True)
    rc = r - mu
    var = jnp.mean(rc * rc, axis=-1, keepdims=True)
    return rc * lax.rsqrt(var + LN_EPS) * g + b


def _adaln_body(c_ref, w_ref, b_ref, o_ref):
    c = c_ref[...]
    s = (c * _sigmoid(c)).astype(BF16)
    o_ref[...] = jnp.dot(s, w_ref[...].astype(BF16), preferred_element_type=F32) + b_ref[...]


def _adaln(cond, w_mod, b_mod):
    rows, n = cond.shape[0], w_mod.shape[1]
    tn = 1024
    return pl.pallas_call(
        _adaln_body,
        grid=(n // tn,),
        in_specs=[pl.BlockSpec((rows, D), lambda j: (0, 0)),
                  pl.BlockSpec((D, tn), lambda j: (0, j)),
                  pl.BlockSpec((1, tn), lambda j: (0, j))],
        out_specs=pl.BlockSpec((rows, tn), lambda j: (0, j)),
        out_shape=jax.ShapeDtypeStruct((rows, n), F32),
        compiler_params=_params(("arbitrary",)),
        name="adaln",
    )(cond, w_mod, b_mod.reshape(1, n))


def _proj_dst(j):
    if j < 4:
        return ZQ + j * MXU_N
    if j == 4:
        return ZK
    if j == 5:
        return ZV
    if j < 10:
        return ZXR + (j - 6) * MXU_N
    if j < 14:
        return ZGR + (j - 10) * MXU_N
    return None


def _proj_body(x_ref, mod_ref, w_ref, csw_ref, *rest, rope):
    if rope:
        c_ref, sa_ref, sb_ref, o_ref = rest
    else:
        (o_ref,) = rest
    mod = mod_ref[...]
    u = (x_ref[...] * (1.0 + mod[1:2]) + mod[0:1]).astype(BF16)
    for j in range(IN_COLS // MXU_N):
        acc = jnp.dot(u, w_ref[:, j * MXU_N:(j + 1) * MXU_N], preferred_element_type=F32)
        if rope and j < 5:
            acc = (acc * c_ref[...] + pltpu.roll(acc, MXU_N - 16, 1) * sa_ref[...]
                   + pltpu.roll(acc, 16, 1) * sb_ref[...])
        dst = _proj_dst(j)
        if dst is not None:
            o_ref[:, dst:dst + MXU_N] = acc.astype(BF16)
        else:
            g = j - 14
            t = jnp.dot(acc.astype(BF16), csw_ref[...], preferred_element_type=F32)
            o_ref[:, ZXC + g * F_GW:ZXC + (g + 1) * F_GW] = t[:, :F_GW].astype(BF16)
            o_ref[:, ZXS + g * F_GW:ZXS + (g + 1) * F_GW] = t[:, F_GW:].astype(BF16)


def _proj(x2, mods, w_in, csw, rope_tabs, seq_len, tm):
    n = x2.shape[0]
    tpb = seq_len // tm
    bm = mods.shape[0]
    mod_map = (lambda i: (i // tpb, 0, 0)) if bm > 1 else (lambda i: (0, 0, 0))
    in_specs = [pl.BlockSpec((tm, D), lambda i: (i, 0)),
                pl.BlockSpec((None, 6, D), mod_map),
                _resident((D, IN_COLS)),
                _resident((F_GW, 2 * F_GW))]
    args = [x2, mods, w_in, csw]
    if rope_tabs is not None:
        in_specs += [pl.BlockSpec((tm, MXU_N), lambda i: (i % tpb, 0))] * 3
        args += list(rope_tabs)
    return pl.pallas_call(
        functools.partial(_proj_body, rope=rope_tabs is not None),
        grid=(n // tm,),
        in_specs=in_specs,
        out_specs=pl.BlockSpec((tm, Z_COLS), lambda i: (i, 0)),
        out_shape=jax.ShapeDtypeStruct((n, Z_COLS), BF16),
        compiler_params=_params(("parallel",)),
        name="proj_rope" if rope_tabs is not None else "proj_ctx",
    )(*args)


def _group_attention(sink_ref, q_ref, o_ref, g, parts):
    rows = q_ref.shape[0]
    key_low = g % 2 == 0
    lane = lax.broadcasted_iota(jnp.int32, (1, LANES), 1)
    keep = (lane < HEAD_DIM) if key_low else (lane >= HEAD_DIM)
    kv_lanes = slice((g // 2) * LANES, (g // 2 + 1) * LANES)

    blocks = []
    for j in range(2):
        q32 = q_ref[:, (2 * g + j) * LANES:(2 * g + j + 1) * LANES].astype(F32) * (HEAD_DIM ** -0.5)
        same, rot = q32.astype(BF16), pltpu.roll(q32, HEAD_DIM, 1).astype(BF16)
        blocks += [same, rot] if key_low else [rot, same]
    qs = jnp.concatenate(blocks, axis=0)

    row = lax.broadcasted_iota(jnp.int32, (Q_PER_KV * rows, 1), 0)
    sink = jnp.full((Q_PER_KV * rows, 1), sink_ref[g * Q_PER_KV + Q_PER_KV - 1], F32)
    for r in range(Q_PER_KV - 2, -1, -1):
        sink = jnp.where(row < (r + 1) * rows, sink_ref[g * Q_PER_KV + r], sink)

    def lane_blocks(a):
        return [a[:, c * LANES:(c + 1) * LANES] for c in range(a.shape[1] // LANES)]

    scores = []
    m_fold = None
    for k_ref, _, bias in parts:
        k2 = jnp.where(keep, k_ref[:, kv_lanes], jnp.zeros((), BF16))
        s = lax.dot_general(qs, k2, (((1,), (1,)), ((), ())), preferred_element_type=F32)
        if bias is not None:
            s = s + bias
        scores.append(s)
        for blk in lane_blocks(s):
            m_fold = blk if m_fold is None else jnp.maximum(m_fold, blk)
    m = jnp.maximum(sink, jnp.max(m_fold, axis=-1, keepdims=True))
    p_fold = None
    out = None
    for s, (_, v_ref, _) in zip(scores, parts):
        p = jnp.exp(s - m)
        for blk in lane_blocks(p):
            p_fold = blk if p_fold is None else p_fold + blk
        v2 = jnp.where(keep, v_ref[:, kv_lanes], jnp.zeros((), BF16))
        o = jnp.dot(p.astype(BF16), v2, preferred_element_type=F32)
        out = o if out is None else out + o
    den = jnp.exp(sink - m) + jnp.sum(p_fold, axis=-1, keepdims=True)
    out = out / den
    for j in range(2):
        first, second = out[2 * j * rows:(2 * j + 1) * rows], out[(2 * j + 1) * rows:(2 * j + 2) * rows]
        if key_low:
            both = first + pltpu.roll(second, HEAD_DIM, 1)
        else:
            both = pltpu.roll(first, HEAD_DIM, 1) + second
        o_ref[:, (2 * g + j) * LANES:(2 * g + j + 1) * LANES] = both.astype(o_ref.dtype)


def _attn_body(sink_ref, q_ref, kl_ref, km_ref, kr_ref, vl_ref, vm_ref, vr_ref, kc_ref, vc_ref,
               o_ref, *, n_blk):
    n = pl.program_id(1)
    i = lax.broadcasted_iota(jnp.int32, (Q_PER_KV * QBLK, QBLK), 0) % QBLK
    j = lax.broadcasted_iota(jnp.int32, (Q_PER_KV * QBLK, QBLK), 1)
    bias_prev = jnp.where((j >= i) & (n > 0), 0.0, NEG_INF).astype(F32)
    bias_next = jnp.where((j <= i) & (n < n_blk - 1), 0.0, NEG_INF).astype(F32)
    parts = ((kl_ref, vl_ref, bias_prev), (km_ref, vm_ref, None), (kr_ref, vr_ref, bias_next),
             (kc_ref, vc_ref, None))
    for g in range(N_KV):
        _group_attention(sink_ref, q_ref, o_ref, g, parts)


def _attention(z3, zc3, sink):
    bsz, seq_len, _ = z3.shape
    ctx_len = zc3.shape[1]
    nb = seq_len // QBLK
    kcol, vcol = ZK // MXU_N, ZV // MXU_N

    def blk(col, off):
        return pl.BlockSpec((None, QBLK, MXU_N),
                            lambda b, n: (b, jnp.clip(n + off, 0, nb - 1), col))

    return pl.pallas_call(
        functools.partial(_attn_body, n_blk=nb),
        grid=(bsz, nb),
        in_specs=[pl.BlockSpec(memory_space=pltpu.SMEM),
                  pl.BlockSpec((None, QBLK, D), lambda b, n: (b, n, 0)),
                  blk(kcol, -1), blk(kcol, 0), blk(kcol, 1),
                  blk(vcol, -1), blk(vcol, 0), blk(vcol, 1),
                  pl.BlockSpec((None, ctx_len, MXU_N), lambda b, n: (b, 0, kcol)),
                  pl.BlockSpec((None, ctx_len, MXU_N), lambda b, n: (b, 0, vcol))],
        out_specs=pl.BlockSpec((None, QBLK, D), lambda b, n: (b, n, 0)),
        out_shape=jax.ShapeDtypeStruct((bsz, seq_len, D), BF16),
        compiler_params=_params(("parallel", "parallel")),
        name="window_attn",
    )(sink, z3, z3, z3, z3, z3, z3, z3, zc3, zc3)


def _ctx_attn_body(sink_ref, q_ref, kc_ref, vc_ref, o_ref):
    for g in range(N_KV):
        _group_attention(sink_ref, q_ref, o_ref, g, ((kc_ref, vc_ref, None),))


def _ctx_attention(zc3, sink):
    bsz, ctx_len, _ = zc3.shape
    kcol, vcol = ZK // MXU_N, ZV // MXU_N
    return pl.pallas_call(
        _ctx_attn_body,
        grid=(bsz,),
        in_specs=[pl.BlockSpec(memory_space=pltpu.SMEM),
                  pl.BlockSpec((None, ctx_len, D), lambda b: (b, 0, 0)),
                  pl.BlockSpec((None, ctx_len, MXU_N), lambda b: (b, 0, kcol)),
                  pl.BlockSpec((None, ctx_len, MXU_N), lambda b: (b, 0, vcol))],
        out_specs=pl.BlockSpec((None, ctx_len, D), lambda b: (b, 0, 0)),
        out_shape=jax.ShapeDtypeStruct((bsz, ctx_len, D), BF16),
        compiler_params=_params(("parallel",)),
        name="ctx_attn",
    )(sink, zc3, zc3, zc3)


RG_CB = 1024
RG_HALO = 16


def _rglru_body(x_ref, xp_ref, xn_ref, cw_ref, cb_ref, wg_ref, ba_ref, bi_ref, lam_ref, h0_ref,
                h_ref, hend_ref, a_scr, b_scr, xc_scr, carry_scr, *, reverse, n_t):
    t = pl.program_id(2)
    t_idx = (n_t - 1 - t) if reverse else t
    rows = x_ref.shape[0]
    n_grp = rows // 8
    x = x_ref[...].astype(F32)

    cw = cw_ref[...]

    def taps(xm2, xm1, x0, xp1):
        return cb_ref[...] + xm2 * cw[0:1] + xm1 * cw[1:2] + x0 * cw[2:3] + xp1 * cw[3:4]

    x_m1, x_m2, x_p1 = pltpu.roll(x, 1, 0), pltpu.roll(x, 2, 0), pltpu.roll(x, rows - 1, 0)
    xc_scr[...] = taps(x_m2, x_m1, x, x_p1)
    prev = xp_ref[...].astype(F32) * jnp.where(t_idx > 0, 1.0, 0.0)
    nxt = xn_ref[...].astype(F32) * jnp.where(t_idx < n_t - 1, 1.0, 0.0)
    p2, p1, n0 = prev[RG_HALO - 2:RG_HALO - 1], prev[RG_HALO - 1:RG_HALO], nxt[0:1]
    r8 = lax.broadcasted_iota(jnp.int32, (8, RG_CB), 0)
    head, tail = x[0:8], x[rows - 8:rows]
    h_m1 = jnp.where(r8 == 0, p1, pltpu.roll(head, 1, 0))
    h_m2 = jnp.where(r8 == 0, p2, jnp.where(r8 == 1, p1, pltpu.roll(head, 2, 0)))
    t_p1 = jnp.where(r8 == 7, n0, pltpu.roll(tail, 7, 0))
    xc_scr[0:8, :] = taps(h_m2, h_m1, head, x_p1[0:8])
    xc_scr[rows - 8:rows, :] = taps(x_m2[rows - 8:rows], x_m1[rows - 8:rows], tail, t_p1)

    lam = lam_ref[...]
    softplus_neg_lam = jnp.maximum(-lam, 0.0) + jnp.log1p(jnp.exp(-jnp.abs(lam)))
    sub = lax.broadcasted_iota(jnp.int32, (n_grp, 8, LANES), 1)
    for j in range(RG_CB // LANES):
        sl = slice(j * LANES, (j + 1) * LANES)
        xj = xc_scr[:, sl]
        gates = jnp.dot(xj.astype(BF16), wg_ref[j], preferred_element_type=F32)
        r = _sigmoid(gates[:, :LANES] + ba_ref[:, sl])
        ig = _sigmoid(gates[:, LANES:] + bi_ref[:, sl])
        log_a = -RG_C * r * softplus_neg_lam[:, sl]
        a = jnp.exp(log_a)
        th = jnp.tanh(log_a)
        b = jnp.sqrt(-2.0 * th / (1.0 - th)) * (ig * xj)
        a = a.reshape(n_grp, 8, LANES)
        b = b.reshape(n_grp, 8, LANES)
        for d in (1, 2, 4):
            shift = 8 - d if reverse else d
            ok = (sub < 8 - d) if reverse else (sub >= d)
            a_s, b_s = pltpu.roll(a, shift, 1), pltpu.roll(b, shift, 1)
            b = jnp.where(ok, a * b_s + b, b)
            a = jnp.where(ok, a * a_s, a)
        a_scr[:, sl] = a.reshape(rows, LANES)
        b_scr[:, sl] = b.reshape(rows, LANES)

    @pl.when(t == 0)
    def _():
        carry_scr[...] = jnp.broadcast_to(h0_ref[...], (8, RG_CB))

    def group(i, carry):
        g = (n_grp - 1 - i) if reverse else i
        off = pl.multiple_of(g * 8, 8)
        h = a_scr[pl.ds(off, 8), :] * carry + b_scr[pl.ds(off, 8), :]
        b_scr[pl.ds(off, 8), :] = h
        last = h[0:1] if reverse else h[7:8]
        return jnp.broadcast_to(last, (8, RG_CB))

    carry = lax.fori_loop(0, n_grp, group, carry_scr[...])
    carry_scr[...] = carry
    h_ref[...] = b_scr[...].astype(h_ref.dtype)
    hend_ref[...] = carry[0:1]


def _rglru(z3, conv_w, conv_b, wg, b_a, b_i, lam, h0, *, direction, tile):
    bsz, seq_len, _ = z3.shape
    n_t = seq_len // tile
    reverse = direction == 1
    n_cb = D // RG_CB
    xcol = ZXR // RG_CB
    hpt = tile // RG_HALO
    n_halo = seq_len // RG_HALO

    def tix(t):
        return (n_t - 1 - t) if reverse else t

    vec = lambda: pl.BlockSpec((None, 1, RG_CB), lambda b, c, t: (direction, 0, c))
    return pl.pallas_call(
        functools.partial(_rglru_body, reverse=reverse, n_t=n_t),
        grid=(bsz, n_cb, n_t),
        in_specs=[
            pl.BlockSpec((None, tile, RG_CB), lambda b, c, t: (b, tix(t), xcol + c)),
            pl.BlockSpec((None, RG_HALO, RG_CB),
                         lambda b, c, t: (b, jnp.maximum(tix(t) * hpt - 1, 0), xcol + c)),
            pl.BlockSpec((None, RG_HALO, RG_CB),
                         lambda b, c, t: (b, jnp.minimum((tix(t) + 1) * hpt, n_halo - 1), xcol + c)),
            pl.BlockSpec((4, RG_CB), lambda b, c, t: (0, c)),
            pl.BlockSpec((1, RG_CB), lambda b, c, t: (0, c)),
            pl.BlockSpec((None, RG_CB // LANES, LANES, 2 * LANES), lambda b, c, t: (direction, c, 0, 0)),
            vec(), vec(), vec(),
            pl.BlockSpec((None, 1, RG_CB), lambda b, c, t: (b, 0, c)),
        ],
        out_specs=[pl.BlockSpec((None, tile, RG_CB), lambda b, c, t: (b, tix(t), c)),
                   pl.BlockSpec((None, 1, RG_CB), lambda b, c, t: (b, 0, c))],
        out_shape=[jax.ShapeDtypeStruct((bsz, seq_len, D), BF16),
                   jax.ShapeDtypeStruct((bsz, 1, D), F32)],
        scratch_shapes=[pltpu.VMEM((tile, RG_CB), F32), pltpu.VMEM((tile, RG_CB), F32),
                        pltpu.VMEM((tile, RG_CB), F32), pltpu.VMEM((8, RG_CB), F32)],
        compiler_params=_params(("parallel", "parallel", "arbitrary")),
        name="rglru_bwd" if reverse else "rglru_fwd",
    )(z3, z3, z3, conv_w, conv_b.reshape(1, D), wg, b_a.reshape(2, 1, D), b_i.reshape(2, 1, D),
      lam.reshape(2, 1, D), h0)


def _gate_weights(w_a, w_i):
    def pair(w):
        w = w.reshape(2, 8, 2, 64, 64)
        z = jnp.zeros_like(w[:, :, 0])
        top = jnp.concatenate([w[:, :, 0], z], axis=-1)
        bot = jnp.concatenate([z, w[:, :, 1]], axis=-1)
        return jnp.concatenate([top, bot], axis=-2)
    return jnp.concatenate([pair(w_a), pair(w_i)], axis=-1).astype(BF16)


def _seqdft_body(c_ref, s_ref, xc_ref, xs_ref, o_ref, acc_ref):
    k = pl.program_id(2)

    @pl.when(k == 0)
    def _():
        acc_ref[...] = jnp.zeros_like(acc_ref)

    acc_ref[...] += (jnp.dot(c_ref[...], xc_ref[...], preferred_element_type=F32)
                     + jnp.dot(s_ref[...], xs_ref[...], preferred_element_type=F32))

    @pl.when(k == pl.num_programs(2) - 1)
    def _():
        o_ref[...] = acc_ref[...].astype(o_ref.dtype)


def _seq_dft(z3, cmat, nsmat, tile):
    bsz, seq_len, _ = z3.shape
    nt = seq_len // tile
    return pl.pallas_call(
        _seqdft_body,
        grid=(bsz, nt, nt),
        in_specs=[pl.BlockSpec((tile, tile), lambda b, i, k: (i, k)),
                  pl.BlockSpec((tile, tile), lambda b, i, k: (i, k)),
                  pl.BlockSpec((None, tile, D), lambda b, i, k: (b, k, ZXC // D)),
                  pl.BlockSpec((None, tile, D), lambda b, i, k: (b, k, ZXS // D))],
        out_specs=pl.BlockSpec((None, tile, D), lambda b, i, k: (b, i, 0)),
        out_shape=jax.ShapeDtypeStruct((bsz, seq_len, D), BF16),
        scratch_shapes=[pltpu.VMEM((tile, D), F32)],
        compiler_params=_params(("parallel", "parallel", "arbitrary")),
        name="seq_dft",
    )(cmat, nsmat, z3, z3)


def _dft_mats(n, scale):
    idx = jnp.arange(n, dtype=jnp.int32)
    ang = ((idx[:, None] * idx[None, :]) % n).astype(F32) * (2.0 * math.pi / n)
    return (jnp.cos(ang) * scale), (jnp.sin(ang) * scale)


FFT_L1 = 32
FFT_GRP = 16
FFT_K1B = 8


def _fft1_body(wa_ref, wb_ref, xc_ref, xs_ref, o_ref):
    rows = FFT_L1 * FFT_GRP
    xc = xc_ref[...].reshape(rows, D)
    xs = xs_ref[...].reshape(rows, D)
    a = (jnp.dot(wa_ref[...], xc, preferred_element_type=F32)
         + jnp.dot(wb_ref[...], xs, preferred_element_type=F32))
    a = a.astype(BF16)
    o_ref[:, :, :D] = a[:rows].reshape(FFT_L1, FFT_GRP, D)
    o_ref[:, :, D:] = a[rows:].reshape(FFT_L1, FFT_GRP, D)


def _fft2_body(m_ref, a_ref, o_ref):
    l2 = m_ref.shape[1]
    for j in range(FFT_K1B):
        o_ref[j] = (jnp.dot(m_ref[j, :, :l2], a_ref[j, :, :D], preferred_element_type=F32)
                    + jnp.dot(m_ref[j, :, l2:], a_ref[j, :, D:], preferred_element_type=F32)
                    ).astype(o_ref.dtype)


def _fft_tables(seq_len):
    l1, l2, g = FFT_L1, seq_len // FFT_L1, FFT_GRP
    i1 = jnp.arange(l1, dtype=jnp.int32)
    ang1 = ((i1[:, None] * i1[None, :]) % l1).astype(F32) * (2.0 * math.pi / l1)
    w1r, w1i = jnp.cos(ang1), -jnp.sin(ang1)
    eye = jnp.eye(g, dtype=F32)
    kron = lambda w: jnp.kron(w, eye)
    wa = jnp.concatenate([kron(w1r), kron(w1i)], axis=0).astype(BF16)
    wb = jnp.concatenate([kron(w1i), kron(-w1r)], axis=0).astype(BF16)
    i2 = jnp.arange(l2, dtype=jnp.int32)
    num = (i2[None, :, None] * i2[None, None, :] * l1 + i1[:, None, None] * i2[None, None, :]) % seq_len
    ang2 = num.astype(F32) * (2.0 * math.pi / seq_len)
    scale = seq_len ** -0.5
    m2 = jnp.concatenate([jnp.cos(ang2) * scale, jnp.sin(ang2) * scale], axis=2).astype(BF16)
    return wa, wb, m2


def _seq_fft(z, bsz, seq_len, tabs):
    wa, wb, m2 = tabs
    l1, l2, g = FFT_L1, seq_len // FFT_L1, FFT_GRP
    z4 = z.reshape(bsz, l1, l2, Z_COLS)
    rows = l1 * g
    a4 = pl.pallas_call(
        _fft1_body,
        grid=(bsz, l2 // g),
        in_specs=[_resident((2 * rows, rows)), _resident((2 * rows, rows)),
                  pl.BlockSpec((None, l1, g, D), lambda b, t: (b, 0, t, ZXC // D)),
                  pl.BlockSpec((None, l1, g, D), lambda b, t: (b, 0, t, ZXS // D))],
        out_specs=pl.BlockSpec((None, l1, g, 2 * D), lambda b, t: (b, 0, t, 0)),
        out_shape=jax.ShapeDtypeStruct((bsz, l1, l2, 2 * D), BF16),
        compiler_params=_params(("parallel", "parallel")),
        name="seq_fft_stage1",
    )(wa, wb, z4, z4)
    yp = pl.pallas_call(
        _fft2_body,
        grid=(l1 // FFT_K1B, bsz),
        in_specs=[pl.BlockSpec((FFT_K1B, l2, 2 * l2), lambda k, b: (k, 0, 0)),
                  pl.BlockSpec((None, FFT_K1B, l2, 2 * D), lambda k, b: (b, k, 0, 0))],
        out_specs=pl.BlockSpec((None, FFT_K1B, l2, D), lambda k, b: (b, k, 0, 0)),
        out_shape=jax.ShapeDtypeStruct((bsz, l1, l2, D), BF16),
        compiler_params=_params(("parallel", "parallel")),
        name="seq_fft_stage2",
    )(m2, a4)
    return yp.transpose(0, 2, 1, 3).reshape(bsz, seq_len, D)


def _gelu_tanh(x):
    return 0.5 * x * (1.0 + jnp.tanh(math.sqrt(2.0 / math.pi) * (x + 0.044715 * (x * x * x))))


MERGE_ROW_SPLIT = 2


def _merge_body(x_ref, mod_ref, attn_ref, gr_ref, hf_ref, hb_ref, four_ref,
                woa_ref, wor_ref, wof_ref, wm_ref, bm_ref, wout_ref, lng_ref, lnb_ref, wr_ref, br_ref,
                x1_ref, u2_ref, eid_ref, gate_ref, cnt_ref):
    mod = mod_ref[...]
    rows = x_ref.shape[0] // MERGE_ROW_SPLIT
    for s in range(MERGE_ROW_SPLIT):
        rs = slice(s * rows, (s + 1) * rows)
        x = x_ref[rs, :]
        u = (x * (1.0 + mod[1:2]) + mod[0:1]).astype(BF16)
        rg_in = (_gelu_tanh(gr_ref[rs, :].astype(F32))
                 * (hf_ref[rs, :].astype(F32) + hb_ref[rs, :].astype(F32))).astype(BF16)
        branches = ((attn_ref[rs, :], woa_ref), (rg_in, wor_ref), (four_ref[rs, :], wof_ref))
        merged = None
        for j, (inp, w_ref) in enumerate(branches):
            y = jnp.dot(inp, w_ref[...], preferred_element_type=F32)
            g = _sigmoid(jnp.dot(u, wm_ref[:, j * D:(j + 1) * D], preferred_element_type=F32)
                         + bm_ref[:, j * D:(j + 1) * D])
            merged = g * y if merged is None else merged + g * y
        y = jnp.dot(merged.astype(BF16), wout_ref[...], preferred_element_type=F32)
        x1 = _layer_norm(DEEPNORM_ALPHA * x + mod[2:3] * y, lng_ref[...], lnb_ref[...])
        x1_ref[rs, :] = x1
        u2_ref[rs, :] = (x1 * (1.0 + mod[4:5]) + mod[3:4]).astype(BF16)

    logits = jnp.dot(u2_ref[...], wr_ref[...], preferred_element_type=F32) + br_ref[...]
    lane = lax.broadcasted_iota(jnp.int32, logits.shape, 1)
    vals, idxs = [], []
    for _ in range(TOP_K):
        m = jnp.max(logits, axis=-1, keepdims=True)
        idx = jnp.min(jnp.where(logits == m, lane, LANES), axis=-1, keepdims=True)
        vals.append(m)
        idxs.append(idx)
        logits = jnp.where(lane == idx, -3.0e38, logits)
    exps = [jnp.exp(v - vals[0]) for v in vals]
    den = exps[0] + exps[1] + exps[2] + exps[3]
    eid = jnp.zeros(lane.shape, jnp.int32)
    gate = jnp.zeros(lane.shape, F32)
    member = jnp.zeros(lane.shape, F32)
    for k in range(TOP_K):
        eid = jnp.where(lane == k, idxs[k], eid)
        gate = jnp.where(lane == k, exps[k] / den, gate)
        member = member + jnp.where(lane == idxs[k], 1.0, 0.0)
    eid_ref[...] = eid
    gate_ref[...] = gate
    cnt_ref[...] = jnp.sum(member, axis=0, keepdims=True)


def _merge(x2, mods, attn, z, hf, hb, four, p, seq_len, tm):
    n = x2.shape[0]
    tpb = seq_len // tm
    bm = mods.shape[0]
    mod_map = (lambda i: (i // tpb, 0, 0)) if bm > 1 else (lambda i: (0, 0, 0))
    act = lambda: pl.BlockSpec((tm, D), lambda i: (i, 0))
    return pl.pallas_call(
        _merge_body,
        grid=(n // tm,),
        in_specs=[act(), pl.BlockSpec((None, 6, D), mod_map), act(),
                  pl.BlockSpec((tm, D), lambda i: (i, ZGR // D)), act(), act(), act(),
                  _resident((D, D)), _resident((D, D)), _resident((D, D)),
                  _resident((D, 3 * D)), _resident((1, 3 * D)), _resident((D, D)),
                  _resident((1, D)), _resident((1, D)), _resident((D, LANES)), _resident((1, LANES))],
        out_specs=[act(), act(), pl.BlockSpec((tm, LANES), lambda i: (i, 0)),
                   pl.BlockSpec((tm, LANES), lambda i: (i, 0)),
                   pl.BlockSpec((None, 1, LANES), lambda i: (i, 0, 0))],
        out_shape=[jax.ShapeDtypeStruct((n, D), F32), jax.ShapeDtypeStruct((n, D), BF16),
                   jax.ShapeDtypeStruct((n, LANES), jnp.int32), jax.ShapeDtypeStruct((n, LANES), F32),
                   jax.ShapeDtypeStruct((n // tm, 1, LANES), F32)],
        compiler_params=_params(("parallel",)),
        name="merge_ln_router",
    )(x2, mods, attn, z, hf, hb, four, p["w_o_attn"], p["w_o_rg"], p["w_o_four"], p["w_merge"],
      p["b_merge"], p["w_out"], p["ln1_g"], p["ln1_b"], p["w_router"], p["b_router"])


RUN_ALIGN = 8


def _sorted_rows(td):
    return TOP_K * td + N_EXPERTS * RUN_ALIGN


def _rows_copy(wait, src_ref, dst_ref, src_off, dst_off, length, sem):
    aligned = lambda v: v if isinstance(v, int) else pl.multiple_of(v, RUN_ALIGN)

    @pl.when(length > 0)
    def _():
        n = aligned(length)
        cp = pltpu.make_async_copy(src_ref.at[pl.ds(aligned(src_off), n)],
                                   dst_ref.at[pl.ds(aligned(dst_off), n)], sem)
        if wait:
            cp.wait()
        else:
            cp.start()


def _slot_positions(eid, toff, ltri):
    lane = lax.broadcasted_iota(jnp.int32, eid.shape, 1)
    hits = [eid[:, k:k + 1] == lane for k in range(TOP_K)]
    member = jnp.zeros(eid.shape, F32)
    for h in hits:
        member = member + jnp.where(h, 1.0, 0.0)
    rank = jnp.dot(ltri, member.astype(BF16), preferred_element_type=F32)
    base = toff + rank
    return [jnp.sum(jnp.where(h, base, 0.0), axis=-1, keepdims=True) for h in hits]


def _dispatch_body(rl_ref, to_ref, ro_ref, tlo_ref, tll_ref, nu_ref, tt_ref, u_ref, eid_ref, toff_ref, ltri_ref,
                   xs_hbm, pos_ref, srt, sems, *, td, tb):
    t = pl.program_id(0)
    n_t = pl.num_programs(0)
    slot = t % 2
    rows = srt.shape[1]

    def start_runs(tile, s):
        def one(e, c):
            i = tile * N_EXPERTS + e
            _rows_copy(False, srt.at[s], xs_hbm, to_ref[i], ro_ref[i], rl_ref[i], sems.at[s])
            return c
        lax.fori_loop(0, N_EXPERTS, one, 0)

    @pl.when(t == 0)
    def _():
        srt[1, :tb, :] = jnp.zeros((tb, D), F32)
        for wait in (False, True):
            def one(e, c):
                _rows_copy(wait, srt.at[1], xs_hbm, 0, tlo_ref[e], tll_ref[e], sems.at[1])
                return c
            lax.fori_loop(0, N_EXPERTS, one, 0)
        n_blocks = xs_hbm.shape[0] // tb
        for wait in (False, True):
            def blk(j, c):
                cp = pltpu.make_async_copy(srt.at[1, pl.ds(0, tb)],
                                           xs_hbm.at[pl.ds(pl.multiple_of(j * tb, RUN_ALIGN), tb)], sems.at[1])
                if wait:
                    cp.wait()
                else:
                    cp.start()
                return c
            lax.fori_loop(nu_ref[0], n_blocks, blk, 0)

    pos = _slot_positions(eid_ref[...], toff_ref[...], ltri_ref[...])
    lane = lax.broadcasted_iota(jnp.int32, (td, LANES), 1)
    pos = [p.astype(jnp.int32) for p in pos]
    pos_out = jnp.zeros((td, LANES), jnp.int32)
    for k in range(TOP_K):
        pos_out = jnp.where(lane == k, pos[k], pos_out)
    pos_ref[...] = pos_out

    col = lax.broadcasted_iota(jnp.int32, (td, rows), 1)
    hit = col == pos[0]
    for k in range(1, TOP_K):
        hit = hit | (col == pos[k])
    onehot_t = jnp.where(hit, 1.0, 0.0).astype(BF16)
    srt[slot] = lax.dot_general(onehot_t, u_ref[...], (((0,), (0,)), ((), ())),
                                preferred_element_type=F32)
    start_runs(t, slot)

    def wait_runs(tile, s):
        _rows_copy(True, srt.at[s], xs_hbm, 0, 0, tt_ref[tile], sems.at[s])

    @pl.when(t >= 1)
    def _():
        wait_runs(t - 1, 1 - slot)

    @pl.when(t == n_t - 1)
    def _():
        wait_runs(t, slot)


def _dispatch(u2, eid128, rt, td, tb):
    n_tok = u2.shape[0]
    n_t = n_tok // td
    rows = _sorted_rows(td)
    ltri = (jnp.arange(td)[:, None] > jnp.arange(td)[None, :]).astype(BF16)
    grid_spec = pltpu.PrefetchScalarGridSpec(
        num_scalar_prefetch=7,
        grid=(n_t,),
        in_specs=[pl.BlockSpec((td, D), lambda t, *_: (t, 0)),
                  pl.BlockSpec((td, LANES), lambda t, *_: (t, 0)),
                  pl.BlockSpec((None, 1, LANES), lambda t, *_: (t, 0, 0)),
                  pl.BlockSpec((td, td), lambda t, *_: (0, 0), pipeline_mode=pl.Buffered(1))],
        out_specs=[pl.BlockSpec(memory_space=pl.ANY),
                   pl.BlockSpec((td, LANES), lambda t, *_: (t, 0))],
        scratch_shapes=[pltpu.VMEM((2, rows, D), F32), pltpu.SemaphoreType.DMA((2,))],
    )
    return pl.pallas_call(
        functools.partial(_dispatch_body, td=td, tb=tb),
        grid_spec=grid_spec,
        out_shape=[jax.ShapeDtypeStruct((rt["n_slots"], D), F32),
                   jax.ShapeDtypeStruct((n_tok, LANES), jnp.int32)],
        compiler_params=_params(("arbitrary",)),
        name="moe_dispatch",
    )(rt["run_len"], rt["tile_off"], rt["run_off"], rt["tail_off"], rt["tail_len"], rt["n_used"], rt["tile_tot"],
      u2, eid128, rt["toff_f"], ltri)


MOE_ROW_SPLIT = 2


def _moe_body(be_ref, nused_ref, nb_ref, xs_ref, wgu_hbm, bgu_ref, wdn_hbm, bdn_ref, ys_ref,
              wgu_st, wdn_st, wgu_bf, wdn_bf, sems, *, layer):
    i = pl.program_id(0)
    n_used = nused_ref[0]

    def fetch(e):
        return (pltpu.make_async_copy(wgu_hbm.at[layer, e], wgu_st, sems.at[0]),
                pltpu.make_async_copy(wdn_hbm.at[layer, e], wdn_st, sems.at[1]))

    @pl.when(i == 0)
    def _():
        for cp in fetch(be_ref[0]):
            cp.start()

    @pl.when((i < n_used) & ((i == 0) | (be_ref[i] != be_ref[jnp.maximum(i - 1, 0)])))
    def _():
        e = be_ref[i]
        for cp in fetch(e):
            cp.wait()
        wgu_bf[...] = wgu_st[...].astype(BF16)
        wdn_bf[...] = wdn_st[...].astype(BF16)
        nxt = i + nb_ref[e]

        @pl.when(nxt < n_used)
        def _():
            for cp in fetch(be_ref[nxt]):
                cp.start()

    @pl.when(i < n_used)
    def _():
        rows = xs_ref.shape[0] // MOE_ROW_SPLIT
        for s in range(MOE_ROW_SPLIT):
            rs = slice(s * rows, (s + 1) * rows)
            x = xs_ref[rs, :].astype(BF16)
            h = jnp.dot(x, wgu_bf[...], preferred_element_type=F32) + bgu_ref[...]
            half = h.shape[1] // 2
            x_glu = jnp.minimum(h[:, :half], SWIGLU_LIMIT)
            x_lin = jnp.clip(h[:, half:], -SWIGLU_LIMIT, SWIGLU_LIMIT)
            act = (x_glu * _sigmoid(SWIGLU_ALPHA * x_glu) * (x_lin + 1.0)).astype(BF16)
            ys_ref[rs, :] = jnp.dot(act, wdn_bf[...], preferred_element_type=F32) + bdn_ref[...]

    @pl.when(i >= n_used)
    def _():
        ys_ref[...] = jnp.zeros_like(ys_ref)


def _moe(xs, rt, layer, w_gu, b_gu, w_dn, b_dn, tb):
    n_blocks = xs.shape[0] // tb
    d_ff2 = w_gu.shape[3]
    live = lambda i, nu: jnp.minimum(i, nu[0] - 1)
    grid_spec = pltpu.PrefetchScalarGridSpec(
        num_scalar_prefetch=3,
        grid=(n_blocks,),
        in_specs=[pl.BlockSpec((tb, D), lambda i, be, nu, nb: (live(i, nu), 0)),
                  pl.BlockSpec(memory_space=pl.ANY),
                  pl.BlockSpec((None, None, 1, d_ff2), lambda i, be, nu, nb: (layer, be[live(i, nu)], 0, 0)),
                  pl.BlockSpec(memory_space=pl.ANY),
                  pl.BlockSpec((None, None, 1, D), lambda i, be, nu, nb: (layer, be[live(i, nu)], 0, 0))],
        out_specs=pl.BlockSpec((tb, D), lambda i, be, nu, nb: (i, 0)),
        scratch_shapes=[pltpu.VMEM((D, d_ff2), F32), pltpu.VMEM((d_ff2 // 2, D), F32),
                        pltpu.VMEM((D, d_ff2), BF16), pltpu.VMEM((d_ff2 // 2, D), BF16),
                        pltpu.SemaphoreType.DMA((2,))],
    )
    return pl.pallas_call(
        functools.partial(_moe_body, layer=layer),
        grid_spec=grid_spec,
        out_shape=jax.ShapeDtypeStruct(xs.shape, F32),
        compiler_params=_params(("arbitrary",)),
        name="moe_experts",
    )(rt["block_e"], rt["n_used"], rt["blocks_per_e"], xs, w_gu, b_gu, w_dn, b_dn)


def _route(cnt128, n_tok, td, tb):
    n_t = n_tok // td
    cnt = cnt128[:, 0, :N_EXPERTS].astype(jnp.int32)
    run = (cnt + RUN_ALIGN - 1) // RUN_ALIGN * RUN_ALIGN
    tile_off = jnp.cumsum(run, axis=1) - run
    total = run.sum(axis=0)
    padded = (total + tb - 1) // tb * tb
    pend = jnp.cumsum(padded)
    pstart = pend - padded
    run_off = pstart[None, :] + jnp.cumsum(run, axis=0) - run
    worst = TOP_K * n_tok + n_t * N_EXPERTS * RUN_ALIGN + N_EXPERTS * tb
    n_slots = (worst + tb - 1) // tb * tb
    first_slot = jnp.arange(n_slots // tb, dtype=jnp.int32) * tb
    block_e = jnp.minimum(jnp.sum((pend[None, :] <= first_slot[:, None]).astype(jnp.int32), axis=1),
                          N_EXPERTS - 1)
    toff_f = jnp.zeros((n_t, 1, LANES), F32).at[:, 0, :N_EXPERTS].set(tile_off.astype(F32))
    i32 = lambda a: a.astype(jnp.int32).reshape(-1)
    return dict(run_len=i32(run), tile_off=i32(tile_off), run_off=i32(run_off),
                tail_off=i32(pstart + total), tail_len=i32(padded - total), toff_f=toff_f,
                tile_tot=i32(run.sum(axis=1)), blocks_per_e=i32(padded // tb),
                block_e=i32(block_e), n_used=i32(pend[-1] // tb), n_slots=n_slots)


def _combine_body(rl_ref, to_ref, ro_ref, tt_ref, ys_hbm, x1_ref, mod_ref, gate_ref, pos_ref, lng_ref, lnb_ref,
                  o_ref, srt, sems, *, td):
    t = pl.program_id(0)
    n_t = pl.num_programs(0)
    slot = t % 2
    rows = srt.shape[1]

    def start_runs(tile, s):
        def one(e, c):
            i = tile * N_EXPERTS + e
            _rows_copy(False, ys_hbm, srt.at[s], ro_ref[i], to_ref[i], rl_ref[i], sems.at[s])
            return c
        lax.fori_loop(0, N_EXPERTS, one, 0)

    @pl.when(t == 0)
    def _():
        srt[...] = jnp.zeros_like(srt)
        start_runs(0, 0)

    @pl.when(t + 1 < n_t)
    def _():
        start_runs(t + 1, 1 - slot)

    _rows_copy(True, ys_hbm, srt.at[slot], 0, 0, tt_ref[t], sems.at[slot])

    gate = gate_ref[...]
    pos = pos_ref[...]
    col = lax.broadcasted_iota(jnp.int32, (td, rows), 1)
    weights = jnp.zeros((td, rows), F32)
    for k in range(TOP_K):
        weights = jnp.where(col == pos[:, k:k + 1], gate[:, k:k + 1], weights)
    f = jnp.dot(weights.astype(BF16), srt[slot].astype(BF16), preferred_element_type=F32)
    mod = mod_ref[...]
    o_ref[...] = _layer_norm(DEEPNORM_ALPHA * x1_ref[...] + mod[5:6] * f, lng_ref[...], lnb_ref[...])


def _combine(x1, mods, gate128, pos128, ys, rt, ln_g, ln_b, seq_len, td):
    n = x1.shape[0]
    tpb = seq_len // td
    bm = mods.shape[0]
    mod_map = (lambda i, *_: (i // tpb, 0, 0)) if bm > 1 else (lambda i, *_: (0, 0, 0))
    grid_spec = pltpu.PrefetchScalarGridSpec(
        num_scalar_prefetch=4,
        grid=(n // td,),
        in_specs=[pl.BlockSpec(memory_space=pl.ANY),
                  pl.BlockSpec((td, D), lambda i, *_: (i, 0)),
                  pl.BlockSpec((None, 6, D), mod_map),
                  pl.BlockSpec((td, LANES), lambda i, *_: (i, 0)),
                  pl.BlockSpec((td, LANES), lambda i, *_: (i, 0)),
                  pl.BlockSpec((1, D), lambda i, *_: (0, 0)),
                  pl.BlockSpec((1, D), lambda i, *_: (0, 0))],
        out_specs=pl.BlockSpec((td, D), lambda i, *_: (i, 0)),
        scratch_shapes=[pltpu.VMEM((2, _sorted_rows(td), D), F32), pltpu.SemaphoreType.DMA((2,))],
    )
    return pl.pallas_call(
        functools.partial(_combine_body, td=td),
        grid_spec=grid_spec,
        out_shape=jax.ShapeDtypeStruct((n, D), F32),
        compiler_params=_params(("arbitrary",)),
        name="combine_ln",
    )(rt["run_len"], rt["tile_off"], rt["run_off"], rt["tile_tot"], ys, x1, mods, gate128, pos128,
      ln_g, ln_b)


def _rope_tables(seq_len):
    pos = jnp.arange(seq_len, dtype=jnp.int32)
    row = (pos // GRID_W).astype(F32)
    col = (pos % GRID_W).astype(F32)
    n_freq = HEAD_DIM // 4
    freqs = ROPE_BASE ** (-jnp.arange(n_freq, dtype=F32) / n_freq)
    ar, ac = row[:, None] * freqs, col[:, None] * freqs
    zero = jnp.zeros_like(ar)
    c = jnp.concatenate([jnp.cos(ar), jnp.cos(ar), jnp.cos(ac), jnp.cos(ac)], axis=1)
    sa = jnp.concatenate([-jnp.sin(ar), zero, -jnp.sin(ac), zero], axis=1)
    sb = jnp.concatenate([zero, jnp.sin(ar), zero, jnp.sin(ac)], axis=1)
    rep = MXU_N // HEAD_DIM
    return tuple(jnp.tile(t, (1, rep)) for t in (c, sa, sb))


def _layer_params(l, w):
    w_router = jnp.zeros((D, LANES), BF16).at[:, :N_EXPERTS].set(w["w_router"][l].astype(BF16))
    b_router = jnp.full((1, LANES), NEG_INF, F32).at[0, :N_EXPERTS].set(w["b_router"][l])
    return dict(
        w_in=w["w_in"][l].astype(BF16),
        w_o_attn=w["w_o_attn"][l].astype(BF16), w_o_rg=w["w_o_rg"][l].astype(BF16),
        w_o_four=w["w_o_four"][l].astype(BF16), w_merge=w["w_merge"][l].astype(BF16),
        b_merge=w["b_merge"][l].reshape(1, 3 * D), w_out=w["w_out"][l].astype(BF16),
        ln1_g=w["ln1_g"][l].reshape(1, D), ln1_b=w["ln1_b"][l].reshape(1, D),
        ln2_g=w["ln2_g"][l].reshape(1, D), ln2_b=w["ln2_b"][l].reshape(1, D),
        w_router=w_router, b_router=b_router,
        layer=l, w_gu=w["w_gate_up"], b_gu=w["b_gate_up"].reshape(DEPTH, N_EXPERTS, 1, -1),
        w_dn=w["w_down"], b_dn=w["b_down"].reshape(DEPTH, N_EXPERTS, 1, D),
        wg=_gate_weights(w["rg_w_a"][l], w["rg_w_i"][l]),
        conv_w=w["conv_w"][l], conv_b=w["conv_b"][l], b_a=w["rg_b_a"][l], b_i=w["rg_b_i"][l],
        lam=w["rg_lambda"][l], sink=w["attn_sink"][l],
    )


def _row_tile(seq_len, want):
    return min(want, seq_len)


def _ffn(x1, u2, eid128, gate128, cnt128, mods, p, seq_len, tb, td):
    rt = _route(cnt128, x1.shape[0], td, tb)
    xs, pos128 = _dispatch(u2, eid128, rt, td, tb)
    ys = _moe(xs, rt, p["layer"], p["w_gu"], p["b_gu"], p["w_dn"], p["b_dn"], tb)
    return _combine(x1, mods, gate128, pos128, ys, rt, p["ln2_g"], p["ln2_b"], seq_len, td)


def kernel(x, c, ctx, c_ctx, w_mod, b_mod, w_in, attn_sink, w_o_attn, conv_w, conv_b, rg_w_a, rg_b_a, rg_w_i, rg_b_i, rg_lambda, w_o_rg, w_o_four, w_merge, b_merge, w_out, ln1_g, ln1_b, w_router, b_router, w_gate_up, b_gate_up, w_down, b_down, ln2_g, ln2_b):
    w = dict(w_mod=w_mod, b_mod=b_mod, w_in=w_in, attn_sink=attn_sink, w_o_attn=w_o_attn, conv_w=conv_w,
             conv_b=conv_b, rg_w_a=rg_w_a, rg_b_a=rg_b_a, rg_w_i=rg_w_i, rg_b_i=rg_b_i, rg_lambda=rg_lambda,
             w_o_rg=w_o_rg, w_o_four=w_o_four, w_merge=w_merge, b_merge=b_merge, w_out=w_out, ln1_g=ln1_g,
             ln1_b=ln1_b, w_router=w_router, b_router=b_router, w_gate_up=w_gate_up, b_gate_up=b_gate_up,
             w_down=w_down, b_down=b_down, ln2_g=ln2_g, ln2_b=ln2_b)
    bsz, seq_len, _ = x.shape
    ctx_len = ctx.shape[1]
    n_lat, n_ctx = bsz * seq_len, bsz * ctx_len
    tm_lat, tm_ctx = _row_tile(seq_len, 512), _row_tile(ctx_len, 256)
    dft_lat, dft_ctx = _row_tile(seq_len, 1024), _row_tile(ctx_len, 256)
    rg_lat, rg_ctx = _row_tile(seq_len, 512), _row_tile(ctx_len, 256)
    tb_lat, tb_ctx = 512, 128

    rope = _rope_tables(seq_len)
    cw, sw = _dft_mats(F_GW, F_GW ** -0.5)
    csw = jnp.concatenate([cw, sw], axis=1).astype(BF16)
    use_fft = seq_len % (FFT_L1 * FFT_GRP) == 0
    if use_fft:
        fft_tabs = _fft_tables(seq_len)
    else:
        cl, sl = _dft_mats(seq_len, seq_len ** -0.5)
        cl, nsl = cl.astype(BF16), (-sl).astype(BF16)
    cc, sc = _dft_mats(ctx_len, ctx_len ** -0.5)
    cc, nsc = cc.astype(BF16), (-sc).astype(BF16)

    cond = jnp.zeros((16, D), F32).at[:bsz].set(c).at[bsz].set(c_ctx)
    x2 = x.reshape(n_lat, D)
    ctx2 = ctx.reshape(n_ctx, D)
    zero_h = jnp.zeros((bsz, 1, D), F32)

    for l in range(DEPTH):
        last = l == DEPTH - 1
        p = _layer_params(l, w)
        m = _adaln(cond, w_mod[l], b_mod[l]).reshape(16, 6, D)
        mods_lat, mods_ctx = m[:bsz], m[bsz:bsz + 1]

        zc = _proj(ctx2, mods_ctx, p["w_in"], csw, None, ctx_len, tm_ctx)
        z = _proj(x2, mods_lat, p["w_in"], csw, rope, seq_len, tm_lat)
        zc3 = zc.reshape(bsz, ctx_len, Z_COLS)
        z3 = z.reshape(bsz, seq_len, Z_COLS)

        rg = lambda zz, h0, d, tile: _rglru(zz, p["conv_w"], p["conv_b"], p["wg"], p["b_a"], p["b_i"],
                                            p["lam"], h0, direction=d, tile=tile)
        hcf, endf = rg(zc3, zero_h, 0, rg_ctx)
        hcb, endb = rg(zc3, zero_h, 1, rg_ctx)
        hf, _ = rg(z3, endf, 0, rg_lat)
        hb, _ = rg(z3, endb, 1, rg_lat)

        attn = _attention(z3, zc3, p["sink"])
        four = _seq_fft(z, bsz, seq_len, fft_tabs) if use_fft else _seq_dft(z3, cl, nsl, dft_lat)
        x1, u2, eid, gate, cnt = _merge(x2, mods_lat, attn.reshape(n_lat, D), z, hf.reshape(n_lat, D),
                                   hb.reshape(n_lat, D), four.reshape(n_lat, D), p, seq_len, tm_lat)
        x2 = _ffn(x1, u2, eid, gate, cnt, mods_lat, p, seq_len, tb_lat, tm_lat)

        if not last:
            attn_c = _ctx_attention(zc3, p["sink"])
            four_c = _seq_dft(zc3, cc, nsc, dft_ctx)
            c1, uc2, eid_c, gate_c, cnt_c = _merge(ctx2, mods_ctx, attn_c.reshape(n_ctx, D), zc,
                                            hcf.reshape(n_ctx, D), hcb.reshape(n_ctx, D),
                                            four_c.reshape(n_ctx, D), p, ctx_len, tm_ctx)
            ctx2 = _ffn(c1, uc2, eid_c, gate_c, cnt_c, mods_ctx, p, ctx_len, tb_ctx, tm_ctx)

    return x2.reshape(bsz, seq_len, D)
```

```python
import functools
import math

import jax
import jax.numpy as jnp
from jax import lax
from jax.experimental import pallas as pl
from jax.experimental.pallas import tpu as pltpu

F32 = jnp.float32
BF16 = jnp.bfloat16

D = 1024
HEAD_DIM = 64
N_HEADS = 16
N_KV = 4
Q_PER_KV = 4
WINDOW = 128
QBLK = 128
assert WINDOW == QBLK
GRID_W = 64
ROPE_BASE = 10000.0
RG_C = 8.0
CONV_LEFT = 2
F_GROUPS = 4
F_GW = 256
N_EXPERTS = 32
TOP_K = 4
SWIGLU_LIMIT = 7.0
SWIGLU_ALPHA = 1.702
LN_EPS = 1e-5
DEPTH = 2
DEEPNORM_ALPHA = (2 * DEPTH) ** 0.25
NEG_INF = -1e30

Q_OFF, K_OFF, V_OFF, XR_OFF, GR_OFF, XF_OFF, IN_COLS = 0, 1024, 1280, 1536, 2560, 3584, 4608
ZQ, ZXR, ZGR, ZXC, ZXS, ZK, ZV, Z_COLS = 0, 1024, 2048, 3072, 4096, 5120, 5376, 5632

VMEM_LIMIT_V7X = 56 * 1024 * 1024
LANES = 128
MXU_N = 256


def _params(sem, vmem=VMEM_LIMIT_V7X):
    return pltpu.CompilerParams(dimension_semantics=sem, vmem_limit_bytes=vmem)


def _resident(shape):
    nd = len(shape)
    return pl.BlockSpec(shape, lambda *_: (0,) * nd, pipeline_mode=pl.Buffered(1))


def _sigmoid(x):
    return 1.0 / (1.0 + jnp.exp(-x))


def _layer_norm(r, g, b):
    mu = jnp.mean(r, axis=-1, keepdims=True)
    rc = r - mu
    var = jnp.mean(rc * rc, axis=-1, keepdims=True)
    return rc * lax.rsqrt(var + LN_EPS) * g + b


def _adaln_body(c_ref, w_ref, b_ref, o_ref):
    c = c_ref[...]
    s = (c * _sigmoid(c)).astype(BF16)
    o_ref[...] = jnp.dot(s, w_ref[...].astype(BF16), preferred_element_type=F32) + b_ref[...]


def _adaln(cond, w_mod, b_mod):
    rows, n = cond.shape[0], w_mod.shape[1]
    tn = 1024
    return pl.pallas_call(
        _adaln_body,
        grid=(n // tn,),
        in_specs=[pl.BlockSpec((rows, D), lambda j: (0, 0)),
                  pl.BlockSpec((D, tn), lambda j: (0, j)),
                  pl.BlockSpec((1, tn), lambda j: (0, j))],
        out_specs=pl.BlockSpec((rows, tn), lambda j: (0, j)),
        out_shape=jax.ShapeDtypeStruct((rows, n), F32),
        compiler_params=_params(("arbitrary",)),
        name="adaln",
    )(cond, w_mod, b_mod.reshape(1, n))


def _proj_dst(j):
    if j < 4:
        return ZQ + j * MXU_N
    if j == 4:
        return ZK
    if j == 5:
        return ZV
    if j < 10:
        return ZXR + (j - 6) * MXU_N
    if j < 14:
        return ZGR + (j - 10) * MXU_N
    return None


def _proj_body(x_ref, mod_ref, w_ref, csw_ref, *rest, rope):
    if rope:
        c_ref, sa_ref, sb_ref, o_ref = rest
    else:
        (o_ref,) = rest
    mod = mod_ref[...]
    u = (x_ref[...] * (1.0 + mod[1:2]) + mod[0:1]).astype(BF16)
    for j in range(IN_COLS // MXU_N):
        acc = jnp.dot(u, w_ref[:, j * MXU_N:(j + 1) * MXU_N], preferred_element_type=F32)
        if rope and j < 5:
            acc = (acc * c_ref[...] + pltpu.roll(acc, MXU_N - 16, 1) * sa_ref[...]
                   + pltpu.roll(acc, 16, 1) * sb_ref[...])
        dst = _proj_dst(j)
        if dst is not None:
            o_ref[:, dst:dst + MXU_N] = acc.astype(BF16)
        else:
            g = j - 14
            t = jnp.dot(acc.astype(BF16), csw_ref[...], preferred_element_type=F32)
            o_ref[:, ZXC + g * F_GW:ZXC + (g + 1) * F_GW] = t[:, :F_GW].astype(BF16)
            o_ref[:, ZXS + g * F_GW:ZXS + (g + 1) * F_GW] = t[:, F_GW:].astype(BF16)


def _proj(x2, mods, w_in, csw, rope_tabs, seq_len, tm):
    n = x2.shape[0]
    tpb = seq_len // tm
    bm = mods.shape[0]
    mod_map = (lambda i: (i // tpb, 0, 0)) if bm > 1 else (lambda i: (0, 0, 0))
    in_specs = [pl.BlockSpec((tm, D), lambda i: (i, 0)),
                pl.BlockSpec((None, 6, D), mod_map),
                _resident((D, IN_COLS)),
                _resident((F_GW, 2 * F_GW))]
    args = [x2, mods, w_in, csw]
    if rope_tabs is not None:
        in_specs += [pl.BlockSpec((tm, MXU_N), lambda i: (i % tpb, 0))] * 3
        args += list(rope_tabs)
    return pl.pallas_call(
        functools.partial(_proj_body, rope=rope_tabs is not None),
        grid=(n // tm,),
        in_specs=in_specs,
        out_specs=pl.BlockSpec((tm, Z_COLS), lambda i: (i, 0)),
        out_shape=jax.ShapeDtypeStruct((n, Z_COLS), BF16),
        compiler_params=_params(("parallel",)),
        name="proj_rope" if rope_tabs is not None else "proj_ctx",
    )(*args)


def _group_attention(sink_ref, q_ref, o_ref, g, parts):
    rows = q_ref.shape[0]
    key_low = g % 2 == 0
    lane = lax.broadcasted_iota(jnp.int32, (1, LANES), 1)
    keep = (lane < HEAD_DIM) if key_low else (lane >= HEAD_DIM)
    kv_lanes = slice((g // 2) * LANES, (g // 2 + 1) * LANES)

    blocks = []
    for j in range(2):
        q32 = q_ref[:, (2 * g + j) * LANES:(2 * g + j + 1) * LANES].astype(F32) * (HEAD_DIM ** -0.5)
        same, rot = q32.astype(BF16), pltpu.roll(q32, HEAD_DIM, 1).astype(BF16)
        blocks += [same, rot] if key_low else [rot, same]
    qs = jnp.concatenate(blocks, axis=0)

    row = lax.broadcasted_iota(jnp.int32, (Q_PER_KV * rows, 1), 0)
    sink = jnp.full((Q_PER_KV * rows, 1), sink_ref[g * Q_PER_KV + Q_PER_KV - 1], F32)
    for r in range(Q_PER_KV - 2, -1, -1):
        sink = jnp.where(row < (r + 1) * rows, sink_ref[g * Q_PER_KV + r], sink)

    def lane_blocks(a):
        return [a[:, c * LANES:(c + 1) * LANES] for c in range(a.shape[1] // LANES)]

    scores = []
    m_fold = None
    for k_ref, _, bias in parts:
        k2 = jnp.where(keep, k_ref[:, kv_lanes], jnp.zeros((), BF16))
        s = lax.dot_general(qs, k2, (((1,), (1,)), ((), ())), preferred_element_type=F32)
        if bias is not None:
            s = s + bias
        scores.append(s)
        for blk in lane_blocks(s):
            m_fold = blk if m_fold is None else jnp.maximum(m_fold, blk)
    m = jnp.maximum(sink, jnp.max(m_fold, axis=-1, keepdims=True))
    p_fold = None
    out = None
    for s, (_, v_ref, _) in zip(scores, parts):
        p = jnp.exp(s - m)
        for blk in lane_blocks(p):
            p_fold = blk if p_fold is None else p_fold + blk
        v2 = jnp.where(keep, v_ref[:, kv_lanes], jnp.zeros((), BF16))
        o = jnp.dot(p.astype(BF16), v2, preferred_element_type=F32)
        out = o if out is None else out + o
    den = jnp.exp(sink - m) + jnp.sum(p_fold, axis=-1, keepdims=True)
    out = out / den
    for j in range(2):
        first, second = out[2 * j * rows:(2 * j + 1) * rows], out[(2 * j + 1) * rows:(2 * j + 2) * rows]
        if key_low:
            both = first + pltpu.roll(second, HEAD_DIM, 1)
        else:
            both = pltpu.roll(first, HEAD_DIM, 1) + second
        o_ref[:, (2 * g + j) * LANES:(2 * g + j + 1) * LANES] = both.astype(o_ref.dtype)


def _attn_body(sink_ref, q_ref, kl_ref, km_ref, kr_ref, vl_ref, vm_ref, vr_ref, kc_ref, vc_ref,
               o_ref, *, n_blk):
    n = pl.program_id(1)
    i = lax.broadcasted_iota(jnp.int32, (Q_PER_KV * QBLK, QBLK), 0) % QBLK
    j = lax.broadcasted_iota(jnp.int32, (Q_PER_KV * QBLK, QBLK), 1)
    bias_prev = jnp.where((j >= i) & (n > 0), 0.0, NEG_INF).astype(F32)
    bias_next = jnp.where((j <= i) & (n < n_blk - 1), 0.0, NEG_INF).astype(F32)
    parts = ((kl_ref, vl_ref, bias_prev), (km_ref, vm_ref, None), (kr_ref, vr_ref, bias_next),
             (kc_ref, vc_ref, None))
    for g in range(N_KV):
        _group_attention(sink_ref, q_ref, o_ref, g, parts)


def _attention(z3, zc3, sink):
    bsz, seq_len, _ = z3.shape
    ctx_len = zc3.shape[1]
    nb = seq_len // QBLK
    kcol, vcol = ZK // MXU_N, ZV // MXU_N

    def blk(col, off):
        return pl.BlockSpec((None, QBLK, MXU_N),
                            lambda b, n: (b, jnp.clip(n + off, 0, nb - 1), col))

    return pl.pallas_call(
        functools.partial(_attn_body, n_blk=nb),
        grid=(bsz, nb),
        in_specs=[pl.BlockSpec(memory_space=pltpu.SMEM),
                  pl.BlockSpec((None, QBLK, D), lambda b, n: (b, n, 0)),
                  blk(kcol, -1), blk(kcol, 0), blk(kcol, 1),
                  blk(vcol, -1), blk(vcol, 0), blk(vcol, 1),
                  pl.BlockSpec((None, ctx_len, MXU_N), lambda b, n: (b, 0, kcol)),
                  pl.BlockSpec((None, ctx_len, MXU_N), lambda b, n: (b, 0, vcol))],
        out_specs=pl.BlockSpec((None, QBLK, D), lambda b, n: (b, n, 0)),
        out_shape=jax.ShapeDtypeStruct((bsz, seq_len, D), BF16),
        compiler_params=_params(("parallel", "parallel")),
        name="window_attn",
    )(sink, z3, z3, z3, z3, z3, z3, z3, zc3, zc3)


def _ctx_attn_body(sink_ref, q_ref, kc_ref, vc_ref, o_ref):
    for g in range(N_KV):
        _group_attention(sink_ref, q_ref, o_ref, g, ((kc_ref, vc_ref, None),))


def _ctx_attention(zc3, sink):
    bsz, ctx_len, _ = zc3.shape
    kcol, vcol = ZK // MXU_N, ZV // MXU_N
    return pl.pallas_call(
        _ctx_attn_body,
        grid=(bsz,),
        in_specs=[pl.BlockSpec(memory_space=pltpu.SMEM),
                  pl.BlockSpec((None, ctx_len, D), lambda b: (b, 0, 0)),
                  pl.BlockSpec((None, ctx_len, MXU_N), lambda b: (b, 0, kcol)),
                  pl.BlockSpec((None, ctx_len, MXU_N), lambda b: (b, 0, vcol))],
        out_specs=pl.BlockSpec((None, ctx_len, D), lambda b: (b, 0, 0)),
        out_shape=jax.ShapeDtypeStruct((bsz, ctx_len, D), BF16),
        compiler_params=_params(("parallel",)),
        name="ctx_attn",
    )(sink, zc3, zc3, zc3)


RG_CB = 1024
RG_HALO = 16


def _rglru_body(x_ref, xp_ref, xn_ref, cw_ref, cb_ref, wg_ref, ba_ref, bi_ref, lam_ref, h0_ref,
                h_ref, hend_ref, a_scr, b_scr, xc_scr, carry_scr, *, reverse, n_t):
    t = pl.program_id(2)
    t_idx = (n_t - 1 - t) if reverse else t
    rows = x_ref.shape[0]
    n_grp = rows // 8
    x = x_ref[...].astype(F32)

    cw = cw_ref[...]

    def taps(xm2, xm1, x0, xp1):
        return cb_ref[...] + xm2 * cw[0:1] + xm1 * cw[1:2] + x0 * cw[2:3] + xp1 * cw[3:4]

    x_m1, x_m2, x_p1 = pltpu.roll(x, 1, 0), pltpu.roll(x, 2, 0), pltpu.roll(x, rows - 1, 0)
    xc_scr[...] = taps(x_m2, x_m1, x, x_p1)
    prev = xp_ref[...].astype(F32) * jnp.where(t_idx > 0, 1.0, 0.0)
    nxt = xn_ref[...].astype(F32) * jnp.where(t_idx < n_t - 1, 1.0, 0.0)
    p2, p1, n0 = prev[RG_HALO - 2:RG_HALO - 1], prev[RG_HALO - 1:RG_HALO], nxt[0:1]
    r8 = lax.broadcasted_iota(jnp.int32, (8, RG_CB), 0)
    head, tail = x[0:8], x[rows - 8:rows]
    h_m1 = jnp.where(r8 == 0, p1, pltpu.roll(head, 1, 0))
    h_m2 = jnp.where(r8 == 0, p2, jnp.where(r8 == 1, p1, pltpu.roll(head, 2, 0)))
    t_p1 = jnp.where(r8 == 7, n0, pltpu.roll(tail, 7, 0))
    xc_scr[0:8, :] = taps(h_m2, h_m1, head, x_p1[0:8])
    xc_scr[rows - 8:rows, :] = taps(x_m2[rows - 8:rows], x_m1[rows - 8:rows], tail, t_p1)

    lam = lam_ref[...]
    softplus_neg_lam = jnp.maximum(-lam, 0.0) + jnp.log1p(jnp.exp(-jnp.abs(lam)))
    sub = lax.broadcasted_iota(jnp.int32, (n_grp, 8, LANES), 1)
    for j in range(RG_CB // LANES):
        sl = slice(j * LANES, (j + 1) * LANES)
        xj = xc_scr[:, sl]
        gates = jnp.dot(xj.astype(BF16), wg_ref[j], preferred_element_type=F32)
        r = _sigmoid(gates[:, :LANES] + ba_ref[:, sl])
        ig = _sigmoid(gates[:, LANES:] + bi_ref[:, sl])
        log_a = -RG_C * r * softplus_neg_lam[:, sl]
        a = jnp.exp(log_a)
        th = jnp.tanh(log_a)
        b = jnp.sqrt(-2.0 * th / (1.0 - th)) * (ig * xj)
        a = a.reshape(n_grp, 8, LANES)
        b = b.reshape(n_grp, 8, LANES)
        for d in (1, 2, 4):
            shift = 8 - d if reverse else d
            ok = (sub < 8 - d) if reverse else (sub >= d)
            a_s, b_s = pltpu.roll(a, shift, 1), pltpu.roll(b, shift, 1)
            b = jnp.where(ok, a * b_s + b, b)
            a = jnp.where(ok, a * a_s, a)
        a_scr[:, sl] = a.reshape(rows, LANES)
        b_scr[:, sl] = b.reshape(rows, LANES)

    @pl.when(t == 0)
    def _():
        carry_scr[...] = jnp.broadcast_to(h0_ref[...], (8, RG_CB))

    def group(i, carry):
        g = (n_grp - 1 - i) if reverse else i
        off = pl.multiple_of(g * 8, 8)
        h = a_scr[pl.ds(off, 8), :] * carry + b_scr[pl.ds(off, 8), :]
        b_scr[pl.ds(off, 8), :] = h
        last = h[0:1] if reverse else h[7:8]
        return jnp.broadcast_to(last, (8, RG_CB))

    carry = lax.fori_loop(0, n_grp, group, carry_scr[...])
    carry_scr[...] = carry
    h_ref[...] = b_scr[...].astype(h_ref.dtype)
    hend_ref[...] = carry[0:1]


def _rglru(z3, conv_w, conv_b, wg, b_a, b_i, lam, h0, *, direction, tile):
    bsz, seq_len, _ = z3.shape
    n_t = seq_len // tile
    reverse = direction == 1
    n_cb = D // RG_CB
    xcol = ZXR // RG_CB
    hpt = tile // RG_HALO
    n_halo = seq_len // RG_HALO

    def tix(t):
        return (n_t - 1 - t) if reverse else t

    vec = lambda: pl.BlockSpec((None, 1, RG_CB), lambda b, c, t: (direction, 0, c))
    return pl.pallas_call(
        functools.partial(_rglru_body, reverse=reverse, n_t=n_t),
        grid=(bsz, n_cb, n_t),
        in_specs=[
            pl.BlockSpec((None, tile, RG_CB), lambda b, c, t: (b, tix(t), xcol + c)),
            pl.BlockSpec((None, RG_HALO, RG_CB),
                         lambda b, c, t: (b, jnp.maximum(tix(t) * hpt - 1, 0), xcol + c)),
            pl.BlockSpec((None, RG_HALO, RG_CB),
                         lambda b, c, t: (b, jnp.minimum((tix(t) + 1) * hpt, n_halo - 1), xcol + c)),
            pl.BlockSpec((4, RG_CB), lambda b, c, t: (0, c)),
            pl.BlockSpec((1, RG_CB), lambda b, c, t: (0, c)),
            pl.BlockSpec((None, RG_CB // LANES, LANES, 2 * LANES), lambda b, c, t: (direction, c, 0, 0)),
            vec(), vec(), vec(),
            pl.BlockSpec((None, 1, RG_CB), lambda b, c, t: (b, 0, c)),
        ],
        out_specs=[pl.BlockSpec((None, tile, RG_CB), lambda b, c, t: (b, tix(t), c)),
                   pl.BlockSpec((None, 1, RG_CB), lambda b, c, t: (b, 0, c))],
        out_shape=[jax.ShapeDtypeStruct((bsz, seq_len, D), BF16),
                   jax.ShapeDtypeStruct((bsz, 1, D), F32)],
        scratch_shapes=[pltpu.VMEM((tile, RG_CB), F32), pltpu.VMEM((tile, RG_CB), F32),
                        pltpu.VMEM((tile, RG_CB), F32), pltpu.VMEM((8, RG_CB), F32)],
        compiler_params=_params(("parallel", "parallel", "arbitrary")),
        name="rglru_bwd" if reverse else "rglru_fwd",
    )(z3, z3, z3, conv_w, conv_b.reshape(1, D), wg, b_a.reshape(2, 1, D), b_i.reshape(2, 1, D),
      lam.reshape(2, 1, D), h0)


def _gate_weights(w_a, w_i):
    def pair(w):
        w = w.reshape(2, 8, 2, 64, 64)
        z = jnp.zeros_like(w[:, :, 0])
        top = jnp.concatenate([w[:, :, 0], z], axis=-1)
        bot = jnp.concatenate([z, w[:, :, 1]], axis=-1)
        return jnp.concatenate([top, bot], axis=-2)
    return jnp.concatenate([pair(w_a), pair(w_i)], axis=-1).astype(BF16)


def _seqdft_body(c_ref, s_ref, xc_ref, xs_ref, o_ref, acc_ref):
    k = pl.program_id(2)

    @pl.when(k == 0)
    def _():
        acc_ref[...] = jnp.zeros_like(acc_ref)

    acc_ref[...] += (jnp.dot(c_ref[...], xc_ref[...], preferred_element_type=F32)
                     + jnp.dot(s_ref[...], xs_ref[...], preferred_element_type=F32))

    @pl.when(k == pl.num_programs(2) - 1)
    def _():
        o_ref[...] = acc_ref[...].astype(o_ref.dtype)


def _seq_dft(z3, cmat, nsmat, tile):
    bsz, seq_len, _ = z3.shape
    nt = seq_len // tile
    return pl.pallas_call(
        _seqdft_body,
        grid=(bsz, nt, nt),
        in_specs=[pl.BlockSpec((tile, tile), lambda b, i, k: (i, k)),
                  pl.BlockSpec((tile, tile), lambda b, i, k: (i, k)),
                  pl.BlockSpec((None, tile, D), lambda b, i, k: (b, k, ZXC // D)),
                  pl.BlockSpec((None, tile, D), lambda b, i, k: (b, k, ZXS // D))],
        out_specs=pl.BlockSpec((None, tile, D), lambda b, i, k: (b, i, 0)),
        out_shape=jax.ShapeDtypeStruct((bsz, seq_len, D), BF16),
        scratch_shapes=[pltpu.VMEM((tile, D), F32)],
        compiler_params=_params(("parallel", "parallel", "arbitrary")),
        name="seq_dft",
    )(cmat, nsmat, z3, z3)


def _dft_mats(n, scale):
    idx = jnp.arange(n, dtype=jnp.int32)
    ang = ((idx[:, None] * idx[None, :]) % n).astype(F32) * (2.0 * math.pi / n)
    return (jnp.cos(ang) * scale), (jnp.sin(ang) * scale)


FFT_L1 = 32
FFT_GRP = 16
FFT_K1B = 8


def _fft1_body(wa_ref, wb_ref, xc_ref, xs_ref, o_ref):
    rows = FFT_L1 * FFT_GRP
    xc = xc_ref[...].reshape(rows, D)
    xs = xs_ref[...].reshape(rows, D)
    a = (jnp.dot(wa_ref[...], xc, preferred_element_type=F32)
         + jnp.dot(wb_ref[...], xs, preferred_element_type=F32))
    a = a.astype(BF16)
    o_ref[:, :, :D] = a[:rows].reshape(FFT_L1, FFT_GRP, D)
    o_ref[:, :, D:] = a[rows:].reshape(FFT_L1, FFT_GRP, D)


def _fft2_body(m_ref, a_ref, o_ref):
    l2 = m_ref.shape[1]
    for j in range(FFT_K1B):
        o_ref[j] = (jnp.dot(m_ref[j, :, :l2], a_ref[j, :, :D], preferred_element_type=F32)
                    + jnp.dot(m_ref[j, :, l2:], a_ref[j, :, D:], preferred_element_type=F32)
                    ).astype(o_ref.dtype)


def _fft_tables(seq_len):
    l1, l2, g = FFT_L1, seq_len // FFT_L1, FFT_GRP
    i1 = jnp.arange(l1, dtype=jnp.int32)
    ang1 = ((i1[:, None] * i1[None, :]) % l1).astype(F32) * (2.0 * math.pi / l1)
    w1r, w1i = jnp.cos(ang1), -jnp.sin(ang1)
    eye = jnp.eye(g, dtype=F32)
    kron = lambda w: jnp.kron(w, eye)
    wa = jnp.concatenate([kron(w1r), kron(w1i)], axis=0).astype(BF16)
    wb = jnp.concatenate([kron(w1i), kron(-w1r)], axis=0).astype(BF16)
    i2 = jnp.arange(l2, dtype=jnp.int32)
    num = (i2[None, :, None] * i2[None, None, :] * l1 + i1[:, None, None] * i2[None, None, :]) % seq_len
    ang2 = num.astype(F32) * (2.0 * math.pi / seq_len)
    scale = seq_len ** -0.5
    m2 = jnp.concatenate([jnp.cos(ang2) * scale, jnp.sin(ang2) * scale], axis=2).astype(BF16)
    return wa, wb, m2


def _seq_fft(z, bsz, seq_len, tabs):
    wa, wb, m2 = tabs
    l1, l2, g = FFT_L1, seq_len // FFT_L1, FFT_GRP
    z4 = z.reshape(bsz, l1, l2, Z_COLS)
    rows = l1 * g
    a4 = pl.pallas_call(
        _fft1_body,
        grid=(bsz, l2 // g),
        in_specs=[_resident((2 * rows, rows)), _resident((2 * rows, rows)),
                  pl.BlockSpec((None, l1, g, D), lambda b, t: (b, 0, t, ZXC // D)),
                  pl.BlockSpec((None, l1, g, D), lambda b, t: (b, 0, t, ZXS // D))],
        out_specs=pl.BlockSpec((None, l1, g, 2 * D), lambda b, t: (b, 0, t, 0)),
        out_shape=jax.ShapeDtypeStruct((bsz, l1, l2, 2 * D), BF16),
        compiler_params=_params(("parallel", "parallel")),
        name="seq_fft_stage1",
    )(wa, wb, z4, z4)
    yp = pl.pallas_call(
        _fft2_body,
        grid=(l1 // FFT_K1B, bsz),
        in_specs=[pl.BlockSpec((FFT_K1B, l2, 2 * l2), lambda k, b: (k, 0, 0)),
                  pl.BlockSpec((None, FFT_K1B, l2, 2 * D), lambda k, b: (b, k, 0, 0))],
        out_specs=pl.BlockSpec((None, FFT_K1B, l2, D), lambda k, b: (b, k, 0, 0)),
        out_shape=jax.ShapeDtypeStruct((bsz, l1, l2, D), BF16),
        compiler_params=_params(("parallel", "parallel")),
        name="seq_fft_stage2",
    )(m2, a4)
    return yp.transpose(0, 2, 1, 3).reshape(bsz, seq_len, D)


def _gelu_tanh(x):
    return 0.5 * x * (1.0 + jnp.tanh(math.sqrt(2.0 / math.pi) * (x + 0.044715 * (x * x * x))))


MERGE_ROW_SPLIT = 2


def _merge_body(x_ref, mod_ref, attn_ref, gr_ref, hf_ref, hb_ref, four_ref,
                woa_ref, wor_ref, wof_ref, wm_ref, bm_ref, wout_ref, lng_ref, lnb_ref, wr_ref, br_ref,
                x1_ref, u2_ref, eid_ref, gate_ref, cnt_ref):
    mod = mod_ref[...]
    rows = x_ref.shape[0] // MERGE_ROW_SPLIT
    for s in range(MERGE_ROW_SPLIT):
        rs = slice(s * rows, (s + 1) * rows)
        x = x_ref[rs, :]
        u = (x * (1.0 + mod[1:2]) + mod[0:1]).astype(BF16)
        rg_in = (_gelu_tanh(gr_ref[rs, :].astype(F32))
                 * (hf_ref[rs, :].astype(F32) + hb_ref[rs, :].astype(F32))).astype(BF16)
        branches = ((attn_ref[rs, :], woa_ref), (rg_in, wor_ref), (four_ref[rs, :], wof_ref))
        merged = None
        for j, (inp, w_ref) in enumerate(branches):
            y = jnp.dot(inp, w_ref[...], preferred_element_type=F32)
            g = _sigmoid(jnp.dot(u, wm_ref[:, j * D:(j + 1) * D], preferred_element_type=F32)
                         + bm_ref[:, j * D:(j + 1) * D])
            merged = g * y if merged is None else merged + g * y
        y = jnp.dot(merged.astype(BF16), wout_ref[...], preferred_element_type=F32)
        x1 = _layer_norm(DEEPNORM_ALPHA * x + mod[2:3] * y, lng_ref[...], lnb_ref[...])
        x1_ref[rs, :] = x1
        u2_ref[rs, :] = (x1 * (1.0 + mod[4:5]) + mod[3:4]).astype(BF16)

    logits = jnp.dot(u2_ref[...], wr_ref[...], preferred_element_type=F32) + br_ref[...]
    lane = lax.broadcasted_iota(jnp.int32, logits.shape, 1)
    vals, idxs = [], []
    for _ in range(TOP_K):
        m = jnp.max(logits, axis=-1, keepdims=True)
        idx = jnp.min(jnp.where(logits == m, lane, LANES), axis=-1, keepdims=True)
        vals.append(m)
        idxs.append(idx)
        logits = jnp.where(lane == idx, -3.0e38, logits)
    exps = [jnp.exp(v - vals[0]) for v in vals]
    den = exps[0] + exps[1] + exps[2] + exps[3]
    eid = jnp.zeros(lane.shape, jnp.int32)
    gate = jnp.zeros(lane.shape, F32)
    member = jnp.zeros(lane.shape, F32)
    for k in range(TOP_K):
        eid = jnp.where(lane == k, idxs[k], eid)
        gate = jnp.where(lane == k, exps[k] / den, gate)
        member = member + jnp.where(lane == idxs[k], 1.0, 0.0)
    eid_ref[...] = eid
    gate_ref[...] = gate
    cnt_ref[...] = jnp.sum(member, axis=0, keepdims=True)


def _merge(x2, mods, attn, z, hf, hb, four, p, seq_len, tm):
    n = x2.shape[0]
    tpb = seq_len // tm
    bm = mods.shape[0]
    mod_map = (lambda i: (i // tpb, 0, 0)) if bm > 1 else (lambda i: (0, 0, 0))
    act = lambda: pl.BlockSpec((tm, D), lambda i: (i, 0))
    return pl.pallas_call(
        _merge_body,
        grid=(n // tm,),
        in_specs=[act(), pl.BlockSpec((None, 6, D), mod_map), act(),
                  pl.BlockSpec((tm, D), lambda i: (i, ZGR // D)), act(), act(), act(),
                  _resident((D, D)), _resident((D, D)), _resident((D, D)),
                  _resident((D, 3 * D)), _resident((1, 3 * D)), _resident((D, D)),
                  _resident((1, D)), _resident((1, D)), _resident((D, LANES)), _resident((1, LANES))],
        out_specs=[act(), act(), pl.BlockSpec((tm, LANES), lambda i: (i, 0)),
                   pl.BlockSpec((tm, LANES), lambda i: (i, 0)),
                   pl.BlockSpec((None, 1, LANES), lambda i: (i, 0, 0))],
        out_shape=[jax.ShapeDtypeStruct((n, D), F32), jax.ShapeDtypeStruct((n, D), BF16),
                   jax.ShapeDtypeStruct((n, LANES), jnp.int32), jax.ShapeDtypeStruct((n, LANES), F32),
                   jax.ShapeDtypeStruct((n // tm, 1, LANES), F32)],
        compiler_params=_params(("parallel",)),
        name="merge_ln_router",
    )(x2, mods, attn, z, hf, hb, four, p["w_o_attn"], p["w_o_rg"], p["w_o_four"], p["w_merge"],
      p["b_merge"], p["w_out"], p["ln1_g"], p["ln1_b"], p["w_router"], p["b_router"])


RUN_ALIGN = 8


def _sorted_rows(td):
    return TOP_K * td + N_EXPERTS * RUN_ALIGN


def _rows_copy(wait, src_ref, dst_ref, src_off, dst_off, length, sem):
    aligned = lambda v: v if isinstance(v, int) else pl.multiple_of(v, RUN_ALIGN)

    @pl.when(length > 0)
    def _():
        n = aligned(length)
        cp = pltpu.make_async_copy(src_ref.at[pl.ds(aligned(src_off), n)],
                                   dst_ref.at[pl.ds(aligned(dst_off), n)], sem)
        if wait:
            cp.wait()
        else:
            cp.start()


def _slot_positions(eid, toff, ltri):
    lane = lax.broadcasted_iota(jnp.int32, eid.shape, 1)
    hits = [eid[:, k:k + 1] == lane for k in range(TOP_K)]
    member = jnp.zeros(eid.shape, F32)
    for h in hits:
        member = member + jnp.where(h, 1.0, 0.0)
    rank = jnp.dot(ltri, member.astype(BF16), preferred_element_type=F32)
    base = toff + rank
    return [jnp.sum(jnp.where(h, base, 0.0), axis=-1, keepdims=True) for h in hits]


def _dispatch_body(rl_ref, to_ref, ro_ref, tlo_ref, tll_ref, nu_ref, tt_ref, u_ref, eid_ref, toff_ref, ltri_ref,
                   xs_hbm, pos_ref, srt, sems, *, td, tb):
    t = pl.program_id(0)
    n_t = pl.num_programs(0)
    slot = t % 2
    rows = srt.shape[1]

    def start_runs(tile, s):
        def one(e, c):
            i = tile * N_EXPERTS + e
            _rows_copy(False, srt.at[s], xs_hbm, to_ref[i], ro_ref[i], rl_ref[i], sems.at[s])
            return c
        lax.fori_loop(0, N_EXPERTS, one, 0)

    @pl.when(t == 0)
    def _():
        srt[1, :tb, :] = jnp.zeros((tb, D), F32)
        for wait in (False, True):
            def one(e, c):
                _rows_copy(wait, srt.at[1], xs_hbm, 0, tlo_ref[e], tll_ref[e], sems.at[1])
                return c
            lax.fori_loop(0, N_EXPERTS, one, 0)
        n_blocks = xs_hbm.shape[0] // tb
        for wait in (False, True):
            def blk(j, c):
                cp = pltpu.make_async_copy(srt.at[1, pl.ds(0, tb)],
                                           xs_hbm.at[pl.ds(pl.multiple_of(j * tb, RUN_ALIGN), tb)], sems.at[1])
                if wait:
                    cp.wait()
                else:
                    cp.start()
                return c
            lax.fori_loop(nu_ref[0], n_blocks, blk, 0)

    pos = _slot_positions(eid_ref[...], toff_ref[...], ltri_ref[...])
    lane = lax.broadcasted_iota(jnp.int32, (td, LANES), 1)
    pos = [p.astype(jnp.int32) for p in pos]
    pos_out = jnp.zeros((td, LANES), jnp.int32)
    for k in range(TOP_K):
        pos_out = jnp.where(lane == k, pos[k], pos_out)
    pos_ref[...] = pos_out

    col = lax.broadcasted_iota(jnp.int32, (td, rows), 1)
    onehot_t = jnp.zeros((td, rows), F32)
    for k in range(TOP_K):
        onehot_t = jnp.where(col == pos[k], 1.0, onehot_t)
    srt[slot] = lax.dot_general(onehot_t.astype(BF16), u_ref[...], (((0,), (0,)), ((), ())),
                                preferred_element_type=F32)
    start_runs(t, slot)

    def wait_runs(tile, s):
        _rows_copy(True, srt.at[s], xs_hbm, 0, 0, tt_ref[tile], sems.at[s])

    @pl.when(t >= 1)
    def _():
        wait_runs(t - 1, 1 - slot)

    @pl.when(t == n_t - 1)
    def _():
        wait_runs(t, slot)


def _dispatch(u2, eid128, rt, td, tb):
    n_tok = u2.shape[0]
    n_t = n_tok // td
    rows = _sorted_rows(td)
    ltri = (jnp.arange(td)[:, None] > jnp.arange(td)[None, :]).astype(BF16)
    grid_spec = pltpu.PrefetchScalarGridSpec(
        num_scalar_prefetch=7,
        grid=(n_t,),
        in_specs=[pl.BlockSpec((td, D), lambda t, *_: (t, 0)),
                  pl.BlockSpec((td, LANES), lambda t, *_: (t, 0)),
                  pl.BlockSpec((None, 1, LANES), lambda t, *_: (t, 0, 0)),
                  pl.BlockSpec((td, td), lambda t, *_: (0, 0), pipeline_mode=pl.Buffered(1))],
        out_specs=[pl.BlockSpec(memory_space=pl.ANY),
                   pl.BlockSpec((td, LANES), lambda t, *_: (t, 0))],
        scratch_shapes=[pltpu.VMEM((2, rows, D), F32), pltpu.SemaphoreType.DMA((2,))],
    )
    return pl.pallas_call(
        functools.partial(_dispatch_body, td=td, tb=tb),
        grid_spec=grid_spec,
        out_shape=[jax.ShapeDtypeStruct((rt["n_slots"], D), F32),
                   jax.ShapeDtypeStruct((n_tok, LANES), jnp.int32)],
        compiler_params=_params(("arbitrary",)),
        name="moe_dispatch",
    )(rt["run_len"], rt["tile_off"], rt["run_off"], rt["tail_off"], rt["tail_len"], rt["n_used"], rt["tile_tot"],
      u2, eid128, rt["toff_f"], ltri)


MOE_ROW_SPLIT = 2


def _moe_body(be_ref, nused_ref, nb_ref, xs_ref, wgu_hbm, bgu_ref, wdn_hbm, bdn_ref, ys_ref,
              wgu_st, wdn_st, wgu_bf, wdn_bf, sems, *, layer):
    i = pl.program_id(0)
    n_used = nused_ref[0]

    def fetch(e):
        return (pltpu.make_async_copy(wgu_hbm.at[layer, e], wgu_st, sems.at[0]),
                pltpu.make_async_copy(wdn_hbm.at[layer, e], wdn_st, sems.at[1]))

    @pl.when(i == 0)
    def _():
        for cp in fetch(be_ref[0]):
            cp.start()

    @pl.when((i < n_used) & ((i == 0) | (be_ref[i] != be_ref[jnp.maximum(i - 1, 0)])))
    def _():
        e = be_ref[i]
        for cp in fetch(e):
            cp.wait()
        wgu_bf[...] = wgu_st[...].astype(BF16)
        wdn_bf[...] = wdn_st[...].astype(BF16)
        nxt = i + nb_ref[e]

        @pl.when(nxt < n_used)
        def _():
            for cp in fetch(be_ref[nxt]):
                cp.start()

    @pl.when(i < n_used)
    def _():
        rows = xs_ref.shape[0] // MOE_ROW_SPLIT
        for s in range(MOE_ROW_SPLIT):
            rs = slice(s * rows, (s + 1) * rows)
            x = xs_ref[rs, :].astype(BF16)
            h = jnp.dot(x, wgu_bf[...], preferred_element_type=F32) + bgu_ref[...]
            half = h.shape[1] // 2
            x_glu = jnp.minimum(h[:, :half], SWIGLU_LIMIT)
            x_lin = jnp.clip(h[:, half:], -SWIGLU_LIMIT, SWIGLU_LIMIT)
            act = (x_glu * _sigmoid(SWIGLU_ALPHA * x_glu) * (x_lin + 1.0)).astype(BF16)
            ys_ref[rs, :] = jnp.dot(act, wdn_bf[...], preferred_element_type=F32) + bdn_ref[...]

    @pl.when(i >= n_used)
    def _():
        ys_ref[...] = jnp.zeros_like(ys_ref)


def _moe(xs, rt, layer, w_gu, b_gu, w_dn, b_dn, tb):
    n_blocks = xs.shape[0] // tb
    d_ff2 = w_gu.shape[3]
    live = lambda i, nu: jnp.minimum(i, nu[0] - 1)
    grid_spec = pltpu.PrefetchScalarGridSpec(
        num_scalar_prefetch=3,
        grid=(n_blocks,),
        in_specs=[pl.BlockSpec((tb, D), lambda i, be, nu, nb: (live(i, nu), 0)),
                  pl.BlockSpec(memory_space=pl.ANY),
                  pl.BlockSpec((None, None, 1, d_ff2), lambda i, be, nu, nb: (layer, be[live(i, nu)], 0, 0)),
                  pl.BlockSpec(memory_space=pl.ANY),
                  pl.BlockSpec((None, None, 1, D), lambda i, be, nu, nb: (layer, be[live(i, nu)], 0, 0))],
        out_specs=pl.BlockSpec((tb, D), lambda i, be, nu, nb: (i, 0)),
        scratch_shapes=[pltpu.VMEM((D, d_ff2), F32), pltpu.VMEM((d_ff2 // 2, D), F32),
                        pltpu.VMEM((D, d_ff2), BF16), pltpu.VMEM((d_ff2 // 2, D), BF16),
                        pltpu.SemaphoreType.DMA((2,))],
    )
    return pl.pallas_call(
        functools.partial(_moe_body, layer=layer),
        grid_spec=grid_spec,
        out_shape=jax.ShapeDtypeStruct(xs.shape, F32),
        compiler_params=_params(("arbitrary",)),
        name="moe_experts",
    )(rt["block_e"], rt["n_used"], rt["blocks_per_e"], xs, w_gu, b_gu, w_dn, b_dn)


def _route(cnt128, n_tok, td, tb):
    n_t = n_tok // td
    cnt = cnt128[:, 0, :N_EXPERTS].astype(jnp.int32)
    run = (cnt + RUN_ALIGN - 1) // RUN_ALIGN * RUN_ALIGN
    tile_off = jnp.cumsum(run, axis=1) - run
    total = run.sum(axis=0)
    padded = (total + tb - 1) // tb * tb
    pend = jnp.cumsum(padded)
    pstart = pend - padded
    run_off = pstart[None, :] + jnp.cumsum(run, axis=0) - run
    worst = TOP_K * n_tok + n_t * N_EXPERTS * RUN_ALIGN + N_EXPERTS * tb
    n_slots = (worst + tb - 1) // tb * tb
    first_slot = jnp.arange(n_slots // tb, dtype=jnp.int32) * tb
    block_e = jnp.minimum(jnp.sum((pend[None, :] <= first_slot[:, None]).astype(jnp.int32), axis=1),
                          N_EXPERTS - 1)
    toff_f = jnp.zeros((n_t, 1, LANES), F32).at[:, 0, :N_EXPERTS].set(tile_off.astype(F32))
    i32 = lambda a: a.astype(jnp.int32).reshape(-1)
    return dict(run_len=i32(run), tile_off=i32(tile_off), run_off=i32(run_off),
                tail_off=i32(pstart + total), tail_len=i32(padded - total), toff_f=toff_f,
                tile_tot=i32(run.sum(axis=1)), blocks_per_e=i32(padded // tb),
                block_e=i32(block_e), n_used=i32(pend[-1] // tb), n_slots=n_slots)


def _combine_body(rl_ref, to_ref, ro_ref, tt_ref, ys_hbm, x1_ref, mod_ref, gate_ref, pos_ref, lng_ref, lnb_ref,
                  o_ref, srt, sems, *, td):
    t = pl.program_id(0)
    n_t = pl.num_programs(0)
    slot = t % 2
    rows = srt.shape[1]

    def start_runs(tile, s):
        def one(e, c):
            i = tile * N_EXPERTS + e
            _rows_copy(False, ys_hbm, srt.at[s], ro_ref[i], to_ref[i], rl_ref[i], sems.at[s])
            return c
        lax.fori_loop(0, N_EXPERTS, one, 0)

    @pl.when(t == 0)
    def _():
        srt[...] = jnp.zeros_like(srt)
        start_runs(0, 0)

    @pl.when(t + 1 < n_t)
    def _():
        start_runs(t + 1, 1 - slot)

    _rows_copy(True, ys_hbm, srt.at[slot], 0, 0, tt_ref[t], sems.at[slot])

    gate = gate_ref[...]
    pos = pos_ref[...]
    col = lax.broadcasted_iota(jnp.int32, (td, rows), 1)
    weights = jnp.zeros((td, rows), F32)
    for k in range(TOP_K):
        weights = jnp.where(col == pos[:, k:k + 1], gate[:, k:k + 1], weights)
    f = jnp.dot(weights.astype(BF16), srt[slot].astype(BF16), preferred_element_type=F32)
    mod = mod_ref[...]
    o_ref[...] = _layer_norm(DEEPNORM_ALPHA * x1_ref[...] + mod[5:6] * f, lng_ref[...], lnb_ref[...])


def _combine(x1, mods, gate128, pos128, ys, rt, ln_g, ln_b, seq_len, td):
    n = x1.shape[0]
    tpb = seq_len // td
    bm = mods.shape[0]
    mod_map = (lambda i, *_: (i // tpb, 0, 0)) if bm > 1 else (lambda i, *_: (0, 0, 0))
    grid_spec = pltpu.PrefetchScalarGridSpec(
        num_scalar_prefetch=4,
        grid=(n // td,),
        in_specs=[pl.BlockSpec(memory_space=pl.ANY),
                  pl.BlockSpec((td, D), lambda i, *_: (i, 0)),
                  pl.BlockSpec((None, 6, D), mod_map),
                  pl.BlockSpec((td, LANES), lambda i, *_: (i, 0)),
                  pl.BlockSpec((td, LANES), lambda i, *_: (i, 0)),
                  pl.BlockSpec((1, D), lambda i, *_: (0, 0)),
                  pl.BlockSpec((1, D), lambda i, *_: (0, 0))],
        out_specs=pl.BlockSpec((td, D), lambda i, *_: (i, 0)),
        scratch_shapes=[pltpu.VMEM((2, _sorted_rows(td), D), F32), pltpu.SemaphoreType.DMA((2,))],
    )
    return pl.pallas_call(
        functools.partial(_combine_body, td=td),
        grid_spec=grid_spec,
        out_shape=jax.ShapeDtypeStruct((n, D), F32),
        compiler_params=_params(("arbitrary",)),
        name="combine_ln",
    )(rt["run_len"], rt["tile_off"], rt["run_off"], rt["tile_tot"], ys, x1, mods, gate128, pos128,
      ln_g, ln_b)


def _rope_tables(seq_len):
    pos = jnp.arange(seq_len, dtype=jnp.int32)
    row = (pos // GRID_W).astype(F32)
    col = (pos % GRID_W).astype(F32)
    n_freq = HEAD_DIM // 4
    freqs = ROPE_BASE ** (-jnp.arange(n_freq, dtype=F32) / n_freq)
    ar, ac = row[:, None] * freqs, col[:, None] * freqs
    zero = jnp.zeros_like(ar)
    c = jnp.concatenate([jnp.cos(ar), jnp.cos(ar), jnp.cos(ac), jnp.cos(ac)], axis=1)
    sa = jnp.concatenate([-jnp.sin(ar), zero, -jnp.sin(ac), zero], axis=1)
    sb = jnp.concatenate([zero, jnp.sin(ar), zero, jnp.sin(ac)], axis=1)
    rep = MXU_N // HEAD_DIM
    return tuple(jnp.tile(t, (1, rep)) for t in (c, sa, sb))


def _layer_params(l, w):
    w_router = jnp.zeros((D, LANES), BF16).at[:, :N_EXPERTS].set(w["w_router"][l].astype(BF16))
    b_router = jnp.full((1, LANES), NEG_INF, F32).at[0, :N_EXPERTS].set(w["b_router"][l])
    return dict(
        w_in=w["w_in"][l].astype(BF16),
        w_o_attn=w["w_o_attn"][l].astype(BF16), w_o_rg=w["w_o_rg"][l].astype(BF16),
        w_o_four=w["w_o_four"][l].astype(BF16), w_merge=w["w_merge"][l].astype(BF16),
        b_merge=w["b_merge"][l].reshape(1, 3 * D), w_out=w["w_out"][l].astype(BF16),
        ln1_g=w["ln1_g"][l].reshape(1, D), ln1_b=w["ln1_b"][l].reshape(1, D),
        ln2_g=w["ln2_g"][l].reshape(1, D), ln2_b=w["ln2_b"][l].reshape(1, D),
        w_router=w_router, b_router=b_router,
        layer=l, w_gu=w["w_gate_up"], b_gu=w["b_gate_up"].reshape(DEPTH, N_EXPERTS, 1, -1),
        w_dn=w["w_down"], b_dn=w["b_down"].reshape(DEPTH, N_EXPERTS, 1, D),
        wg=_gate_weights(w["rg_w_a"][l], w["rg_w_i"][l]),
        conv_w=w["conv_w"][l], conv_b=w["conv_b"][l], b_a=w["rg_b_a"][l], b_i=w["rg_b_i"][l],
        lam=w["rg_lambda"][l], sink=w["attn_sink"][l],
    )


def _row_tile(seq_len, want):
    return min(want, seq_len)


def _ffn(x1, u2, eid128, gate128, cnt128, mods, p, seq_len, tb, td):
    rt = _route(cnt128, x1.shape[0], td, tb)
    xs, pos128 = _dispatch(u2, eid128, rt, td, tb)
    ys = _moe(xs, rt, p["layer"], p["w_gu"], p["b_gu"], p["w_dn"], p["b_dn"], tb)
    return _combine(x1, mods, gate128, pos128, ys, rt, p["ln2_g"], p["ln2_b"], seq_len, td)


def kernel(x, c, ctx, c_ctx, w_mod, b_mod, w_in, attn_sink, w_o_attn, conv_w, conv_b, rg_w_a, rg_b_a, rg_w_i, rg_b_i, rg_lambda, w_o_rg, w_o_four, w_merge, b_merge, w_out, ln1_g, ln1_b, w_router, b_router, w_gate_up, b_gate_up, w_down, b_down, ln2_g, ln2_b):
    w = dict(w_mod=w_mod, b_mod=b_mod, w_in=w_in, attn_sink=attn_sink, w_o_attn=w_o_attn, conv_w=conv_w,
             conv_b=conv_b, rg_w_a=rg_w_a, rg_b_a=rg_b_a, rg_w_i=rg_w_i, rg_b_i=rg_b_i, rg_lambda=rg_lambda,
             w_o_rg=w_o_rg, w_o_four=w_o_four, w_merge=w_merge, b_merge=b_merge, w_out=w_out, ln1_g=ln1_g,
             ln1_b=ln1_b, w_router=w_router, b_router=b_router, w_gate_up=w_gate_up, b_gate_up=b_gate_up,
             w_down=w_down, b_down=b_down, ln2_g=ln2_g, ln2_b=ln2_b)
    bsz, seq_len, _ = x.shape
    ctx_len = ctx.shape[1]
    n_lat, n_ctx = bsz * seq_len, bsz * ctx_len
    tm_lat, tm_ctx = _row_tile(seq_len, 512), _row_tile(ctx_len, 256)
    dft_lat, dft_ctx = _row_tile(seq_len, 1024), _row_tile(ctx_len, 256)
    rg_lat, rg_ctx = _row_tile(seq_len, 1024), _row_tile(ctx_len, 256)
    tb_lat, tb_ctx = 512, 128

    rope = _rope_tables(seq_len)
    cw, sw = _dft_mats(F_GW, F_GW ** -0.5)
    csw = jnp.concatenate([cw, sw], axis=1).astype(BF16)
    use_fft = seq_len % (FFT_L1 * FFT_GRP) == 0
    if use_fft:
        fft_tabs = _fft_tables(seq_len)
    else:
        cl, sl = _dft_mats(seq_len, seq_len ** -0.5)
        cl, nsl = cl.astype(BF16), (-sl).astype(BF16)
    cc, sc = _dft_mats(ctx_len, ctx_len ** -0.5)
    cc, nsc = cc.astype(BF16), (-sc).astype(BF16)

    cond = jnp.zeros((16, D), F32).at[:bsz].set(c).at[bsz].set(c_ctx)
    x2 = x.reshape(n_lat, D)
    ctx2 = ctx.reshape(n_ctx, D)
    zero_h = jnp.zeros((bsz, 1, D), F32)

    for l in range(DEPTH):
        last = l == DEPTH - 1
        p = _layer_params(l, w)
        m = _adaln(cond, w_mod[l], b_mod[l]).reshape(16, 6, D)
        mods_lat, mods_ctx = m[:bsz], m[bsz:bsz + 1]

        zc = _proj(ctx2, mods_ctx, p["w_in"], csw, None, ctx_len, tm_ctx)
        z = _proj(x2, mods_lat, p["w_in"], csw, rope, seq_len, tm_lat)
        zc3 = zc.reshape(bsz, ctx_len, Z_COLS)
        z3 = z.reshape(bsz, seq_len, Z_COLS)

        rg = lambda zz, h0, d, tile: _rglru(zz, p["conv_w"], p["conv_b"], p["wg"], p["b_a"], p["b_i"],
                                            p["lam"], h0, direction=d, tile=tile)
        hcf, endf = rg(zc3, zero_h, 0, rg_ctx)
        hcb, endb = rg(zc3, zero_h, 1, rg_ctx)
        hf, _ = rg(z3, endf, 0, rg_lat)
        hb, _ = rg(z3, endb, 1, rg_lat)

        attn = _attention(z3, zc3, p["sink"])
        four = _seq_fft(z, bsz, seq_len, fft_tabs) if use_fft else _seq_dft(z3, cl, nsl, dft_lat)
        x1, u2, eid, gate, cnt = _merge(x2, mods_lat, attn.reshape(n_lat, D), z, hf.reshape(n_lat, D),
                                   hb.reshape(n_lat, D), four.reshape(n_lat, D), p, seq_len, tm_lat)
        x2 = _ffn(x1, u2, eid, gate, cnt, mods_lat, p, seq_len, tb_lat, tm_lat)

        if not last:
            attn_c = _ctx_attention(zc3, p["sink"])
            four_c = _seq_dft(zc3, cc, nsc, dft_ctx)
            c1, uc2, eid_c, gate_c, cnt_c = _merge(ctx2, mods_ctx, attn_c.reshape(n_ctx, D), zc,
                                            hcf.reshape(n_ctx, D), hcb.reshape(n_ctx, D),
                                            four_c.reshape(n_ctx, D), p, ctx_len, tm_ctx)
            ctx2 = _ffn(c1, uc2, eid_c, gate_c, cnt_c, mods_ctx, p, ctx_len, tb_ctx, tm_ctx)

    return x2.reshape(bsz, seq_len, D)
```

```python
import functools
import math

import jax
import jax.numpy as jnp
from jax import lax
from jax.experimental import pallas as pl
from jax.experimental.pallas import tpu as pltpu

F32 = jnp.float32
BF16 = jnp.bfloat16

D = 1024
HEAD_DIM = 64
N_HEADS = 16
N_KV = 4
Q_PER_KV = 4
WINDOW = 128
QBLK = 128
assert WINDOW == QBLK
GRID_W = 64
ROPE_BASE = 10000.0
RG_C = 8.0
CONV_LEFT = 2
F_GROUPS = 4
F_GW = 256
N_EXPERTS = 32
TOP_K = 4
SWIGLU_LIMIT = 7.0
SWIGLU_ALPHA = 1.702
LN_EPS = 1e-5
DEPTH = 2
DEEPNORM_ALPHA = (2 * DEPTH) ** 0.25
NEG_INF = -1e30

Q_OFF, K_OFF, V_OFF, XR_OFF, GR_OFF, XF_OFF, IN_COLS = 0, 1024, 1280, 1536, 2560, 3584, 4608
ZQ, ZXR, ZGR, ZXC, ZXS, ZK, ZV, Z_COLS = 0, 1024, 2048, 3072, 4096, 5120, 5376, 5632

VMEM_LIMIT_V7X = 56 * 1024 * 1024
LANES = 128
MXU_N = 256


def _params(sem, vmem=VMEM_LIMIT_V7X):
    return pltpu.CompilerParams(dimension_semantics=sem, vmem_limit_bytes=vmem)


def _resident(shape):
    nd = len(shape)
    return pl.BlockSpec(shape, lambda *_: (0,) * nd, pipeline_mode=pl.Buffered(1))


def _sigmoid(x):
    return 0.5 * jnp.tanh(0.5 * x) + 0.5


def _layer_norm(r, g, b):
    mu = jnp.mean(r, axis=-1, keepdims=True)
    rc = r - mu
    var = jnp.mean(rc * rc, axis=-1, keepdims=True)
    return rc * lax.rsqrt(var + LN_EPS) * g + b


def _adaln_body(c_ref, w_ref, b_ref, o_ref):
    c = c_ref[...]
    s = (c * _sigmoid(c)).astype(BF16)
    o_ref[...] = jnp.dot(s, w_ref[...].astype(BF16), preferred_element_type=F32) + b_ref[...]


def _adaln(cond, w_mod, b_mod):
    rows, n = cond.shape[0], w_mod.shape[1]
    tn = 1024
    return pl.pallas_call(
        _adaln_body,
        grid=(n // tn,),
        in_specs=[pl.BlockSpec((rows, D), lambda j: (0, 0)),
                  pl.BlockSpec((D, tn), lambda j: (0, j)),
                  pl.BlockSpec((1, tn), lambda j: (0, j))],
        out_specs=pl.BlockSpec((rows, tn), lambda j: (0, j)),
        out_shape=jax.ShapeDtypeStruct((rows, n), F32),
        compiler_params=_params(("arbitrary",)),
        name="adaln",
    )(cond, w_mod, b_mod.reshape(1, n))


def _proj_dst(j):
    if j < 4:
        return ZQ + j * MXU_N
    if j == 4:
        return ZK
    if j == 5:
        return ZV
    if j < 10:
        return ZXR + (j - 6) * MXU_N
    if j < 14:
        return ZGR + (j - 10) * MXU_N
    return None


def _proj_body(x_ref, mod_ref, w_ref, csw_ref, *rest, rope):
    if rope:
        c_ref, sa_ref, sb_ref, o_ref = rest
    else:
        (o_ref,) = rest
    mod = mod_ref[...]
    u = (x_ref[...] * (1.0 + mod[1:2]) + mod[0:1]).astype(BF16)
    for j in range(IN_COLS // MXU_N):
        acc = jnp.dot(u, w_ref[:, j * MXU_N:(j + 1) * MXU_N], preferred_element_type=F32)
        if rope and j < 5:
            acc = (acc * c_ref[...] + pltpu.roll(acc, MXU_N - 16, 1) * sa_ref[...]
                   + pltpu.roll(acc, 16, 1) * sb_ref[...])
        dst = _proj_dst(j)
        if dst is not None:
            o_ref[:, dst:dst + MXU_N] = acc.astype(BF16)
        else:
            g = j - 14
            t = jnp.dot(acc.astype(BF16), csw_ref[...], preferred_element_type=F32)
            o_ref[:, ZXC + g * F_GW:ZXC + (g + 1) * F_GW] = t[:, :F_GW].astype(BF16)
            o_ref[:, ZXS + g * F_GW:ZXS + (g + 1) * F_GW] = t[:, F_GW:].astype(BF16)


def _proj(x2, mods, w_in, csw, rope_tabs, seq_len, tm):
    n = x2.shape[0]
    tpb = seq_len // tm
    bm = mods.shape[0]
    mod_map = (lambda i: (i // tpb, 0, 0)) if bm > 1 else (lambda i: (0, 0, 0))
    in_specs = [pl.BlockSpec((tm, D), lambda i: (i, 0)),
                pl.BlockSpec((None, 6, D), mod_map),
                _resident((D, IN_COLS)),
                _resident((F_GW, 2 * F_GW))]
    args = [x2, mods, w_in, csw]
    if rope_tabs is not None:
        in_specs += [pl.BlockSpec((tm, MXU_N), lambda i: (i % tpb, 0))] * 3
        args += list(rope_tabs)
    return pl.pallas_call(
        functools.partial(_proj_body, rope=rope_tabs is not None),
        grid=(n // tm,),
        in_specs=in_specs,
        out_specs=pl.BlockSpec((tm, Z_COLS), lambda i: (i, 0)),
        out_shape=jax.ShapeDtypeStruct((n, Z_COLS), BF16),
        compiler_params=_params(("parallel",)),
        name="proj_rope" if rope_tabs is not None else "proj_ctx",
    )(*args)


def _group_attention(sink_ref, q_ref, o_ref, g, parts):
    rows = q_ref.shape[0]
    key_low = g % 2 == 0
    lane = lax.broadcasted_iota(jnp.int32, (1, LANES), 1)
    keep = (lane < HEAD_DIM) if key_low else (lane >= HEAD_DIM)
    kv_lanes = slice((g // 2) * LANES, (g // 2 + 1) * LANES)

    blocks = []
    for j in range(2):
        q32 = q_ref[:, (2 * g + j) * LANES:(2 * g + j + 1) * LANES].astype(F32) * (HEAD_DIM ** -0.5)
        same, rot = q32.astype(BF16), pltpu.roll(q32, HEAD_DIM, 1).astype(BF16)
        blocks += [same, rot] if key_low else [rot, same]
    qs = jnp.concatenate(blocks, axis=0)

    row = lax.broadcasted_iota(jnp.int32, (Q_PER_KV * rows, 1), 0)
    sink = jnp.full((Q_PER_KV * rows, 1), sink_ref[g * Q_PER_KV + Q_PER_KV - 1], F32)
    for r in range(Q_PER_KV - 2, -1, -1):
        sink = jnp.where(row < (r + 1) * rows, sink_ref[g * Q_PER_KV + r], sink)

    def lane_blocks(a):
        return [a[:, c * LANES:(c + 1) * LANES] for c in range(a.shape[1] // LANES)]

    scores = []
    m_fold = None
    for k_ref, _, bias in parts:
        k2 = jnp.where(keep, k_ref[:, kv_lanes], jnp.zeros((), BF16))
        s = lax.dot_general(qs, k2, (((1,), (1,)), ((), ())), preferred_element_type=F32)
        if bias is not None:
            s = s + bias
        scores.append(s)
        for blk in lane_blocks(s):
            m_fold = blk if m_fold is None else jnp.maximum(m_fold, blk)
    m = jnp.maximum(sink, jnp.max(m_fold, axis=-1, keepdims=True))
    p_fold = None
    out = None
    for s, (_, v_ref, _) in zip(scores, parts):
        p = jnp.exp(s - m)
        for blk in lane_blocks(p):
            p_fold = blk if p_fold is None else p_fold + blk
        v2 = jnp.where(keep, v_ref[:, kv_lanes], jnp.zeros((), BF16))
        o = jnp.dot(p.astype(BF16), v2, preferred_element_type=F32)
        out = o if out is None else out + o
    den = jnp.exp(sink - m) + jnp.sum(p_fold, axis=-1, keepdims=True)
    out = out / den
    for j in range(2):
        first, second = out[2 * j * rows:(2 * j + 1) * rows], out[(2 * j + 1) * rows:(2 * j + 2) * rows]
        if key_low:
            both = first + pltpu.roll(second, HEAD_DIM, 1)
        else:
            both = pltpu.roll(first, HEAD_DIM, 1) + second
        o_ref[:, (2 * g + j) * LANES:(2 * g + j + 1) * LANES] = both.astype(o_ref.dtype)


def _attn_body(sink_ref, q_ref, kl_ref, km_ref, kr_ref, vl_ref, vm_ref, vr_ref, kc_ref, vc_ref,
               o_ref, *, n_blk):
    n = pl.program_id(1)
    i = lax.broadcasted_iota(jnp.int32, (Q_PER_KV * QBLK, QBLK), 0) % QBLK
    j = lax.broadcasted_iota(jnp.int32, (Q_PER_KV * QBLK, QBLK), 1)
    bias_prev = jnp.where((j >= i) & (n > 0), 0.0, NEG_INF).astype(F32)
    bias_next = jnp.where((j <= i) & (n < n_blk - 1), 0.0, NEG_INF).astype(F32)
    parts = ((kl_ref, vl_ref, bias_prev), (km_ref, vm_ref, None), (kr_ref, vr_ref, bias_next),
             (kc_ref, vc_ref, None))
    for g in range(N_KV):
        _group_attention(sink_ref, q_ref, o_ref, g, parts)


def _attention(z3, zc3, sink):
    bsz, seq_len, _ = z3.shape
    ctx_len = zc3.shape[1]
    nb = seq_len // QBLK
    kcol, vcol = ZK // MXU_N, ZV // MXU_N

    def blk(col, off):
        return pl.BlockSpec((None, QBLK, MXU_N),
                            lambda b, n: (b, jnp.clip(n + off, 0, nb - 1), col))

    return pl.pallas_call(
        functools.partial(_attn_body, n_blk=nb),
        grid=(bsz, nb),
        in_specs=[pl.BlockSpec(memory_space=pltpu.SMEM),
                  pl.BlockSpec((None, QBLK, D), lambda b, n: (b, n, 0)),
                  blk(kcol, -1), blk(kcol, 0), blk(kcol, 1),
                  blk(vcol, -1), blk(vcol, 0), blk(vcol, 1),
                  pl.BlockSpec((None, ctx_len, MXU_N), lambda b, n: (b, 0, kcol)),
                  pl.BlockSpec((None, ctx_len, MXU_N), lambda b, n: (b, 0, vcol))],
        out_specs=pl.BlockSpec((None, QBLK, D), lambda b, n: (b, n, 0)),
        out_shape=jax.ShapeDtypeStruct((bsz, seq_len, D), BF16),
        compiler_params=_params(("parallel", "parallel")),
        name="window_attn",
    )(sink, z3, z3, z3, z3, z3, z3, z3, zc3, zc3)


def _ctx_attn_body(sink_ref, q_ref, kc_ref, vc_ref, o_ref):
    for g in range(N_KV):
        _group_attention(sink_ref, q_ref, o_ref, g, ((kc_ref, vc_ref, None),))


def _ctx_attention(zc3, sink):
    bsz, ctx_len, _ = zc3.shape
    kcol, vcol = ZK // MXU_N, ZV // MXU_N
    return pl.pallas_call(
        _ctx_attn_body,
        grid=(bsz,),
        in_specs=[pl.BlockSpec(memory_space=pltpu.SMEM),
                  pl.BlockSpec((None, ctx_len, D), lambda b: (b, 0, 0)),
                  pl.BlockSpec((None, ctx_len, MXU_N), lambda b: (b, 0, kcol)),
                  pl.BlockSpec((None, ctx_len, MXU_N), lambda b: (b, 0, vcol))],
        out_specs=pl.BlockSpec((None, ctx_len, D), lambda b: (b, 0, 0)),
        out_shape=jax.ShapeDtypeStruct((bsz, ctx_len, D), BF16),
        compiler_params=_params(("parallel",)),
        name="ctx_attn",
    )(sink, zc3, zc3, zc3)


RG_CB = 1024
RG_HALO = 16


def _rglru_body(x_ref, xp_ref, xn_ref, cw_ref, cb_ref, wg_ref, ba_ref, bi_ref, lam_ref, h0_ref,
                h_ref, hend_ref, a_scr, b_scr, xc_scr, carry_scr, *, reverse, n_t):
    t = pl.program_id(2)
    t_idx = (n_t - 1 - t) if reverse else t
    rows = x_ref.shape[0]
    n_grp = rows // 8
    x = x_ref[...].astype(F32)

    cw = cw_ref[...]

    def taps(xm2, xm1, x0, xp1):
        return cb_ref[...] + xm2 * cw[0:1] + xm1 * cw[1:2] + x0 * cw[2:3] + xp1 * cw[3:4]

    x_m1, x_m2, x_p1 = pltpu.roll(x, 1, 0), pltpu.roll(x, 2, 0), pltpu.roll(x, rows - 1, 0)
    xc_scr[...] = taps(x_m2, x_m1, x, x_p1)
    prev = xp_ref[...].astype(F32) * jnp.where(t_idx > 0, 1.0, 0.0)
    nxt = xn_ref[...].astype(F32) * jnp.where(t_idx < n_t - 1, 1.0, 0.0)
    p2, p1, n0 = prev[RG_HALO - 2:RG_HALO - 1], prev[RG_HALO - 1:RG_HALO], nxt[0:1]
    r8 = lax.broadcasted_iota(jnp.int32, (8, RG_CB), 0)
    head, tail = x[0:8], x[rows - 8:rows]
    h_m1 = jnp.where(r8 == 0, p1, pltpu.roll(head, 1, 0))
    h_m2 = jnp.where(r8 == 0, p2, jnp.where(r8 == 1, p1, pltpu.roll(head, 2, 0)))
    t_p1 = jnp.where(r8 == 7, n0, pltpu.roll(tail, 7, 0))
    xc_scr[0:8, :] = taps(h_m2, h_m1, head, x_p1[0:8])
    xc_scr[rows - 8:rows, :] = taps(x_m2[rows - 8:rows], x_m1[rows - 8:rows], tail, t_p1)

    lam = lam_ref[...]
    softplus_neg_lam = jnp.maximum(-lam, 0.0) + jnp.log1p(jnp.exp(-jnp.abs(lam)))
    sub = lax.broadcasted_iota(jnp.int32, (n_grp, 8, LANES), 1)
    for j in range(RG_CB // LANES):
        sl = slice(j * LANES, (j + 1) * LANES)
        xj = xc_scr[:, sl]
        gates = jnp.dot(xj.astype(BF16), wg_ref[j], preferred_element_type=F32)
        r = _sigmoid(gates[:, :LANES] + ba_ref[:, sl])
        ig = _sigmoid(gates[:, LANES:] + bi_ref[:, sl])
        log_a = -RG_C * r * softplus_neg_lam[:, sl]
        a = jnp.exp(log_a)
        th = jnp.tanh(log_a)
        b = jnp.sqrt(-2.0 * th / (1.0 - th)) * (ig * xj)
        a = a.reshape(n_grp, 8, LANES)
        b = b.reshape(n_grp, 8, LANES)
        for d in (1, 2, 4):
            shift = 8 - d if reverse else d
            ok = (sub < 8 - d) if reverse else (sub >= d)
            a_s, b_s = pltpu.roll(a, shift, 1), pltpu.roll(b, shift, 1)
            b = jnp.where(ok, a * b_s + b, b)
            a = jnp.where(ok, a * a_s, a)
        a_scr[:, sl] = a.reshape(rows, LANES)
        b_scr[:, sl] = b.reshape(rows, LANES)

    @pl.when(t == 0)
    def _():
        carry_scr[...] = jnp.broadcast_to(h0_ref[...], (8, RG_CB))

    def group(i, carry):
        g = (n_grp - 1 - i) if reverse else i
        off = pl.multiple_of(g * 8, 8)
        h = a_scr[pl.ds(off, 8), :] * carry + b_scr[pl.ds(off, 8), :]
        b_scr[pl.ds(off, 8), :] = h
        last = h[0:1] if reverse else h[7:8]
        return jnp.broadcast_to(last, (8, RG_CB))

    carry = lax.fori_loop(0, n_grp, group, carry_scr[...])
    carry_scr[...] = carry
    h_ref[...] = b_scr[...].astype(h_ref.dtype)
    hend_ref[...] = carry[0:1]


def _rglru(z3, conv_w, conv_b, wg, b_a, b_i, lam, h0, *, direction, tile):
    bsz, seq_len, _ = z3.shape
    n_t = seq_len // tile
    reverse = direction == 1
    n_cb = D // RG_CB
    xcol = ZXR // RG_CB
    hpt = tile // RG_HALO
    n_halo = seq_len // RG_HALO

    def tix(t):
        return (n_t - 1 - t) if reverse else t

    vec = lambda: pl.BlockSpec((None, 1, RG_CB), lambda b, c, t: (direction, 0, c))
    return pl.pallas_call(
        functools.partial(_rglru_body, reverse=reverse, n_t=n_t),
        grid=(bsz, n_cb, n_t),
        in_specs=[
            pl.BlockSpec((None, tile, RG_CB), lambda b, c, t: (b, tix(t), xcol + c)),
            pl.BlockSpec((None, RG_HALO, RG_CB),
                         lambda b, c, t: (b, jnp.maximum(tix(t) * hpt - 1, 0), xcol + c)),
            pl.BlockSpec((None, RG_HALO, RG_CB),
                         lambda b, c, t: (b, jnp.minimum((tix(t) + 1) * hpt, n_halo - 1), xcol + c)),
            pl.BlockSpec((4, RG_CB), lambda b, c, t: (0, c)),
            pl.BlockSpec((1, RG_CB), lambda b, c, t: (0, c)),
            pl.BlockSpec((None, RG_CB // LANES, LANES, 2 * LANES), lambda b, c, t: (direction, c, 0, 0)),
            vec(), vec(), vec(),
            pl.BlockSpec((None, 1, RG_CB), lambda b, c, t: (b, 0, c)),
        ],
        out_specs=[pl.BlockSpec((None, tile, RG_CB), lambda b, c, t: (b, tix(t), c)),
                   pl.BlockSpec((None, 1, RG_CB), lambda b, c, t: (b, 0, c))],
        out_shape=[jax.ShapeDtypeStruct((bsz, seq_len, D), BF16),
                   jax.ShapeDtypeStruct((bsz, 1, D), F32)],
        scratch_shapes=[pltpu.VMEM((tile, RG_CB), F32), pltpu.VMEM((tile, RG_CB), F32),
                        pltpu.VMEM((tile, RG_CB), F32), pltpu.VMEM((8, RG_CB), F32)],
        compiler_params=_params(("parallel", "parallel", "arbitrary")),
        name="rglru_bwd" if reverse else "rglru_fwd",
    )(z3, z3, z3, conv_w, conv_b.reshape(1, D), wg, b_a.reshape(2, 1, D), b_i.reshape(2, 1, D),
      lam.reshape(2, 1, D), h0)


def _gate_weights(w_a, w_i):
    def pair(w):
        w = w.reshape(2, 8, 2, 64, 64)
        z = jnp.zeros_like(w[:, :, 0])
        top = jnp.concatenate([w[:, :, 0], z], axis=-1)
        bot = jnp.concatenate([z, w[:, :, 1]], axis=-1)
        return jnp.concatenate([top, bot], axis=-2)
    return jnp.concatenate([pair(w_a), pair(w_i)], axis=-1).astype(BF16)


def _seqdft_body(c_ref, s_ref, xc_ref, xs_ref, o_ref, acc_ref):
    k = pl.program_id(2)

    @pl.when(k == 0)
    def _():
        acc_ref[...] = jnp.zeros_like(acc_ref)

    acc_ref[...] += (jnp.dot(c_ref[...], xc_ref[...], preferred_element_type=F32)
                     + jnp.dot(s_ref[...], xs_ref[...], preferred_element_type=F32))

    @pl.when(k == pl.num_programs(2) - 1)
    def _():
        o_ref[...] = acc_ref[...].astype(o_ref.dtype)


def _seq_dft(z3, cmat, nsmat, tile):
    bsz, seq_len, _ = z3.shape
    nt = seq_len // tile
    return pl.pallas_call(
        _seqdft_body,
        grid=(bsz, nt, nt),
        in_specs=[pl.BlockSpec((tile, tile), lambda b, i, k: (i, k)),
                  pl.BlockSpec((tile, tile), lambda b, i, k: (i, k)),
                  pl.BlockSpec((None, tile, D), lambda b, i, k: (b, k, ZXC // D)),
                  pl.BlockSpec((None, tile, D), lambda b, i, k: (b, k, ZXS // D))],
        out_specs=pl.BlockSpec((None, tile, D), lambda b, i, k: (b, i, 0)),
        out_shape=jax.ShapeDtypeStruct((bsz, seq_len, D), BF16),
        scratch_shapes=[pltpu.VMEM((tile, D), F32)],
        compiler_params=_params(("parallel", "parallel", "arbitrary")),
        name="seq_dft",
    )(cmat, nsmat, z3, z3)


def _dft_mats(n, scale):
    idx = jnp.arange(n, dtype=jnp.int32)
    ang = ((idx[:, None] * idx[None, :]) % n).astype(F32) * (2.0 * math.pi / n)
    return (jnp.cos(ang) * scale), (jnp.sin(ang) * scale)


FFT_L1 = 32
FFT_GRP = 16
FFT_K1B = 8


def _fft1_body(wa_ref, wb_ref, xc_ref, xs_ref, o_ref):
    rows = FFT_L1 * FFT_GRP
    xc = xc_ref[...].reshape(rows, D)
    xs = xs_ref[...].reshape(rows, D)
    a = (jnp.dot(wa_ref[...], xc, preferred_element_type=F32)
         + jnp.dot(wb_ref[...], xs, preferred_element_type=F32))
    a = a.astype(BF16)
    o_ref[:, :, :D] = a[:rows].reshape(FFT_L1, FFT_GRP, D)
    o_ref[:, :, D:] = a[rows:].reshape(FFT_L1, FFT_GRP, D)


def _fft2_body(m_ref, a_ref, o_ref):
    l2 = m_ref.shape[1]
    for j in range(FFT_K1B):
        o_ref[j] = (jnp.dot(m_ref[j, :, :l2], a_ref[j, :, :D], preferred_element_type=F32)
                    + jnp.dot(m_ref[j, :, l2:], a_ref[j, :, D:], preferred_element_type=F32)
                    ).astype(o_ref.dtype)


def _fft_tables(seq_len):
    l1, l2, g = FFT_L1, seq_len // FFT_L1, FFT_GRP
    i1 = jnp.arange(l1, dtype=jnp.int32)
    ang1 = ((i1[:, None] * i1[None, :]) % l1).astype(F32) * (2.0 * math.pi / l1)
    w1r, w1i = jnp.cos(ang1), -jnp.sin(ang1)
    eye = jnp.eye(g, dtype=F32)
    kron = lambda w: jnp.kron(w, eye)
    wa = jnp.concatenate([kron(w1r), kron(w1i)], axis=0).astype(BF16)
    wb = jnp.concatenate([kron(w1i), kron(-w1r)], axis=0).astype(BF16)
    i2 = jnp.arange(l2, dtype=jnp.int32)
    num = (i2[None, :, None] * i2[None, None, :] * l1 + i1[:, None, None] * i2[None, None, :]) % seq_len
    ang2 = num.astype(F32) * (2.0 * math.pi / seq_len)
    scale = seq_len ** -0.5
    m2 = jnp.concatenate([jnp.cos(ang2) * scale, jnp.sin(ang2) * scale], axis=2).astype(BF16)
    return wa, wb, m2


def _seq_fft(z, bsz, seq_len, tabs):
    wa, wb, m2 = tabs
    l1, l2, g = FFT_L1, seq_len // FFT_L1, FFT_GRP
    z4 = z.reshape(bsz, l1, l2, Z_COLS)
    rows = l1 * g
    a4 = pl.pallas_call(
        _fft1_body,
        grid=(bsz, l2 // g),
        in_specs=[_resident((2 * rows, rows)), _resident((2 * rows, rows)),
                  pl.BlockSpec((None, l1, g, D), lambda b, t: (b, 0, t, ZXC // D)),
                  pl.BlockSpec((None, l1, g, D), lambda b, t: (b, 0, t, ZXS // D))],
        out_specs=pl.BlockSpec((None, l1, g, 2 * D), lambda b, t: (b, 0, t, 0)),
        out_shape=jax.ShapeDtypeStruct((bsz, l1, l2, 2 * D), BF16),
        compiler_params=_params(("parallel", "parallel")),
        name="seq_fft_stage1",
    )(wa, wb, z4, z4)
    yp = pl.pallas_call(
        _fft2_body,
        grid=(l1 // FFT_K1B, bsz),
        in_specs=[pl.BlockSpec((FFT_K1B, l2, 2 * l2), lambda k, b: (k, 0, 0)),
                  pl.BlockSpec((None, FFT_K1B, l2, 2 * D), lambda k, b: (b, k, 0, 0))],
        out_specs=pl.BlockSpec((None, FFT_K1B, l2, D), lambda k, b: (b, k, 0, 0)),
        out_shape=jax.ShapeDtypeStruct((bsz, l1, l2, D), BF16),
        compiler_params=_params(("parallel", "parallel")),
        name="seq_fft_stage2",
    )(m2, a4)
    return yp.transpose(0, 2, 1, 3).reshape(bsz, seq_len, D)


def _gelu_tanh(x):
    return 0.5 * x * (1.0 + jnp.tanh(math.sqrt(2.0 / math.pi) * (x + 0.044715 * (x * x * x))))


MERGE_ROW_SPLIT = 2


def _merge_body(x_ref, mod_ref, attn_ref, gr_ref, hf_ref, hb_ref, four_ref,
                woa_ref, wor_ref, wof_ref, wm_ref, bm_ref, wout_ref, lng_ref, lnb_ref, wr_ref, br_ref,
                x1_ref, u2_ref, eid_ref, gate_ref, cnt_ref):
    mod = mod_ref[...]
    rows = x_ref.shape[0] // MERGE_ROW_SPLIT
    for s in range(MERGE_ROW_SPLIT):
        rs = slice(s * rows, (s + 1) * rows)
        x = x_ref[rs, :]
        u = (x * (1.0 + mod[1:2]) + mod[0:1]).astype(BF16)
        rg_in = (_gelu_tanh(gr_ref[rs, :].astype(F32))
                 * (hf_ref[rs, :].astype(F32) + hb_ref[rs, :].astype(F32))).astype(BF16)
        branches = ((attn_ref[rs, :], woa_ref), (rg_in, wor_ref), (four_ref[rs, :], wof_ref))
        merged = None
        for j, (inp, w_ref) in enumerate(branches):
            y = jnp.dot(inp, w_ref[...], preferred_element_type=F32)
            g = _sigmoid(jnp.dot(u, wm_ref[:, j * D:(j + 1) * D], preferred_element_type=F32)
                         + bm_ref[:, j * D:(j + 1) * D])
            merged = g * y if merged is None else merged + g * y
        y = jnp.dot(merged.astype(BF16), wout_ref[...], preferred_element_type=F32)
        x1 = _layer_norm(DEEPNORM_ALPHA * x + mod[2:3] * y, lng_ref[...], lnb_ref[...])
        x1_ref[rs, :] = x1
        u2_ref[rs, :] = (x1 * (1.0 + mod[4:5]) + mod[3:4]).astype(BF16)

    logits = jnp.dot(u2_ref[...], wr_ref[...], preferred_element_type=F32) + br_ref[...]
    lane = lax.broadcasted_iota(jnp.int32, logits.shape, 1)
    vals, idxs = [], []
    for _ in range(TOP_K):
        m = jnp.max(logits, axis=-1, keepdims=True)
        idx = jnp.min(jnp.where(logits == m, lane, LANES), axis=-1, keepdims=True)
        vals.append(m)
        idxs.append(idx)
        logits = jnp.where(lane == idx, -3.0e38, logits)
    exps = [jnp.exp(v - vals[0]) for v in vals]
    den = exps[0] + exps[1] + exps[2] + exps[3]
    eid = jnp.zeros(lane.shape, jnp.int32)
    gate = jnp.zeros(lane.shape, F32)
    member = jnp.zeros(lane.shape, F32)
    for k in range(TOP_K):
        eid = jnp.where(lane == k, idxs[k], eid)
        gate = jnp.where(lane == k, exps[k] / den, gate)
        member = member + jnp.where(lane == idxs[k], 1.0, 0.0)
    eid_ref[...] = eid
    gate_ref[...] = gate
    cnt_ref[...] = jnp.sum(member, axis=0, keepdims=True)


def _merge(x2, mods, attn, z, hf, hb, four, p, seq_len, tm):
    n = x2.shape[0]
    tpb = seq_len // tm
    bm = mods.shape[0]
    mod_map = (lambda i: (i // tpb, 0, 0)) if bm > 1 else (lambda i: (0, 0, 0))
    act = lambda: pl.BlockSpec((tm, D), lambda i: (i, 0))
    return pl.pallas_call(
        _merge_body,
        grid=(n // tm,),
        in_specs=[act(), pl.BlockSpec((None, 6, D), mod_map), act(),
                  pl.BlockSpec((tm, D), lambda i: (i, ZGR // D)), act(), act(), act(),
                  _resident((D, D)), _resident((D, D)), _resident((D, D)),
                  _resident((D, 3 * D)), _resident((1, 3 * D)), _resident((D, D)),
                  _resident((1, D)), _resident((1, D)), _resident((D, LANES)), _resident((1, LANES))],
        out_specs=[act(), act(), pl.BlockSpec((tm, LANES), lambda i: (i, 0)),
                   pl.BlockSpec((tm, LANES), lambda i: (i, 0)),
                   pl.BlockSpec((None, 1, LANES), lambda i: (i, 0, 0))],
        out_shape=[jax.ShapeDtypeStruct((n, D), F32), jax.ShapeDtypeStruct((n, D), BF16),
                   jax.ShapeDtypeStruct((n, LANES), jnp.int32), jax.ShapeDtypeStruct((n, LANES), F32),
                   jax.ShapeDtypeStruct((n // tm, 1, LANES), F32)],
        compiler_params=_params(("parallel",)),
        name="merge_ln_router",
    )(x2, mods, attn, z, hf, hb, four, p["w_o_attn"], p["w_o_rg"], p["w_o_four"], p["w_merge"],
      p["b_merge"], p["w_out"], p["ln1_g"], p["ln1_b"], p["w_router"], p["b_router"])


RUN_ALIGN = 8


def _sorted_rows(td):
    return TOP_K * td + N_EXPERTS * RUN_ALIGN


def _rows_copy(wait, src_ref, dst_ref, src_off, dst_off, length, sem):
    aligned = lambda v: v if isinstance(v, int) else pl.multiple_of(v, RUN_ALIGN)

    @pl.when(length > 0)
    def _():
        n = aligned(length)
        cp = pltpu.make_async_copy(src_ref.at[pl.ds(aligned(src_off), n)],
                                   dst_ref.at[pl.ds(aligned(dst_off), n)], sem)
        if wait:
            cp.wait()
        else:
            cp.start()


def _slot_positions(eid, toff, ltri):
    lane = lax.broadcasted_iota(jnp.int32, eid.shape, 1)
    hits = [eid[:, k:k + 1] == lane for k in range(TOP_K)]
    member = jnp.zeros(eid.shape, F32)
    for h in hits:
        member = member + jnp.where(h, 1.0, 0.0)
    rank = jnp.dot(ltri, member.astype(BF16), preferred_element_type=F32)
    base = toff + rank
    return [jnp.sum(jnp.where(h, base, 0.0), axis=-1, keepdims=True) for h in hits]


def _dispatch_body(rl_ref, to_ref, ro_ref, tlo_ref, tll_ref, nu_ref, tt_ref, u_ref, eid_ref, toff_ref, ltri_ref,
                   xs_hbm, pos_ref, srt, sems, *, td, tb):
    t = pl.program_id(0)
    n_t = pl.num_programs(0)
    slot = t % 2
    rows = srt.shape[1]

    def start_runs(tile, s):
        def one(e, c):
            i = tile * N_EXPERTS + e
            _rows_copy(False, srt.at[s], xs_hbm, to_ref[i], ro_ref[i], rl_ref[i], sems.at[s])
            return c
        lax.fori_loop(0, N_EXPERTS, one, 0)

    @pl.when(t == 0)
    def _():
        srt[1, :tb, :] = jnp.zeros((tb, D), F32)
        for wait in (False, True):
            def one(e, c):
                _rows_copy(wait, srt.at[1], xs_hbm, 0, tlo_ref[e], tll_ref[e], sems.at[1])
                return c
            lax.fori_loop(0, N_EXPERTS, one, 0)
        n_blocks = xs_hbm.shape[0] // tb
        for wait in (False, True):
            def blk(j, c):
                cp = pltpu.make_async_copy(srt.at[1, pl.ds(0, tb)],
                                           xs_hbm.at[pl.ds(pl.multiple_of(j * tb, RUN_ALIGN), tb)], sems.at[1])
                if wait:
                    cp.wait()
                else:
                    cp.start()
                return c
            lax.fori_loop(nu_ref[0], n_blocks, blk, 0)

    pos = _slot_positions(eid_ref[...], toff_ref[...], ltri_ref[...])
    lane = lax.broadcasted_iota(jnp.int32, (td, LANES), 1)
    pos = [p.astype(jnp.int32) for p in pos]
    pos_out = jnp.zeros((td, LANES), jnp.int32)
    for k in range(TOP_K):
        pos_out = jnp.where(lane == k, pos[k], pos_out)
    pos_ref[...] = pos_out

    col = lax.broadcasted_iota(jnp.int32, (td, rows), 1)
    onehot_t = jnp.zeros((td, rows), F32)
    for k in range(TOP_K):
        onehot_t = jnp.where(col == pos[k], 1.0, onehot_t)
    srt[slot] = lax.dot_general(onehot_t.astype(BF16), u_ref[...], (((0,), (0,)), ((), ())),
                                preferred_element_type=F32)
    start_runs(t, slot)

    def wait_runs(tile, s):
        _rows_copy(True, srt.at[s], xs_hbm, 0, 0, tt_ref[tile], sems.at[s])

    @pl.when(t >= 1)
    def _():
        wait_runs(t - 1, 1 - slot)

    @pl.when(t == n_t - 1)
    def _():
        wait_runs(t, slot)


def _dispatch(u2, eid128, rt, td, tb):
    n_tok = u2.shape[0]
    n_t = n_tok // td
    rows = _sorted_rows(td)
    ltri = (jnp.arange(td)[:, None] > jnp.arange(td)[None, :]).astype(BF16)
    grid_spec = pltpu.PrefetchScalarGridSpec(
        num_scalar_prefetch=7,
        grid=(n_t,),
        in_specs=[pl.BlockSpec((td, D), lambda t, *_: (t, 0)),
                  pl.BlockSpec((td, LANES), lambda t, *_: (t, 0)),
                  pl.BlockSpec((None, 1, LANES), lambda t, *_: (t, 0, 0)),
                  pl.BlockSpec((td, td), lambda t, *_: (0, 0), pipeline_mode=pl.Buffered(1))],
        out_specs=[pl.BlockSpec(memory_space=pl.ANY),
                   pl.BlockSpec((td, LANES), lambda t, *_: (t, 0))],
        scratch_shapes=[pltpu.VMEM((2, rows, D), F32), pltpu.SemaphoreType.DMA((2,))],
    )
    return pl.pallas_call(
        functools.partial(_dispatch_body, td=td, tb=tb),
        grid_spec=grid_spec,
        out_shape=[jax.ShapeDtypeStruct((rt["n_slots"], D), F32),
                   jax.ShapeDtypeStruct((n_tok, LANES), jnp.int32)],
        compiler_params=_params(("arbitrary",)),
        name="moe_dispatch",
    )(rt["run_len"], rt["tile_off"], rt["run_off"], rt["tail_off"], rt["tail_len"], rt["n_used"], rt["tile_tot"],
      u2, eid128, rt["toff_f"], ltri)


MOE_ROW_SPLIT = 2


def _moe_body(be_ref, nused_ref, nb_ref, xs_ref, wgu_hbm, bgu_ref, wdn_hbm, bdn_ref, ys_ref,
              wgu_st, wdn_st, wgu_bf, wdn_bf, sems, *, layer):
    i = pl.program_id(0)
    n_used = nused_ref[0]

    def fetch(e):
        return (pltpu.make_async_copy(wgu_hbm.at[layer, e], wgu_st, sems.at[0]),
                pltpu.make_async_copy(wdn_hbm.at[layer, e], wdn_st, sems.at[1]))

    @pl.when(i == 0)
    def _():
        for cp in fetch(be_ref[0]):
            cp.start()

    @pl.when((i < n_used) & ((i == 0) | (be_ref[i] != be_ref[jnp.maximum(i - 1, 0)])))
    def _():
        e = be_ref[i]
        for cp in fetch(e):
            cp.wait()
        wgu_bf[...] = wgu_st[...].astype(BF16)
        wdn_bf[...] = wdn_st[...].astype(BF16)
        nxt = i + nb_ref[e]

        @pl.when(nxt < n_used)
        def _():
            for cp in fetch(be_ref[nxt]):
                cp.start()

    @pl.when(i < n_used)
    def _():
        rows = xs_ref.shape[0] // MOE_ROW_SPLIT
        for s in range(MOE_ROW_SPLIT):
            rs = slice(s * rows, (s + 1) * rows)
            x = xs_ref[rs, :].astype(BF16)
            h = jnp.dot(x, wgu_bf[...], preferred_element_type=F32) + bgu_ref[...]
            half = h.shape[1] // 2
            x_glu = jnp.minimum(h[:, :half], SWIGLU_LIMIT)
            x_lin = jnp.clip(h[:, half:], -SWIGLU_LIMIT, SWIGLU_LIMIT)
            act = (x_glu * _sigmoid(SWIGLU_ALPHA * x_glu) * (x_lin + 1.0)).astype(BF16)
            ys_ref[rs, :] = jnp.dot(act, wdn_bf[...], preferred_element_type=F32) + bdn_ref[...]

    @pl.when(i >= n_used)
    def _():
        ys_ref[...] = jnp.zeros_like(ys_ref)


def _moe(xs, rt, layer, w_gu, b_gu, w_dn, b_dn, tb):
    n_blocks = xs.shape[0] // tb
    d_ff2 = w_gu.shape[3]
    live = lambda i, nu: jnp.minimum(i, nu[0] - 1)
    grid_spec = pltpu.PrefetchScalarGridSpec(
        num_scalar_prefetch=3,
        grid=(n_blocks,),
        in_specs=[pl.BlockSpec((tb, D), lambda i, be, nu, nb: (live(i, nu), 0)),
                  pl.BlockSpec(memory_space=pl.ANY),
                  pl.BlockSpec((None, None, 1, d_ff2), lambda i, be, nu, nb: (layer, be[live(i, nu)], 0, 0)),
                  pl.BlockSpec(memory_space=pl.ANY),
                  pl.BlockSpec((None, None, 1, D), lambda i, be, nu, nb: (layer, be[live(i, nu)], 0, 0))],
        out_specs=pl.BlockSpec((tb, D), lambda i, be, nu, nb: (i, 0)),
        scratch_shapes=[pltpu.VMEM((D, d_ff2), F32), pltpu.VMEM((d_ff2 // 2, D), F32),
                        pltpu.VMEM((D, d_ff2), BF16), pltpu.VMEM((d_ff2 // 2, D), BF16),
                        pltpu.SemaphoreType.DMA((2,))],
    )
    return pl.pallas_call(
        functools.partial(_moe_body, layer=layer),
        grid_spec=grid_spec,
        out_shape=jax.ShapeDtypeStruct(xs.shape, F32),
        compiler_params=_params(("arbitrary",)),
        name="moe_experts",
    )(rt["block_e"], rt["n_used"], rt["blocks_per_e"], xs, w_gu, b_gu, w_dn, b_dn)


def _route(cnt128, n_tok, td, tb):
    n_t = n_tok // td
    cnt = cnt128[:, 0, :N_EXPERTS].astype(jnp.int32)
    run = (cnt + RUN_ALIGN - 1) // RUN_ALIGN * RUN_ALIGN
    tile_off = jnp.cumsum(run, axis=1) - run
    total = run.sum(axis=0)
    padded = (total + tb - 1) // tb * tb
    pend = jnp.cumsum(padded)
    pstart = pend - padded
    run_off = pstart[None, :] + jnp.cumsum(run, axis=0) - run
    worst = TOP_K * n_tok + n_t * N_EXPERTS * RUN_ALIGN + N_EXPERTS * tb
    n_slots = (worst + tb - 1) // tb * tb
    first_slot = jnp.arange(n_slots // tb, dtype=jnp.int32) * tb
    block_e = jnp.minimum(jnp.sum((pend[None, :] <= first_slot[:, None]).astype(jnp.int32), axis=1),
                          N_EXPERTS - 1)
    toff_f = jnp.zeros((n_t, 1, LANES), F32).at[:, 0, :N_EXPERTS].set(tile_off.astype(F32))
    i32 = lambda a: a.astype(jnp.int32).reshape(-1)
    return dict(run_len=i32(run), tile_off=i32(tile_off), run_off=i32(run_off),
                tail_off=i32(pstart + total), tail_len=i32(padded - total), toff_f=toff_f,
                tile_tot=i32(run.sum(axis=1)), blocks_per_e=i32(padded // tb),
                block_e=i32(block_e), n_used=i32(pend[-1] // tb), n_slots=n_slots)


def _combine_body(rl_ref, to_ref, ro_ref, tt_ref, ys_hbm, x1_ref, mod_ref, gate_ref, pos_ref, lng_ref, lnb_ref,
                  o_ref, srt, sems, *, td):
    t = pl.program_id(0)
    n_t = pl.num_programs(0)
    slot = t % 2
    rows = srt.shape[1]

    def start_runs(tile, s):
        def one(e, c):
            i = tile * N_EXPERTS + e
            _rows_copy(False, ys_hbm, srt.at[s], ro_ref[i], to_ref[i], rl_ref[i], sems.at[s])
            return c
        lax.fori_loop(0, N_EXPERTS, one, 0)

    @pl.when(t == 0)
    def _():
        srt[...] = jnp.zeros_like(srt)
        start_runs(0, 0)

    @pl.when(t + 1 < n_t)
    def _():
        start_runs(t + 1, 1 - slot)

    _rows_copy(True, ys_hbm, srt.at[slot], 0, 0, tt_ref[t], sems.at[slot])

    gate = gate_ref[...]
    pos = pos_ref[...]
    col = lax.broadcasted_iota(jnp.int32, (td, rows), 1)
    weights = jnp.zeros((td, rows), F32)
    for k in range(TOP_K):
        weights = jnp.where(col == pos[:, k:k + 1], gate[:, k:k + 1], weights)
    f = jnp.dot(weights.astype(BF16), srt[slot].astype(BF16), preferred_element_type=F32)
    mod = mod_ref[...]
    o_ref[...] = _layer_norm(DEEPNORM_ALPHA * x1_ref[...] + mod[5:6] * f, lng_ref[...], lnb_ref[...])


def _combine(x1, mods, gate128, pos128, ys, rt, ln_g, ln_b, seq_len, td):
    n = x1.shape[0]
    tpb = seq_len // td
    bm = mods.shape[0]
    mod_map = (lambda i, *_: (i // tpb, 0, 0)) if bm > 1 else (lambda i, *_: (0, 0, 0))
    grid_spec = pltpu.PrefetchScalarGridSpec(
        num_scalar_prefetch=4,
        grid=(n // td,),
        in_specs=[pl.BlockSpec(memory_space=pl.ANY),
                  pl.BlockSpec((td, D), lambda i, *_: (i, 0)),
                  pl.BlockSpec((None, 6, D), mod_map),
                  pl.BlockSpec((td, LANES), lambda i, *_: (i, 0)),
                  pl.BlockSpec((td, LANES), lambda i, *_: (i, 0)),
                  pl.BlockSpec((1, D), lambda i, *_: (0, 0)),
                  pl.BlockSpec((1, D), lambda i, *_: (0, 0))],
        out_specs=pl.BlockSpec((td, D), lambda i, *_: (i, 0)),
        scratch_shapes=[pltpu.VMEM((2, _sorted_rows(td), D), F32), pltpu.SemaphoreType.DMA((2,))],
    )
    return pl.pallas_call(
        functools.partial(_combine_body, td=td),
        grid_spec=grid_spec,
        out_shape=jax.ShapeDtypeStruct((n, D), F32),
        compiler_params=_params(("arbitrary",)),
        name="combine_ln",
    )(rt["run_len"], rt["tile_off"], rt["run_off"], rt["tile_tot"], ys, x1, mods, gate128, pos128,
      ln_g, ln_b)


def _rope_tables(seq_len):
    pos = jnp.arange(seq_len, dtype=jnp.int32)
    row = (pos // GRID_W).astype(F32)
    col = (pos % GRID_W).astype(F32)
    n_freq = HEAD_DIM // 4
    freqs = ROPE_BASE ** (-jnp.arange(n_freq, dtype=F32) / n_freq)
    ar, ac = row[:, None] * freqs, col[:, None] * freqs
    zero = jnp.zeros_like(ar)
    c = jnp.concatenate([jnp.cos(ar), jnp.cos(ar), jnp.cos(ac), jnp.cos(ac)], axis=1)
    sa = jnp.concatenate([-jnp.sin(ar), zero, -jnp.sin(ac), zero], axis=1)
    sb = jnp.concatenate([zero, jnp.sin(ar), zero, jnp.sin(ac)], axis=1)
    rep = MXU_N // HEAD_DIM
    return tuple(jnp.tile(t, (1, rep)) for t in (c, sa, sb))


def _layer_params(l, w):
    w_router = jnp.zeros((D, LANES), BF16).at[:, :N_EXPERTS].set(w["w_router"][l].astype(BF16))
    b_router = jnp.full((1, LANES), NEG_INF, F32).at[0, :N_EXPERTS].set(w["b_router"][l])
    return dict(
        w_in=w["w_in"][l].astype(BF16),
        w_o_attn=w["w_o_attn"][l].astype(BF16), w_o_rg=w["w_o_rg"][l].astype(BF16),
        w_o_four=w["w_o_four"][l].astype(BF16), w_merge=w["w_merge"][l].astype(BF16),
        b_merge=w["b_merge"][l].reshape(1, 3 * D), w_out=w["w_out"][l].astype(BF16),
        ln1_g=w["ln1_g"][l].reshape(1, D), ln1_b=w["ln1_b"][l].reshape(1, D),
        ln2_g=w["ln2_g"][l].reshape(1, D), ln2_b=w["ln2_b"][l].reshape(1, D),
        w_router=w_router, b_router=b_router,
        layer=l, w_gu=w["w_gate_up"], b_gu=w["b_gate_up"].reshape(DEPTH, N_EXPERTS, 1, -1),
        w_dn=w["w_down"], b_dn=w["b_down"].reshape(DEPTH, N_EXPERTS, 1, D),
        wg=_gate_weights(w["rg_w_a"][l], w["rg_w_i"][l]),
        conv_w=w["conv_w"][l], conv_b=w["conv_b"][l], b_a=w["rg_b_a"][l], b_i=w["rg_b_i"][l],
        lam=w["rg_lambda"][l], sink=w["attn_sink"][l],
    )


def _row_tile(seq_len, want):
    return min(want, seq_len)


def _ffn(x1, u2, eid128, gate128, cnt128, mods, p, seq_len, tb, td):
    rt = _route(cnt128, x1.shape[0], td, tb)
    xs, pos128 = _dispatch(u2, eid128, rt, td, tb)
    ys = _moe(xs, rt, p["layer"], p["w_gu"], p["b_gu"], p["w_dn"], p["b_dn"], tb)
    return _combine(x1, mods, gate128, pos128, ys, rt, p["ln2_g"], p["ln2_b"], seq_len, td)


def kernel(x, c, ctx, c_ctx, w_mod, b_mod, w_in, attn_sink, w_o_attn, conv_w, conv_b, rg_w_a, rg_b_a, rg_w_i, rg_b_i, rg_lambda, w_o_rg, w_o_four, w_merge, b_merge, w_out, ln1_g, ln1_b, w_router, b_router, w_gate_up, b_gate_up, w_down, b_down, ln2_g, ln2_b):
    w = dict(w_mod=w_mod, b_mod=b_mod, w_in=w_in, attn_sink=attn_sink, w_o_attn=w_o_attn, conv_w=conv_w,
             conv_b=conv_b, rg_w_a=rg_w_a, rg_b_a=rg_b_a, rg_w_i=rg_w_i, rg_b_i=rg_b_i, rg_lambda=rg_lambda,
             w_o_rg=w_o_rg, w_o_four=w_o_four, w_merge=w_merge, b_merge=b_merge, w_out=w_out, ln1_g=ln1_g,
             ln1_b=ln1_b, w_router=w_router, b_router=b_router, w_gate_up=w_gate_up, b_gate_up=b_gate_up,
             w_down=w_down, b_down=b_down, ln2_g=ln2_g, ln2_b=ln2_b)
    bsz, seq_len, _ = x.shape
    ctx_len = ctx.shape[1]
    n_lat, n_ctx = bsz * seq_len, bsz * ctx_len
    tm_lat, tm_ctx = _row_tile(seq_len, 512), _row_tile(ctx_len, 256)
    dft_lat, dft_ctx = _row_tile(seq_len, 1024), _row_tile(ctx_len, 256)
    rg_lat, rg_ctx = _row_tile(seq_len, 1024), _row_tile(ctx_len, 256)
    tb_lat, tb_ctx = 512, 128

    rope = _rope_tables(seq_len)
    cw, sw = _dft_mats(F_GW, F_GW ** -0.5)
    csw = jnp.concatenate([cw, sw], axis=1).astype(BF16)
    use_fft = seq_len % (FFT_L1 * FFT_GRP) == 0
    if use_fft:
        fft_tabs = _fft_tables(seq_len)
    else:
        cl, sl = _dft_mats(seq_len, seq_len ** -0.5)
        cl, nsl = cl.astype(BF16), (-sl).astype(BF16)
    cc, sc = _dft_mats(ctx_len, ctx_len ** -0.5)
    cc, nsc = cc.astype(BF16), (-sc).astype(BF16)

    cond = jnp.zeros((16, D), F32).at[:bsz].set(c).at[bsz].set(c_ctx)
    x2 = x.reshape(n_lat, D)
    ctx2 = ctx.reshape(n_ctx, D)
    zero_h = jnp.zeros((bsz, 1, D), F32)

    for l in range(DEPTH):
        last = l == DEPTH - 1
        p = _layer_params(l, w)
        m = _adaln(cond, w_mod[l], b_mod[l]).reshape(16, 6, D)
        mods_lat, mods_ctx = m[:bsz], m[bsz:bsz + 1]

        zc = _proj(ctx2, mods_ctx, p["w_in"], csw, None, ctx_len, tm_ctx)
        z = _proj(x2, mods_lat, p["w_in"], csw, rope, seq_len, tm_lat)
        zc3 = zc.reshape(bsz, ctx_len, Z_COLS)
        z3 = z.reshape(bsz, seq_len, Z_COLS)

        rg = lambda zz, h0, d, tile: _rglru(zz, p["conv_w"], p["conv_b"], p["wg"], p["b_a"], p["b_i"],
                                            p["lam"], h0, direction=d, tile=tile)
        hcf, endf = rg(zc3, zero_h, 0, rg_ctx)
        hcb, endb = rg(zc3, zero_h, 1, rg_ctx)
        hf, _ = rg(z3, endf, 0, rg_lat)
        hb, _ = rg(z3, endb, 1, rg_lat)

        attn = _attention(z3, zc3, p["sink"])
        four = _seq_fft(z, bsz, seq_len, fft_tabs) if use_fft else _seq_dft(z3, cl, nsl, dft_lat)
        x1, u2, eid, gate, cnt = _merge(x2, mods_lat, attn.reshape(n_lat, D), z, hf.reshape(n_lat, D),
                                   hb.reshape(n_lat, D), four.reshape(n_lat, D), p, seq_len, tm_lat)
        x2 = _ffn(x1, u2, eid, gate, cnt, mods_lat, p, seq_len, tb_lat, tm_lat)

        if not last:
            attn_c = _ctx_attention(zc3, p["sink"])
            four_c = _seq_dft(zc3, cc, nsc, dft_ctx)
            c1, uc2, eid_c, gate_c, cnt_c = _merge(ctx2, mods_ctx, attn_c.reshape(n_ctx, D), zc,
                                            hcf.reshape(n_ctx, D), hcb.reshape(n_ctx, D),
                                            four_c.reshape(n_ctx, D), p, ctx_len, tm_ctx)
            ctx2 = _ffn(c1, uc2, eid_c, gate_c, cnt_c, mods_ctx, p, ctx_len, tb_ctx, tm_ctx)

    return x2.reshape(bsz, seq_len, D)
```

```python
import functools
import math

import jax
import jax.numpy as jnp
from jax import lax
from jax.experimental import pallas as pl
from jax.experimental.pallas import tpu as pltpu

F32 = jnp.float32
BF16 = jnp.bfloat16

D = 1024
HEAD_DIM = 64
N_HEADS = 16
N_KV = 4
Q_PER_KV = 4
WINDOW = 128
QBLK = 128
assert WINDOW == QBLK
GRID_W = 64
ROPE_BASE = 10000.0
RG_C = 8.0
CONV_LEFT = 2
F_GROUPS = 4
F_GW = 256
N_EXPERTS = 32
TOP_K = 4
SWIGLU_LIMIT = 7.0
SWIGLU_ALPHA = 1.702
LN_EPS = 1e-5
DEPTH = 2
DEEPNORM_ALPHA = (2 * DEPTH) ** 0.25
NEG_INF = -1e30

Q_OFF, K_OFF, V_OFF, XR_OFF, GR_OFF, XF_OFF, IN_COLS = 0, 1024, 1280, 1536, 2560, 3584, 4608
ZQ, ZXR, ZGR, ZXC, ZXS, ZK, ZV, Z_COLS = 0, 1024, 2048, 3072, 4096, 5120, 5376, 5632

VMEM_LIMIT_V7X = 56 * 1024 * 1024
LANES = 128
MXU_N = 256


def _params(sem, vmem=VMEM_LIMIT_V7X):
    return pltpu.CompilerParams(dimension_semantics=sem, vmem_limit_bytes=vmem)


def _resident(shape):
    nd = len(shape)
    return pl.BlockSpec(shape, lambda *_: (0,) * nd, pipeline_mode=pl.Buffered(1))


def _sigmoid(x):
    return 0.5 * jnp.tanh(0.5 * x) + 0.5


def _layer_norm(r, g, b):
    mu = jnp.mean(r, axis=-1, keepdims=True)
    rc = r - mu
    var = jnp.mean(rc * rc, axis=-1, keepdims=True)
    return rc * lax.rsqrt(var + LN_EPS) * g + b


def _adaln_body(c_ref, w_ref, b_ref, o_ref):
    c = c_ref[...]
    s = (c * _sigmoid(c)).astype(BF16)
    o_ref[...] = jnp.dot(s, w_ref[...].astype(BF16), preferred_element_type=F32) + b_ref[...]


def _adaln(cond, w_mod, b_mod):
    rows, n = cond.shape[0], w_mod.shape[1]
    tn = 1024
    return pl.pallas_call(
        _adaln_body,
        grid=(n // tn,),
        in_specs=[pl.BlockSpec((rows, D), lambda j: (0, 0)),
                  pl.BlockSpec((D, tn), lambda j: (0, j)),
                  pl.BlockSpec((1, tn), lambda j: (0, j))],
        out_specs=pl.BlockSpec((rows, tn), lambda j: (0, j)),
        out_shape=jax.ShapeDtypeStruct((rows, n), F32),
        compiler_params=_params(("arbitrary",)),
        name="adaln",
    )(cond, w_mod, b_mod.reshape(1, n))


def _proj_dst(j):
    if j < 4:
        return ZQ + j * MXU_N
    if j == 4:
        return ZK
    if j == 5:
        return ZV
    if j < 10:
        return ZXR + (j - 6) * MXU_N
    if j < 14:
        return ZGR + (j - 10) * MXU_N
    return None


def _proj_body(x_ref, mod_ref, w_ref, csw_ref, *rest, rope):
    if rope:
        c_ref, sa_ref, sb_ref, o_ref = rest
    else:
        (o_ref,) = rest
    mod = mod_ref[...]
    u = (x_ref[...] * (1.0 + mod[1:2]) + mod[0:1]).astype(BF16)
    for j in range(IN_COLS // MXU_N):
        acc = jnp.dot(u, w_ref[:, j * MXU_N:(j + 1) * MXU_N], preferred_element_type=F32)
        if rope and j < 5:
            acc = (acc * c_ref[...] + pltpu.roll(acc, MXU_N - 16, 1) * sa_ref[...]
                   + pltpu.roll(acc, 16, 1) * sb_ref[...])
        dst = _proj_dst(j)
        if dst is not None:
            o_ref[:, dst:dst + MXU_N] = acc.astype(BF16)
        else:
            g = j - 14
            t = jnp.dot(acc.astype(BF16), csw_ref[...], preferred_element_type=F32)
            o_ref[:, ZXC + g * F_GW:ZXC + (g + 1) * F_GW] = t[:, :F_GW].astype(BF16)
            o_ref[:, ZXS + g * F_GW:ZXS + (g + 1) * F_GW] = t[:, F_GW:].astype(BF16)


def _proj(x2, mods, w_in, csw, rope_tabs, seq_len, tm):
    n = x2.shape[0]
    tpb = seq_len // tm
    bm = mods.shape[0]
    mod_map = (lambda i: (i // tpb, 0, 0)) if bm > 1 else (lambda i: (0, 0, 0))
    in_specs = [pl.BlockSpec((tm, D), lambda i: (i, 0)),
                pl.BlockSpec((None, 6, D), mod_map),
                _resident((D, IN_COLS)),
                _resident((F_GW, 2 * F_GW))]
    args = [x2, mods, w_in, csw]
    if rope_tabs is not None:
        in_specs += [pl.BlockSpec((tm, MXU_N), lambda i: (i % tpb, 0))] * 3
        args += list(rope_tabs)
    return pl.pallas_call(
        functools.partial(_proj_body, rope=rope_tabs is not None),
        grid=(n // tm,),
        in_specs=in_specs,
        out_specs=pl.BlockSpec((tm, Z_COLS), lambda i: (i, 0)),
        out_shape=jax.ShapeDtypeStruct((n, Z_COLS), BF16),
        compiler_params=_params(("parallel",)),
        name="proj_rope" if rope_tabs is not None else "proj_ctx",
    )(*args)


def _group_attention(sink_ref, q_ref, o_ref, g, parts):
    rows = q_ref.shape[0]
    key_low = g % 2 == 0
    lane = lax.broadcasted_iota(jnp.int32, (1, LANES), 1)
    keep = (lane < HEAD_DIM) if key_low else (lane >= HEAD_DIM)
    kv_lanes = slice((g // 2) * LANES, (g // 2 + 1) * LANES)

    blocks = []
    for j in range(2):
        q32 = q_ref[:, (2 * g + j) * LANES:(2 * g + j + 1) * LANES].astype(F32) * (HEAD_DIM ** -0.5)
        same, rot = q32.astype(BF16), pltpu.roll(q32, HEAD_DIM, 1).astype(BF16)
        blocks += [same, rot] if key_low else [rot, same]
    qs = jnp.concatenate(blocks, axis=0)

    row = lax.broadcasted_iota(jnp.int32, (Q_PER_KV * rows, 1), 0)
    sink = jnp.full((Q_PER_KV * rows, 1), sink_ref[g * Q_PER_KV + Q_PER_KV - 1], F32)
    for r in range(Q_PER_KV - 2, -1, -1):
        sink = jnp.where(row < (r + 1) * rows, sink_ref[g * Q_PER_KV + r], sink)

    def lane_blocks(a):
        return [a[:, c * LANES:(c + 1) * LANES] for c in range(a.shape[1] // LANES)]

    scores = []
    m_fold = None
    for k_ref, _, bias in parts:
        k2 = jnp.where(keep, k_ref[:, kv_lanes], jnp.zeros((), BF16))
        s = lax.dot_general(qs, k2, (((1,), (1,)), ((), ())), preferred_element_type=F32)
        if bias is not None:
            s = s + bias
        scores.append(s)
        for blk in lane_blocks(s):
            m_fold = blk if m_fold is None else jnp.maximum(m_fold, blk)
    m = jnp.maximum(sink, jnp.max(m_fold, axis=-1, keepdims=True))
    p_fold = None
    out = None
    for s, (_, v_ref, _) in zip(scores, parts):
        p = jnp.exp(s - m)
        for blk in lane_blocks(p):
            p_fold = blk if p_fold is None else p_fold + blk
        v2 = jnp.where(keep, v_ref[:, kv_lanes], jnp.zeros((), BF16))
        o = jnp.dot(p.astype(BF16), v2, preferred_element_type=F32)
        out = o if out is None else out + o
    den = jnp.exp(sink - m) + jnp.sum(p_fold, axis=-1, keepdims=True)
    out = out / den
    for j in range(2):
        first, second = out[2 * j * rows:(2 * j + 1) * rows], out[(2 * j + 1) * rows:(2 * j + 2) * rows]
        if key_low:
            both = first + pltpu.roll(second, HEAD_DIM, 1)
        else:
            both = pltpu.roll(first, HEAD_DIM, 1) + second
        o_ref[:, (2 * g + j) * LANES:(2 * g + j + 1) * LANES] = both.astype(o_ref.dtype)


def _attn_body(sink_ref, q_ref, kl_ref, km_ref, kr_ref, vl_ref, vm_ref, vr_ref, kc_ref, vc_ref,
               o_ref, *, n_blk):
    n = pl.program_id(1)
    i = lax.broadcasted_iota(jnp.int32, (Q_PER_KV * QBLK, QBLK), 0) % QBLK
    j = lax.broadcasted_iota(jnp.int32, (Q_PER_KV * QBLK, QBLK), 1)
    bias_prev = jnp.where((j >= i) & (n > 0), 0.0, NEG_INF).astype(F32)
    bias_next = jnp.where((j <= i) & (n < n_blk - 1), 0.0, NEG_INF).astype(F32)
    parts = ((kl_ref, vl_ref, bias_prev), (km_ref, vm_ref, None), (kr_ref, vr_ref, bias_next),
             (kc_ref, vc_ref, None))
    for g in range(N_KV):
        _group_attention(sink_ref, q_ref, o_ref, g, parts)


def _attention(z3, zc3, sink):
    bsz, seq_len, _ = z3.shape
    ctx_len = zc3.shape[1]
    nb = seq_len // QBLK
    kcol, vcol = ZK // MXU_N, ZV // MXU_N

    def blk(col, off):
        return pl.BlockSpec((None, QBLK, MXU_N),
                            lambda b, n: (b, jnp.clip(n + off, 0, nb - 1), col))

    return pl.pallas_call(
        functools.partial(_attn_body, n_blk=nb),
        grid=(bsz, nb),
        in_specs=[pl.BlockSpec(memory_space=pltpu.SMEM),
                  pl.BlockSpec((None, QBLK, D), lambda b, n: (b, n, 0)),
                  blk(kcol, -1), blk(kcol, 0), blk(kcol, 1),
                  blk(vcol, -1), blk(vcol, 0), blk(vcol, 1),
                  pl.BlockSpec((None, ctx_len, MXU_N), lambda b, n: (b, 0, kcol)),
                  pl.BlockSpec((None, ctx_len, MXU_N), lambda b, n: (b, 0, vcol))],
        out_specs=pl.BlockSpec((None, QBLK, D), lambda b, n: (b, n, 0)),
        out_shape=jax.ShapeDtypeStruct((bsz, seq_len, D), BF16),
        compiler_params=_params(("parallel", "parallel")),
        name="window_attn",
    )(sink, z3, z3, z3, z3, z3, z3, z3, zc3, zc3)


def _ctx_attn_body(sink_ref, q_ref, kc_ref, vc_ref, o_ref):
    for g in range(N_KV):
        _group_attention(sink_ref, q_ref, o_ref, g, ((kc_ref, vc_ref, None),))


def _ctx_attention(zc3, sink):
    bsz, ctx_len, _ = zc3.shape
    kcol, vcol = ZK // MXU_N, ZV // MXU_N
    return pl.pallas_call(
        _ctx_attn_body,
        grid=(bsz,),
        in_specs=[pl.BlockSpec(memory_space=pltpu.SMEM),
                  pl.BlockSpec((None, ctx_len, D), lambda b: (b, 0, 0)),
                  pl.BlockSpec((None, ctx_len, MXU_N), lambda b: (b, 0, kcol)),
                  pl.BlockSpec((None, ctx_len, MXU_N), lambda b: (b, 0, vcol))],
        out_specs=pl.BlockSpec((None, ctx_len, D), lambda b: (b, 0, 0)),
        out_shape=jax.ShapeDtypeStruct((bsz, ctx_len, D), BF16),
        compiler_params=_params(("parallel",)),
        name="ctx_attn",
    )(sink, zc3, zc3, zc3)


RG_CB = 1024
RG_HALO = 16


def _rglru_body(x_ref, xp_ref, xn_ref, cw_ref, cb_ref, wg_ref, ba_ref, bi_ref, lam_ref, h0_ref,
                h_ref, hend_ref, a_scr, b_scr, xc_scr, carry_scr, *, reverse, n_t):
    t = pl.program_id(2)
    t_idx = (n_t - 1 - t) if reverse else t
    rows = x_ref.shape[0]
    n_grp = rows // 8
    x = x_ref[...].astype(F32)

    cw = cw_ref[...]

    def taps(xm2, xm1, x0, xp1):
        return cb_ref[...] + xm2 * cw[0:1] + xm1 * cw[1:2] + x0 * cw[2:3] + xp1 * cw[3:4]

    x_m1, x_m2, x_p1 = pltpu.roll(x, 1, 0), pltpu.roll(x, 2, 0), pltpu.roll(x, rows - 1, 0)
    xc_scr[...] = taps(x_m2, x_m1, x, x_p1)
    prev = xp_ref[...].astype(F32) * jnp.where(t_idx > 0, 1.0, 0.0)
    nxt = xn_ref[...].astype(F32) * jnp.where(t_idx < n_t - 1, 1.0, 0.0)
    p2, p1, n0 = prev[RG_HALO - 2:RG_HALO - 1], prev[RG_HALO - 1:RG_HALO], nxt[0:1]
    r8 = lax.broadcasted_iota(jnp.int32, (8, RG_CB), 0)
    head, tail = x[0:8], x[rows - 8:rows]
    h_m1 = jnp.where(r8 == 0, p1, pltpu.roll(head, 1, 0))
    h_m2 = jnp.where(r8 == 0, p2, jnp.where(r8 == 1, p1, pltpu.roll(head, 2, 0)))
    t_p1 = jnp.where(r8 == 7, n0, pltpu.roll(tail, 7, 0))
    xc_scr[0:8, :] = taps(h_m2, h_m1, head, x_p1[0:8])
    xc_scr[rows - 8:rows, :] = taps(x_m2[rows - 8:rows], x_m1[rows - 8:rows], tail, t_p1)

    lam = lam_ref[...]
    softplus_neg_lam = jnp.maximum(-lam, 0.0) + jnp.log1p(jnp.exp(-jnp.abs(lam)))
    sub = lax.broadcasted_iota(jnp.int32, (n_grp, 8, LANES), 1)
    for j in range(RG_CB // LANES):
        sl = slice(j * LANES, (j + 1) * LANES)
        xj = xc_scr[:, sl]
        gates = jnp.dot(xj.astype(BF16), wg_ref[j], preferred_element_type=F32)
        r = _sigmoid(gates[:, :LANES] + ba_ref[:, sl])
        ig = _sigmoid(gates[:, LANES:] + bi_ref[:, sl])
        log_a = -RG_C * r * softplus_neg_lam[:, sl]
        a = jnp.exp(log_a)
        th = jnp.tanh(log_a)
        b = jnp.sqrt(-2.0 * th / (1.0 - th)) * (ig * xj)
        a = a.reshape(n_grp, 8, LANES)
        b = b.reshape(n_grp, 8, LANES)
        for d in (1, 2, 4):
            shift = 8 - d if reverse else d
            ok = (sub < 8 - d) if reverse else (sub >= d)
            a_s, b_s = pltpu.roll(a, shift, 1), pltpu.roll(b, shift, 1)
            b = jnp.where(ok, a * b_s + b, b)
            a = jnp.where(ok, a * a_s, a)
        a_scr[:, sl] = a.reshape(rows, LANES)
        b_scr[:, sl] = b.reshape(rows, LANES)

    @pl.when(t == 0)
    def _():
        carry_scr[...] = jnp.broadcast_to(h0_ref[...], (8, RG_CB))

    def group(i, carry):
        g = (n_grp - 1 - i) if reverse else i
        off = pl.multiple_of(g * 8, 8)
        h = a_scr[pl.ds(off, 8), :] * carry + b_scr[pl.ds(off, 8), :]
        b_scr[pl.ds(off, 8), :] = h
        last = h[0:1] if reverse else h[7:8]
        return jnp.broadcast_to(last, (8, RG_CB))

    carry = lax.fori_loop(0, n_grp, group, carry_scr[...])
    carry_scr[...] = carry
    h_ref[...] = b_scr[...].astype(h_ref.dtype)
    hend_ref[...] = carry[0:1]


def _rglru(z3, conv_w, conv_b, wg, b_a, b_i, lam, h0, *, direction, tile):
    bsz, seq_len, _ = z3.shape
    n_t = seq_len // tile
    reverse = direction == 1
    n_cb = D // RG_CB
    xcol = ZXR // RG_CB
    hpt = tile // RG_HALO
    n_halo = seq_len // RG_HALO

    def tix(t):
        return (n_t - 1 - t) if reverse else t

    vec = lambda: pl.BlockSpec((None, 1, RG_CB), lambda b, c, t: (direction, 0, c))
    return pl.pallas_call(
        functools.partial(_rglru_body, reverse=reverse, n_t=n_t),
        grid=(bsz, n_cb, n_t),
        in_specs=[
            pl.BlockSpec((None, tile, RG_CB), lambda b, c, t: (b, tix(t), xcol + c)),
            pl.BlockSpec((None, RG_HALO, RG_CB),
                         lambda b, c, t: (b, jnp.maximum(tix(t) * hpt - 1, 0), xcol + c)),
            pl.BlockSpec((None, RG_HALO, RG_CB),
                         lambda b, c, t: (b, jnp.minimum((tix(t) + 1) * hpt, n_halo - 1), xcol + c)),
            pl.BlockSpec((4, RG_CB), lambda b, c, t: (0, c)),
            pl.BlockSpec((1, RG_CB), lambda b, c, t: (0, c)),
            pl.BlockSpec((None, RG_CB // LANES, LANES, 2 * LANES), lambda b, c, t: (direction, c, 0, 0)),
            vec(), vec(), vec(),
            pl.BlockSpec((None, 1, RG_CB), lambda b, c, t: (b, 0, c)),
        ],
        out_specs=[pl.BlockSpec((None, tile, RG_CB), lambda b, c, t: (b, tix(t), c)),
                   pl.BlockSpec((None, 1, RG_CB), lambda b, c, t: (b, 0, c))],
        out_shape=[jax.ShapeDtypeStruct((bsz, seq_len, D), BF16),
                   jax.ShapeDtypeStruct((bsz, 1, D), F32)],
        scratch_shapes=[pltpu.VMEM((tile, RG_CB), F32), pltpu.VMEM((tile, RG_CB), F32),
                        pltpu.VMEM((tile, RG_CB), F32), pltpu.VMEM((8, RG_CB), F32)],
        compiler_params=_params(("parallel", "parallel", "arbitrary")),
        name="rglru_bwd" if reverse else "rglru_fwd",
    )(z3, z3, z3, conv_w, conv_b.reshape(1, D), wg, b_a.reshape(2, 1, D), b_i.reshape(2, 1, D),
      lam.reshape(2, 1, D), h0)


def _gate_weights(w_a, w_i):
    def pair(w):
        w = w.reshape(2, 8, 2, 64, 64)
        z = jnp.zeros_like(w[:, :, 0])
        top = jnp.concatenate([w[:, :, 0], z], axis=-1)
        bot = jnp.concatenate([z, w[:, :, 1]], axis=-1)
        return jnp.concatenate([top, bot], axis=-2)
    return jnp.concatenate([pair(w_a), pair(w_i)], axis=-1).astype(BF16)


def _seqdft_body(c_ref, s_ref, xc_ref, xs_ref, o_ref, acc_ref):
    k = pl.program_id(2)

    @pl.when(k == 0)
    def _():
        acc_ref[...] = jnp.zeros_like(acc_ref)

    acc_ref[...] += (jnp.dot(c_ref[...], xc_ref[...], preferred_element_type=F32)
                     + jnp.dot(s_ref[...], xs_ref[...], preferred_element_type=F32))

    @pl.when(k == pl.num_programs(2) - 1)
    def _():
        o_ref[...] = acc_ref[...].astype(o_ref.dtype)


def _seq_dft(z3, cmat, nsmat, tile):
    bsz, seq_len, _ = z3.shape
    nt = seq_len // tile
    return pl.pallas_call(
        _seqdft_body,
        grid=(bsz, nt, nt),
        in_specs=[pl.BlockSpec((tile, tile), lambda b, i, k: (i, k)),
                  pl.BlockSpec((tile, tile), lambda b, i, k: (i, k)),
                  pl.BlockSpec((None, tile, D), lambda b, i, k: (b, k, ZXC // D)),
                  pl.BlockSpec((None, tile, D), lambda b, i, k: (b, k, ZXS // D))],
        out_specs=pl.BlockSpec((None, tile, D), lambda b, i, k: (b, i, 0)),
        out_shape=jax.ShapeDtypeStruct((bsz, seq_len, D), BF16),
        scratch_shapes=[pltpu.VMEM((tile, D), F32)],
        compiler_params=_params(("parallel", "parallel", "arbitrary")),
        name="seq_dft",
    )(cmat, nsmat, z3, z3)


def _dft_mats(n, scale):
    idx = jnp.arange(n, dtype=jnp.int32)
    ang = ((idx[:, None] * idx[None, :]) % n).astype(F32) * (2.0 * math.pi / n)
    return (jnp.cos(ang) * scale), (jnp.sin(ang) * scale)


FFT_L1 = 32
FFT_GRP = 16
FFT_K1B = 8


def _fft1_body(wa_ref, wb_ref, xc_ref, xs_ref, o_ref):
    rows = FFT_L1 * FFT_GRP
    xc = xc_ref[...].reshape(rows, D)
    xs = xs_ref[...].reshape(rows, D)
    a = (jnp.dot(wa_ref[...], xc, preferred_element_type=F32)
         + jnp.dot(wb_ref[...], xs, preferred_element_type=F32))
    a = a.astype(BF16)
    o_ref[:, :, :D] = a[:rows].reshape(FFT_L1, FFT_GRP, D)
    o_ref[:, :, D:] = a[rows:].reshape(FFT_L1, FFT_GRP, D)


def _fft2_body(m_ref, a_ref, o_ref):
    l2 = m_ref.shape[1]
    for j in range(FFT_K1B):
        o_ref[j] = (jnp.dot(m_ref[j, :, :l2], a_ref[j, :, :D], preferred_element_type=F32)
                    + jnp.dot(m_ref[j, :, l2:], a_ref[j, :, D:], preferred_element_type=F32)
                    ).astype(o_ref.dtype)


def _fft_tables(seq_len):
    l1, l2, g = FFT_L1, seq_len // FFT_L1, FFT_GRP
    i1 = jnp.arange(l1, dtype=jnp.int32)
    ang1 = ((i1[:, None] * i1[None, :]) % l1).astype(F32) * (2.0 * math.pi / l1)
    w1r, w1i = jnp.cos(ang1), -jnp.sin(ang1)
    eye = jnp.eye(g, dtype=F32)
    kron = lambda w: jnp.kron(w, eye)
    wa = jnp.concatenate([kron(w1r), kron(w1i)], axis=0).astype(BF16)
    wb = jnp.concatenate([kron(w1i), kron(-w1r)], axis=0).astype(BF16)
    i2 = jnp.arange(l2, dtype=jnp.int32)
    num = (i2[None, :, None] * i2[None, None, :] * l1 + i1[:, None, None] * i2[None, None, :]) % seq_len
    ang2 = num.astype(F32) * (2.0 * math.pi / seq_len)
    scale = seq_len ** -0.5
    m2 = jnp.concatenate([jnp.cos(ang2) * scale, jnp.sin(ang2) * scale], axis=2).astype(BF16)
    return wa, wb, m2


def _seq_fft(z, bsz, seq_len, tabs):
    wa, wb, m2 = tabs
    l1, l2, g = FFT_L1, seq_len // FFT_L1, FFT_GRP
    z4 = z.reshape(bsz, l1, l2, Z_COLS)
    rows = l1 * g
    a4 = pl.pallas_call(
        _fft1_body,
        grid=(bsz, l2 // g),
        in_specs=[_resident((2 * rows, rows)), _resident((2 * rows, rows)),
                  pl.BlockSpec((None, l1, g, D), lambda b, t: (b, 0, t, ZXC // D)),
                  pl.BlockSpec((None, l1, g, D), lambda b, t: (b, 0, t, ZXS // D))],
        out_specs=pl.BlockSpec((None, l1, g, 2 * D), lambda b, t: (b, 0, t, 0)),
        out_shape=jax.ShapeDtypeStruct((bsz, l1, l2, 2 * D), BF16),
        compiler_params=_params(("parallel", "parallel")),
        name="seq_fft_stage1",
    )(wa, wb, z4, z4)
    yp = pl.pallas_call(
        _fft2_body,
        grid=(l1 // FFT_K1B, bsz),
        in_specs=[pl.BlockSpec((FFT_K1B, l2, 2 * l2), lambda k, b: (k, 0, 0)),
                  pl.BlockSpec((None, FFT_K1B, l2, 2 * D), lambda k, b: (b, k, 0, 0))],
        out_specs=pl.BlockSpec((None, FFT_K1B, l2, D), lambda k, b: (b, k, 0, 0)),
        out_shape=jax.ShapeDtypeStruct((bsz, l1, l2, D), BF16),
        compiler_params=_params(("parallel", "parallel")),
        name="seq_fft_stage2",
    )(m2, a4)
    return yp.transpose(0, 2, 1, 3).reshape(bsz, seq_len, D)


def _gelu_tanh(x):
    return 0.5 * x * (1.0 + jnp.tanh(math.sqrt(2.0 / math.pi) * (x + 0.044715 * (x * x * x))))


MERGE_ROW_SPLIT = 2


def _merge_body(x_ref, mod_ref, attn_ref, gr_ref, hf_ref, hb_ref, four_ref,
                woa_ref, wor_ref, wof_ref, wm_ref, bm_ref, wout_ref, lng_ref, lnb_ref, wr_ref, br_ref,
                x1_ref, u2_ref, eid_ref, gate_ref, cnt_ref):
    mod = mod_ref[...]
    rows = x_ref.shape[0] // MERGE_ROW_SPLIT
    for s in range(MERGE_ROW_SPLIT):
        rs = slice(s * rows, (s + 1) * rows)
        x = x_ref[rs, :]
        u = (x * (1.0 + mod[1:2]) + mod[0:1]).astype(BF16)
        rg_in = (_gelu_tanh(gr_ref[rs, :].astype(F32))
                 * (hf_ref[rs, :].astype(F32) + hb_ref[rs, :].astype(F32))).astype(BF16)
        branches = ((attn_ref[rs, :], woa_ref), (rg_in, wor_ref), (four_ref[rs, :], wof_ref))
        chunks = []
        for c in range(D // MXU_N):
            acc = None
            for j, (inp, w_ref) in enumerate(branches):
                lo = j * D + c * MXU_N
                y = jnp.dot(inp, w_ref[:, c * MXU_N:(c + 1) * MXU_N], preferred_element_type=F32)
                g = _sigmoid(jnp.dot(u, wm_ref[:, lo:lo + MXU_N], preferred_element_type=F32)
                             + bm_ref[:, lo:lo + MXU_N])
                acc = g * y if acc is None else acc + g * y
            chunks.append(acc.astype(BF16))
        merged = jnp.concatenate(chunks, axis=1)
        y = jnp.dot(merged, wout_ref[...], preferred_element_type=F32)
        x1 = _layer_norm(DEEPNORM_ALPHA * x + mod[2:3] * y, lng_ref[...], lnb_ref[...])
        x1_ref[rs, :] = x1
        u2_ref[rs, :] = (x1 * (1.0 + mod[4:5]) + mod[3:4]).astype(BF16)

    logits = jnp.dot(u2_ref[...], wr_ref[...], preferred_element_type=F32) + br_ref[...]
    lane = lax.broadcasted_iota(jnp.int32, logits.shape, 1)
    vals, idxs = [], []
    for _ in range(TOP_K):
        m = jnp.max(logits, axis=-1, keepdims=True)
        idx = jnp.min(jnp.where(logits == m, lane, LANES), axis=-1, keepdims=True)
        vals.append(m)
        idxs.append(idx)
        logits = jnp.where(lane == idx, -3.0e38, logits)
    exps = [jnp.exp(v - vals[0]) for v in vals]
    den = exps[0] + exps[1] + exps[2] + exps[3]
    eid = jnp.zeros(lane.shape, jnp.int32)
    gate = jnp.zeros(lane.shape, F32)
    member = jnp.zeros(lane.shape, F32)
    for k in range(TOP_K):
        eid = jnp.where(lane == k, idxs[k], eid)
        gate = jnp.where(lane == k, exps[k] / den, gate)
        member = member + jnp.where(lane == idxs[k], 1.0, 0.0)
    eid_ref[...] = eid
    gate_ref[...] = gate
    cnt_ref[...] = jnp.sum(member, axis=0, keepdims=True)


def _merge(x2, mods, attn, z, hf, hb, four, p, seq_len, tm):
    n = x2.shape[0]
    tpb = seq_len // tm
    bm = mods.shape[0]
    mod_map = (lambda i: (i // tpb, 0, 0)) if bm > 1 else (lambda i: (0, 0, 0))
    act = lambda: pl.BlockSpec((tm, D), lambda i: (i, 0))
    return pl.pallas_call(
        _merge_body,
        grid=(n // tm,),
        in_specs=[act(), pl.BlockSpec((None, 6, D), mod_map), act(),
                  pl.BlockSpec((tm, D), lambda i: (i, ZGR // D)), act(), act(), act(),
                  _resident((D, D)), _resident((D, D)), _resident((D, D)),
                  _resident((D, 3 * D)), _resident((1, 3 * D)), _resident((D, D)),
                  _resident((1, D)), _resident((1, D)), _resident((D, LANES)), _resident((1, LANES))],
        out_specs=[act(), act(), pl.BlockSpec((tm, LANES), lambda i: (i, 0)),
                   pl.BlockSpec((tm, LANES), lambda i: (i, 0)),
                   pl.BlockSpec((None, 1, LANES), lambda i: (i, 0, 0))],
        out_shape=[jax.ShapeDtypeStruct((n, D), F32), jax.ShapeDtypeStruct((n, D), BF16),
                   jax.ShapeDtypeStruct((n, LANES), jnp.int32), jax.ShapeDtypeStruct((n, LANES), F32),
                   jax.ShapeDtypeStruct((n // tm, 1, LANES), F32)],
        compiler_params=_params(("parallel",)),
        name="merge_ln_router",
    )(x2, mods, attn, z, hf, hb, four, p["w_o_attn"], p["w_o_rg"], p["w_o_four"], p["w_merge"],
      p["b_merge"], p["w_out"], p["ln1_g"], p["ln1_b"], p["w_router"], p["b_router"])


RUN_ALIGN = 8


def _sorted_rows(td):
    return TOP_K * td + N_EXPERTS * RUN_ALIGN


def _rows_copy(wait, src_ref, dst_ref, src_off, dst_off, length, sem):
    aligned = lambda v: v if isinstance(v, int) else pl.multiple_of(v, RUN_ALIGN)

    @pl.when(length > 0)
    def _():
        n = aligned(length)
        cp = pltpu.make_async_copy(src_ref.at[pl.ds(aligned(src_off), n)],
                                   dst_ref.at[pl.ds(aligned(dst_off), n)], sem)
        if wait:
            cp.wait()
        else:
            cp.start()


def _slot_positions(eid, toff, ltri):
    lane = lax.broadcasted_iota(jnp.int32, eid.shape, 1)
    hits = [eid[:, k:k + 1] == lane for k in range(TOP_K)]
    member = jnp.zeros(eid.shape, F32)
    for h in hits:
        member = member + jnp.where(h, 1.0, 0.0)
    rank = jnp.dot(ltri, member.astype(BF16), preferred_element_type=F32)
    base = toff + rank
    return [jnp.sum(jnp.where(h, base, 0.0), axis=-1, keepdims=True) for h in hits]


def _dispatch_body(rl_ref, to_ref, ro_ref, tlo_ref, tll_ref, nu_ref, tt_ref, u_ref, eid_ref, toff_ref, ltri_ref,
                   xs_hbm, pos_ref, srt, sems, *, td, tb):
    t = pl.program_id(0)
    n_t = pl.num_programs(0)
    slot = t % 2
    rows = srt.shape[1]

    def start_runs(tile, s):
        def one(e, c):
            i = tile * N_EXPERTS + e
            _rows_copy(False, srt.at[s], xs_hbm, to_ref[i], ro_ref[i], rl_ref[i], sems.at[s])
            return c
        lax.fori_loop(0, N_EXPERTS, one, 0)

    @pl.when(t == 0)
    def _():
        srt[1, :tb, :] = jnp.zeros((tb, D), F32)
        for wait in (False, True):
            def one(e, c):
                _rows_copy(wait, srt.at[1], xs_hbm, 0, tlo_ref[e], tll_ref[e], sems.at[1])
                return c
            lax.fori_loop(0, N_EXPERTS, one, 0)
        n_blocks = xs_hbm.shape[0] // tb
        for wait in (False, True):
            def blk(j, c):
                cp = pltpu.make_async_copy(srt.at[1, pl.ds(0, tb)],
                                           xs_hbm.at[pl.ds(pl.multiple_of(j * tb, RUN_ALIGN), tb)], sems.at[1])
                if wait:
                    cp.wait()
                else:
                    cp.start()
                return c
            lax.fori_loop(nu_ref[0], n_blocks, blk, 0)

    pos = _slot_positions(eid_ref[...], toff_ref[...], ltri_ref[...])
    lane = lax.broadcasted_iota(jnp.int32, (td, LANES), 1)
    pos = [p.astype(jnp.int32) for p in pos]
    pos_out = jnp.zeros((td, LANES), jnp.int32)
    for k in range(TOP_K):
        pos_out = jnp.where(lane == k, pos[k], pos_out)
    pos_ref[...] = pos_out

    col = lax.broadcasted_iota(jnp.int32, (td, rows), 1)
    onehot_t = jnp.zeros((td, rows), F32)
    for k in range(TOP_K):
        onehot_t = jnp.where(col == pos[k], 1.0, onehot_t)
    srt[slot] = lax.dot_general(onehot_t.astype(BF16), u_ref[...], (((0,), (0,)), ((), ())),
                                preferred_element_type=F32)
    start_runs(t, slot)

    def wait_runs(tile, s):
        _rows_copy(True, srt.at[s], xs_hbm, 0, 0, tt_ref[tile], sems.at[s])

    @pl.when(t >= 1)
    def _():
        wait_runs(t - 1, 1 - slot)

    @pl.when(t == n_t - 1)
    def _():
        wait_runs(t, slot)


def _dispatch(u2, eid128, rt, td, tb):
    n_tok = u2.shape[0]
    n_t = n_tok // td
    rows = _sorted_rows(td)
    ltri = (jnp.arange(td)[:, None] > jnp.arange(td)[None, :]).astype(BF16)
    grid_spec = pltpu.PrefetchScalarGridSpec(
        num_scalar_prefetch=7,
        grid=(n_t,),
        in_specs=[pl.BlockSpec((td, D), lambda t, *_: (t, 0)),
                  pl.BlockSpec((td, LANES), lambda t, *_: (t, 0)),
                  pl.BlockSpec((None, 1, LANES), lambda t, *_: (t, 0, 0)),
                  pl.BlockSpec((td, td), lambda t, *_: (0, 0), pipeline_mode=pl.Buffered(1))],
        out_specs=[pl.BlockSpec(memory_space=pl.ANY),
                   pl.BlockSpec((td, LANES), lambda t, *_: (t, 0))],
        scratch_shapes=[pltpu.VMEM((2, rows, D), F32), pltpu.SemaphoreType.DMA((2,))],
    )
    return pl.pallas_call(
        functools.partial(_dispatch_body, td=td, tb=tb),
        grid_spec=grid_spec,
        out_shape=[jax.ShapeDtypeStruct((rt["n_slots"], D), F32),
                   jax.ShapeDtypeStruct((n_tok, LANES), jnp.int32)],
        compiler_params=_params(("arbitrary",)),
        name="moe_dispatch",
    )(rt["run_len"], rt["tile_off"], rt["run_off"], rt["tail_off"], rt["tail_len"], rt["n_used"], rt["tile_tot"],
      u2, eid128, rt["toff_f"], ltri)


MOE_ROW_SPLIT = 2


def _moe_body(be_ref, nused_ref, nb_ref, xs_ref, wgu_hbm, bgu_ref, wdn_hbm, bdn_ref, ys_ref,
              wgu_st, wdn_st, wgu_bf, wdn_bf, sems, *, layer):
    i = pl.program_id(0)
    n_used = nused_ref[0]

    def fetch(e):
        return (pltpu.make_async_copy(wgu_hbm.at[layer, e], wgu_st, sems.at[0]),
                pltpu.make_async_copy(wdn_hbm.at[layer, e], wdn_st, sems.at[1]))

    @pl.when(i == 0)
    def _():
        for cp in fetch(be_ref[0]):
            cp.start()

    @pl.when((i < n_used) & ((i == 0) | (be_ref[i] != be_ref[jnp.maximum(i - 1, 0)])))
    def _():
        e = be_ref[i]
        for cp in fetch(e):
            cp.wait()
        wgu_bf[...] = wgu_st[...].astype(BF16)
        wdn_bf[...] = wdn_st[...].astype(BF16)
        nxt = i + nb_ref[e]

        @pl.when(nxt < n_used)
        def _():
            for cp in fetch(be_ref[nxt]):
                cp.start()

    @pl.when(i < n_used)
    def _():
        rows = xs_ref.shape[0] // MOE_ROW_SPLIT
        for s in range(MOE_ROW_SPLIT):
            rs = slice(s * rows, (s + 1) * rows)
            x = xs_ref[rs, :].astype(BF16)
            h = jnp.dot(x, wgu_bf[...], preferred_element_type=F32) + bgu_ref[...]
            half = h.shape[1] // 2
            x_glu = jnp.minimum(h[:, :half], SWIGLU_LIMIT)
            x_lin = jnp.clip(h[:, half:], -SWIGLU_LIMIT, SWIGLU_LIMIT)
            act = (x_glu * _sigmoid(SWIGLU_ALPHA * x_glu) * (x_lin + 1.0)).astype(BF16)
            ys_ref[rs, :] = jnp.dot(act, wdn_bf[...], preferred_element_type=F32) + bdn_ref[...]

    @pl.when(i >= n_used)
    def _():
        ys_ref[...] = jnp.zeros_like(ys_ref)


def _moe(xs, rt, layer, w_gu, b_gu, w_dn, b_dn, tb):
    n_blocks = xs.shape[0] // tb
    d_ff2 = w_gu.shape[3]
    live = lambda i, nu: jnp.minimum(i, nu[0] - 1)
    grid_spec = pltpu.PrefetchScalarGridSpec(
        num_scalar_prefetch=3,
        grid=(n_blocks,),
        in_specs=[pl.BlockSpec((tb, D), lambda i, be, nu, nb: (live(i, nu), 0)),
                  pl.BlockSpec(memory_space=pl.ANY),
                  pl.BlockSpec((None, None, 1, d_ff2), lambda i, be, nu, nb: (layer, be[live(i, nu)], 0, 0)),
                  pl.BlockSpec(memory_space=pl.ANY),
                  pl.BlockSpec((None, None, 1, D), lambda i, be, nu, nb: (layer, be[live(i, nu)], 0, 0))],
        out_specs=pl.BlockSpec((tb, D), lambda i, be, nu, nb: (i, 0)),
        scratch_shapes=[pltpu.VMEM((D, d_ff2), F32), pltpu.VMEM((d_ff2 // 2, D), F32),
                        pltpu.VMEM((D, d_ff2), BF16), pltpu.VMEM((d_ff2 // 2, D), BF16),
                        pltpu.SemaphoreType.DMA((2,))],
    )
    return pl.pallas_call(
        functools.partial(_moe_body, layer=layer),
        grid_spec=grid_spec,
        out_shape=jax.ShapeDtypeStruct(xs.shape, F32),
        compiler_params=_params(("arbitrary",)),
        name="moe_experts",
    )(rt["block_e"], rt["n_used"], rt["blocks_per_e"], xs, w_gu, b_gu, w_dn, b_dn)


def _route(cnt128, n_tok, td, tb):
    n_t = n_tok // td
    cnt = cnt128[:, 0, :N_EXPERTS].astype(jnp.int32)
    run = (cnt + RUN_ALIGN - 1) // RUN_ALIGN * RUN_ALIGN
    tile_off = jnp.cumsum(run, axis=1) - run
    total = run.sum(axis=0)
    padded = (total + tb - 1) // tb * tb
    pend = jnp.cumsum(padded)
    pstart = pend - padded
    run_off = pstart[None, :] + jnp.cumsum(run, axis=0) - run
    worst = TOP_K * n_tok + n_t * N_EXPERTS * RUN_ALIGN + N_EXPERTS * tb
    n_slots = (worst + tb - 1) // tb * tb
    first_slot = jnp.arange(n_slots // tb, dtype=jnp.int32) * tb
    block_e = jnp.minimum(jnp.sum((pend[None, :] <= first_slot[:, None]).astype(jnp.int32), axis=1),
                          N_EXPERTS - 1)
    toff_f = jnp.zeros((n_t, 1, LANES), F32).at[:, 0, :N_EXPERTS].set(tile_off.astype(F32))
    i32 = lambda a: a.astype(jnp.int32).reshape(-1)
    return dict(run_len=i32(run), tile_off=i32(tile_off), run_off=i32(run_off),
                tail_off=i32(pstart + total), tail_len=i32(padded - total), toff_f=toff_f,
                tile_tot=i32(run.sum(axis=1)), blocks_per_e=i32(padded // tb),
                block_e=i32(block_e), n_used=i32(pend[-1] // tb), n_slots=n_slots)


def _combine_body(rl_ref, to_ref, ro_ref, tt_ref, ys_hbm, x1_ref, mod_ref, gate_ref, pos_ref, lng_ref, lnb_ref,
                  o_ref, srt, sems, *, td):
    t = pl.program_id(0)
    n_t = pl.num_programs(0)
    slot = t % 2
    rows = srt.shape[1]

    def start_runs(tile, s):
        def one(e, c):
            i = tile * N_EXPERTS + e
            _rows_copy(False, ys_hbm, srt.at[s], ro_ref[i], to_ref[i], rl_ref[i], sems.at[s])
            return c
        lax.fori_loop(0, N_EXPERTS, one, 0)

    @pl.when(t == 0)
    def _():
        srt[...] = jnp.zeros_like(srt)
        start_runs(0, 0)

    @pl.when(t + 1 < n_t)
    def _():
        start_runs(t + 1, 1 - slot)

    _rows_copy(True, ys_hbm, srt.at[slot], 0, 0, tt_ref[t], sems.at[slot])

    gate = gate_ref[...]
    pos = pos_ref[...]
    col = lax.broadcasted_iota(jnp.int32, (td, rows), 1)
    weights = jnp.zeros((td, rows), F32)
    for k in range(TOP_K):
        weights = jnp.where(col == pos[:, k:k + 1], gate[:, k:k + 1], weights)
    f = jnp.dot(weights.astype(BF16), srt[slot].astype(BF16), preferred_element_type=F32)
    mod = mod_ref[...]
    o_ref[...] = _layer_norm(DEEPNORM_ALPHA * x1_ref[...] + mod[5:6] * f, lng_ref[...], lnb_ref[...])


def _combine(x1, mods, gate128, pos128, ys, rt, ln_g, ln_b, seq_len, td):
    n = x1.shape[0]
    tpb = seq_len // td
    bm = mods.shape[0]
    mod_map = (lambda i, *_: (i // tpb, 0, 0)) if bm > 1 else (lambda i, *_: (0, 0, 0))
    grid_spec = pltpu.PrefetchScalarGridSpec(
        num_scalar_prefetch=4,
        grid=(n // td,),
        in_specs=[pl.BlockSpec(memory_space=pl.ANY),
                  pl.BlockSpec((td, D), lambda i, *_: (i, 0)),
                  pl.BlockSpec((None, 6, D), mod_map),
                  pl.BlockSpec((td, LANES), lambda i, *_: (i, 0)),
                  pl.BlockSpec((td, LANES), lambda i, *_: (i, 0)),
                  pl.BlockSpec((1, D), lambda i, *_: (0, 0)),
                  pl.BlockSpec((1, D), lambda i, *_: (0, 0))],
        out_specs=pl.BlockSpec((td, D), lambda i, *_: (i, 0)),
        scratch_shapes=[pltpu.VMEM((2, _sorted_rows(td), D), F32), pltpu.SemaphoreType.DMA((2,))],
    )
    return pl.pallas_call(
        functools.partial(_combine_body, td=td),
        grid_spec=grid_spec,
        out_shape=jax.ShapeDtypeStruct((n, D), F32),
        compiler_params=_params(("arbitrary",)),
        name="combine_ln",
    )(rt["run_len"], rt["tile_off"], rt["run_off"], rt["tile_tot"], ys, x1, mods, gate128, pos128,
      ln_g, ln_b)


def _rope_tables(seq_len):
    pos = jnp.arange(seq_len, dtype=jnp.int32)
    row = (pos // GRID_W).astype(F32)
    col = (pos % GRID_W).astype(F32)
    n_freq = HEAD_DIM // 4
    freqs = ROPE_BASE ** (-jnp.arange(n_freq, dtype=F32) / n_freq)
    ar, ac = row[:, None] * freqs, col[:, None] * freqs
    zero = jnp.zeros_like(ar)
    c = jnp.concatenate([jnp.cos(ar), jnp.cos(ar), jnp.cos(ac), jnp.cos(ac)], axis=1)
    sa = jnp.concatenate([-jnp.sin(ar), zero, -jnp.sin(ac), zero], axis=1)
    sb = jnp.concatenate([zero, jnp.sin(ar), zero, jnp.sin(ac)], axis=1)
    rep = MXU_N // HEAD_DIM
    return tuple(jnp.tile(t, (1, rep)) for t in (c, sa, sb))


def _layer_params(l, w):
    w_router = jnp.zeros((D, LANES), BF16).at[:, :N_EXPERTS].set(w["w_router"][l].astype(BF16))
    b_router = jnp.full((1, LANES), NEG_INF, F32).at[0, :N_EXPERTS].set(w["b_router"][l])
    return dict(
        w_in=w["w_in"][l].astype(BF16),
        w_o_attn=w["w_o_attn"][l].astype(BF16), w_o_rg=w["w_o_rg"][l].astype(BF16),
        w_o_four=w["w_o_four"][l].astype(BF16), w_merge=w["w_merge"][l].astype(BF16),
        b_merge=w["b_merge"][l].reshape(1, 3 * D), w_out=w["w_out"][l].astype(BF16),
        ln1_g=w["ln1_g"][l].reshape(1, D), ln1_b=w["ln1_b"][l].reshape(1, D),
        ln2_g=w["ln2_g"][l].reshape(1, D), ln2_b=w["ln2_b"][l].reshape(1, D),
        w_router=w_router, b_router=b_router,
        layer=l, w_gu=w["w_gate_up"], b_gu=w["b_gate_up"].reshape(DEPTH, N_EXPERTS, 1, -1),
        w_dn=w["w_down"], b_dn=w["b_down"].reshape(DEPTH, N_EXPERTS, 1, D),
        wg=_gate_weights(w["rg_w_a"][l], w["rg_w_i"][l]),
        conv_w=w["conv_w"][l], conv_b=w["conv_b"][l], b_a=w["rg_b_a"][l], b_i=w["rg_b_i"][l],
        lam=w["rg_lambda"][l], sink=w["attn_sink"][l],
    )


def _row_tile(seq_len, want):
    return min(want, seq_len)


def _ffn(x1, u2, eid128, gate128, cnt128, mods, p, seq_len, tb, td):
    rt = _route(cnt128, x1.shape[0], td, tb)
    xs, pos128 = _dispatch(u2, eid128, rt, td, tb)
    ys = _moe(xs, rt, p["layer"], p["w_gu"], p["b_gu"], p["w_dn"], p["b_dn"], tb)
    return _combine(x1, mods, gate128, pos128, ys, rt, p["ln2_g"], p["ln2_b"], seq_len, td)


def kernel(x, c, ctx, c_ctx, w_mod, b_mod, w_in, attn_sink, w_o_attn, conv_w, conv_b, rg_w_a, rg_b_a, rg_w_i, rg_b_i, rg_lambda, w_o_rg, w_o_four, w_merge, b_merge, w_out, ln1_g, ln1_b, w_router, b_router, w_gate_up, b_gate_up, w_down, b_down, ln2_g, ln2_b):
    w = dict(w_mod=w_mod, b_mod=b_mod, w_in=w_in, attn_sink=attn_sink, w_o_attn=w_o_attn, conv_w=conv_w,
             conv_b=conv_b, rg_w_a=rg_w_a, rg_b_a=rg_b_a, rg_w_i=rg_w_i, rg_b_i=rg_b_i, rg_lambda=rg_lambda,
             w_o_rg=w_o_rg, w_o_four=w_o_four, w_merge=w_merge, b_merge=b_merge, w_out=w_out, ln1_g=ln1_g,
             ln1_b=ln1_b, w_router=w_router, b_router=b_router, w_gate_up=w_gate_up, b_gate_up=b_gate_up,
             w_down=w_down, b_down=b_down, ln2_g=ln2_g, ln2_b=ln2_b)
    bsz, seq_len, _ = x.shape
    ctx_len = ctx.shape[1]
    n_lat, n_ctx = bsz * seq_len, bsz * ctx_len
    tm_lat, tm_ctx = _row_tile(seq_len, 512), _row_tile(ctx_len, 256)
    dft_lat, dft_ctx = _row_tile(seq_len, 1024), _row_tile(ctx_len, 256)
    rg_lat, rg_ctx = _row_tile(seq_len, 1024), _row_tile(ctx_len, 256)
    tb_lat, tb_ctx = 512, 128

    rope = _rope_tables(seq_len)
    cw, sw = _dft_mats(F_GW, F_GW ** -0.5)
    csw = jnp.concatenate([cw, sw], axis=1).astype(BF16)
    use_fft = seq_len % (FFT_L1 * FFT_GRP) == 0
    if use_fft:
        fft_tabs = _fft_tables(seq_len)
    else:
        cl, sl = _dft_mats(seq_len, seq_len ** -0.5)
        cl, nsl = cl.astype(BF16), (-sl).astype(BF16)
    cc, sc = _dft_mats(ctx_len, ctx_len ** -0.5)
    cc, nsc = cc.astype(BF16), (-sc).astype(BF16)

    cond = jnp.zeros((16, D), F32).at[:bsz].set(c).at[bsz].set(c_ctx)
    x2 = x.reshape(n_lat, D)
    ctx2 = ctx.reshape(n_ctx, D)
    zero_h = jnp.zeros((bsz, 1, D), F32)

    for l in range(DEPTH):
        last = l == DEPTH - 1
        p = _layer_params(l, w)
        m = _adaln(cond, w_mod[l], b_mod[l]).reshape(16, 6, D)
        mods_lat, mods_ctx = m[:bsz], m[bsz:bsz + 1]

        zc = _proj(ctx2, mods_ctx, p["w_in"], csw, None, ctx_len, tm_ctx)
        z = _proj(x2, mods_lat, p["w_in"], csw, rope, seq_len, tm_lat)
        zc3 = zc.reshape(bsz, ctx_len, Z_COLS)
        z3 = z.reshape(bsz, seq_len, Z_COLS)

        rg = lambda zz, h0, d, tile: _rglru(zz, p["conv_w"], p["conv_b"], p["wg"], p["b_a"], p["b_i"],
                                            p["lam"], h0, direction=d, tile=tile)
        hcf, endf = rg(zc3, zero_h, 0, rg_ctx)
        hcb, endb = rg(zc3, zero_h, 1, rg_ctx)
        hf, _ = rg(z3, endf, 0, rg_lat)
        hb, _ = rg(z3, endb, 1, rg_lat)

        attn = _attention(z3, zc3, p["sink"])
        four = _seq_fft(z, bsz, seq_len, fft_tabs) if use_fft else _seq_dft(z3, cl, nsl, dft_lat)
        x1, u2, eid, gate, cnt = _merge(x2, mods_lat, attn.reshape(n_lat, D), z, hf.reshape(n_lat, D),
                                   hb.reshape(n_lat, D), four.reshape(n_lat, D), p, seq_len, tm_lat)
        x2 = _ffn(x1, u2, eid, gate, cnt, mods_lat, p, seq_len, tb_lat, tm_lat)

        if not last:
            attn_c = _ctx_attention(zc3, p["sink"])
            four_c = _seq_dft(zc3, cc, nsc, dft_ctx)
            c1, uc2, eid_c, gate_c, cnt_c = _merge(ctx2, mods_ctx, attn_c.reshape(n_ctx, D), zc,
                                            hcf.reshape(n_ctx, D), hcb.reshape(n_ctx, D),
                                            four_c.reshape(n_ctx, D), p, ctx_len, tm_ctx)
            ctx2 = _ffn(c1, uc2, eid_c, gate_c, cnt_c, mods_ctx, p, ctx_len, tb_ctx, tm_ctx)

    return x2.reshape(bsz, seq_len, D)
```
